```python
import jax, jax.numpy as jnp
from jax import lax
import numpy as np

D_MODEL = 2048
BATCH = 8
SEQ = 2048
DEPTH = 2

N_META = 16
RET_HEADS = 8
RET_DK = 128
RET_DV = D_MODEL // RET_HEADS
RET_CHUNK = 128
HG_HEADS = 8
HG_DK = 128
HG_DV = D_MODEL // HG_HEADS
HG_CHUNK = 16
D_FF = -(-8 * D_MODEL // (3 * 256)) * 256
RMS_EPS = 1e-6
ROPE_BASE = 10000.0
PAD = RET_CHUNK - N_META

RET_QK = RET_HEADS * RET_DK
RET_V = RET_HEADS * RET_DV
HG_QK = HG_HEADS * HG_DK
HG_V = HG_HEADS * HG_DV
IN_WIDTHS = (RET_QK, RET_QK, RET_V, RET_V, HG_QK, HG_QK, HG_V, HG_V, D_MODEL, D_MODEL)
IN_COLS = sum(IN_WIDTHS)
SPLIT_POINTS = tuple(sum(IN_WIDTHS[:i + 1]) for i in range(len(IN_WIDTHS) - 1))

kernel_name = "hybrid_retention_hgrn2_gated_block"


def rms_norm(x, w):
    xf = x.astype(jnp.float32)
    y = xf * lax.rsqrt(jnp.mean(xf * xf, axis=-1, keepdims=True) + RMS_EPS)
    return (y * w.astype(jnp.float32)).astype(x.dtype)


def group_rms_norm(x):
    xf = x.astype(jnp.float32)
    y = xf * lax.rsqrt(jnp.mean(xf * xf, axis=-1, keepdims=True) + RMS_EPS)
    return y.astype(x.dtype)


def rotary(x, pos):
    half = x.shape[-1] // 2
    inv = ROPE_BASE ** (-jnp.arange(half, dtype=jnp.float32) / half)
    ang = pos.astype(jnp.float32)[:, None] * inv[None, :]
    cos = jnp.cos(ang)[None, :, None, :]
    sin = jnp.sin(ang)[None, :, None, :]
    xf = x.astype(jnp.float32)
    x1, x2 = xf[..., :half], xf[..., half:]
    return jnp.concatenate([x1 * cos - x2 * sin, x1 * sin + x2 * cos], axis=-1).astype(x.dtype)


def retention(q, k, v):
    B, L = q.shape[0], q.shape[1]
    C = RET_CHUNK
    N = L // C
    dt = q.dtype
    log_g = jnp.log1p(-jnp.exp2(-5.0 - jnp.arange(RET_HEADS, dtype=jnp.float32)))
    qc = q.reshape(B, N, C, RET_HEADS, RET_DK)
    kc = k.reshape(B, N, C, RET_HEADS, RET_DK)
    vc = v.reshape(B, N, C, RET_HEADS, RET_DV)
    idx = jnp.arange(C, dtype=jnp.float32)
    diff = idx[:, None] - idx[None, :]
    decay_intra = jnp.where(diff[None] >= 0,
                            jnp.exp(jnp.maximum(diff, 0.0)[None] * log_g[:, None, None]), 0.0)
    scores = jnp.einsum('bnchd,bnmhd->bnhcm', qc, kc) * decay_intra.astype(dt)[None, None]
    y_intra = jnp.einsum('bnhcm,bnmhe->bnche', scores, vc)
    k_dec = jnp.exp((C - 1 - idx)[:, None] * log_g[None, :]).astype(dt)
    delta = jnp.einsum('bnchd,ch,bnche->nbhde', kc, k_dec, vc)
    chunk_decay = jnp.exp(C * log_g).astype(delta.dtype)[None, :, None, None]

    def step(S, d):
        return S * chunk_decay + d, S

    _, S_prev = lax.scan(step, jnp.zeros(delta.shape[1:], delta.dtype), delta)
    q_dec = jnp.exp((idx + 1)[:, None] * log_g[None, :]).astype(dt)
    y_inter = jnp.einsum('bnchd,ch,nbhde->bnche', qc, q_dec, S_prev)
    return (y_intra + y_inter).reshape(B, L, RET_HEADS, RET_DV)


def hgrn2(q, log_f, k, v):
    B, L = q.shape[0], q.shape[1]
    C = HG_CHUNK
    N = L // C

    def chunks(a):
        return jnp.moveaxis(a.reshape((B, N, C) + a.shape[2:]), 1, 0)

    qc, kc, vc = chunks(q), chunks(k), chunks(v)
    bc_all = lax.cumsum(chunks(log_f), axis=2)
    mask = (jnp.arange(C)[:, None] >= jnp.arange(C)[None, :])[None, :, :, None, None]

    def step(S, xs):
        qt, kt, vt, bt = xs
        dt = qt.dtype
        rel = bt[:, :, None] - bt[:, None, :]
        decay = jnp.where(mask, jnp.exp(jnp.where(mask, rel, 0.0)), 0.0).astype(dt)
        A = jnp.einsum('btshd,bthd,bshd->bhts', decay, qt, kt)
        y_intra = jnp.einsum('bhts,bshe->bthe', A, vt)
        y_inter = jnp.einsum('bthd,bhde->bthe', qt * jnp.exp(bt).astype(dt), S)
        b_last = bt[:, -1]
        k_end = kt * jnp.exp(b_last[:, None] - bt).astype(dt)
        S_new = S * jnp.exp(b_last)[..., None].astype(S.dtype) + jnp.einsum('bshd,bshe->bhde', k_end, vt)
        return S_new, y_intra + y_inter

    S0 = jnp.zeros((B, HG_HEADS, HG_DK, HG_DV), v.dtype)
    _, ys = lax.scan(step, S0, (qc, kc, vc, bc_all))
    return jnp.moveaxis(ys, 0, 1).reshape(B, L, HG_HEADS, HG_DV)


def mixer(h, pos, valid, lb, w_in, hg_norm_w, w_br_ret, w_br_hg, w_out):
    B, L, _ = h.shape
    dt = h.dtype
    proj = h @ w_in
    rq, rk, rv, rg, hq, hf, hi, hg, gate_ret, gate_hg = jnp.split(proj, SPLIT_POINTS, axis=-1)
    vmask = valid[None, :, None, None]

    rq = rotary(rq.reshape(B, L, RET_HEADS, RET_DK), pos)
    rk = rotary(rk.reshape(B, L, RET_HEADS, RET_DK), pos) * (RET_DK ** -0.5)
    rk = jnp.where(vmask, rk, jnp.zeros_like(rk))
    yr = retention(rq, rk, rv.reshape(B, L, RET_HEADS, RET_DV))
    yr = group_rms_norm(yr).reshape(B, L, RET_V) * jax.nn.silu(rg)
    yr = yr @ w_br_ret

    z = hf.astype(jnp.float32).reshape(B, L, HG_HEADS, HG_DK)
    lbh = lb.reshape(HG_HEADS, HG_DK)
    one_minus_f = (1.0 - lbh) * jax.nn.sigmoid(-z)
    log_f = jnp.log1p(-one_minus_f)
    log_f = jnp.where(vmask, log_f, 0.0)
    kin = jnp.where(vmask, one_minus_f, 0.0).astype(dt)
    qh = jax.nn.silu(hq).reshape(B, L, HG_HEADS, HG_DK)
    yh = hgrn2(qh, log_f, kin, hi.reshape(B, L, HG_HEADS, HG_DV))
    yh = rms_norm(yh, hg_norm_w).reshape(B, L, HG_V) * jax.nn.silu(hg)
    yh = yh @ w_br_hg

    y = jax.nn.sigmoid(gate_ret) * yr + jax.nn.sigmoid(gate_hg) * yh
    return y @ w_out


def swiglu(h, w_gate, w_up, w_down):
    return (jax.nn.silu(h @ w_gate) * (h @ w_up)) @ w_down


def setup_inputs(seed: int = 0) -> dict:
    key = jax.random.key(seed)
    ks = jax.random.split(key, 16)
    f32 = jnp.float32

    def normal(k, shape, scale):
        return jax.random.normal(k, shape, f32) * scale

    def gains(k, shape):
        return 1.0 + 0.1 * jax.random.normal(k, shape, f32)

    return {
        "x": normal(ks[0], (BATCH, SEQ, D_MODEL), 1.0),
        "meta_tokens": normal(ks[1], (N_META, D_MODEL), 1.0),
        "norm_mix_pre": gains(ks[2], (DEPTH, D_MODEL)),
        "norm_mix_post": gains(ks[3], (DEPTH, D_MODEL)),
        "norm_ffn_pre": gains(ks[4], (DEPTH, D_MODEL)),
        "norm_ffn_post": gains(ks[5], (DEPTH, D_MODEL)),
        "w_in": normal(ks[6], (DEPTH, D_MODEL, IN_COLS), D_MODEL ** -0.5),
        "hg_lb_logits": normal(ks[7], (DEPTH, HG_QK), 0.1),
        "hg_norm_w": gains(ks[8], (DEPTH, HG_DV)),
        "w_br_ret": normal(ks[9], (DEPTH, RET_V, D_MODEL), RET_V ** -0.5),
        "w_br_hg": normal(ks[10], (DEPTH, HG_V, D_MODEL), HG_V ** -0.5),
        "w_out": normal(ks[11], (DEPTH, D_MODEL, D_MODEL), D_MODEL ** -0.5),
        "w_ffn_gate": normal(ks[12], (DEPTH, D_MODEL, D_FF), D_MODEL ** -0.5),
        "w_ffn_up": normal(ks[13], (DEPTH, D_MODEL, D_FF), D_MODEL ** -0.5),
        "w_ffn_down": normal(ks[14], (DEPTH, D_FF, D_MODEL), D_FF ** -0.5),
    }


def reference(x, meta_tokens, norm_mix_pre, norm_mix_post, norm_ffn_pre, norm_ffn_post, w_in, hg_lb_logits,
              hg_norm_w, w_br_ret, w_br_hg, w_out, w_ffn_gate, w_ffn_up, w_ffn_down):
    B = x.shape[0]
    meta = jnp.broadcast_to(meta_tokens.astype(x.dtype)[None], (B, N_META, D_MODEL))
    pad = jnp.zeros((B, PAD, D_MODEL), x.dtype)
    h = jnp.concatenate([pad, meta, x], axis=1)
    L = h.shape[1]
    pos = jnp.arange(L, dtype=jnp.int32) - PAD
    valid = pos >= 0

    lb_sm = jax.nn.softmax(hg_lb_logits.astype(jnp.float32), axis=0)
    lbs = jnp.cumsum(lb_sm, axis=0) - lb_sm[0:1]

    for l in range(DEPTH):
        m = mixer(rms_norm(h, norm_mix_pre[l]), pos, valid, lbs[l], w_in[l], hg_norm_w[l],
                  w_br_ret[l], w_br_hg[l], w_out[l])
        h = h + rms_norm(m, norm_mix_post[l])
        f = swiglu(rms_norm(h, norm_ffn_pre[l]), w_ffn_gate[l], w_ffn_up[l], w_ffn_down[l])
        h = h + rms_norm(f, norm_ffn_post[l])
    return h[:, PAD + N_META:]
```

```python
import functools

import numpy as np
import jax
import jax.numpy as jnp
from jax import lax
from jax.experimental import pallas as pl
from jax.experimental.pallas import tpu as pltpu

N_META = 16
HEADS = 8
DK = 128
DV = 256
CHUNK = 128
PAD = CHUNK - N_META
RMS_EPS = 1e-6
ROPE_BASE = 10000.0
LEVELS = 7

VMEM_LIMIT_BYTES = 56 * 1024 * 1024
ROW_TILE = 1024
HEADS_PER_STEP = 2

F32 = jnp.float32
BF16 = jnp.bfloat16
NT_DIMS = (((1,), (1,)), ((), ()))
TN_DIMS = (((0,), (0,)), ((), ()))


def _params(*semantics):
    return pltpu.CompilerParams(dimension_semantics=semantics, vmem_limit_bytes=VMEM_LIMIT_BYTES)


def _sigmoid(x):
    return 1.0 / (1.0 + jnp.exp(-x))


def _rms_scale(x):
    return x * lax.rsqrt(jnp.mean(x * x, axis=-1, keepdims=True) + RMS_EPS)


def _norm_kernel(h_ref, w_ref, o_ref):
    o_ref[...] = (_rms_scale(h_ref[...]) * w_ref[...]).astype(o_ref.dtype)


def _norm_call(h, w_row):
    rows, d = h.shape
    tm = min(rows, 512)
    return pl.pallas_call(
        _norm_kernel,
        grid=(rows // tm,),
        in_specs=[pl.BlockSpec((tm, d), lambda i: (i, 0)), pl.BlockSpec((1, d), lambda i: (0, 0))],
        out_specs=pl.BlockSpec((tm, d), lambda i: (i, 0)),
        out_shape=jax.ShapeDtypeStruct((rows, d), BF16),
        compiler_params=_params("parallel"),
        name="rms_norm",
    )(h, w_row)


def _mm_kernel(a_ref, w_ref, o_ref):
    o_ref[...] = jnp.dot(a_ref[...], w_ref[...], preferred_element_type=F32).astype(o_ref.dtype)


def _proj_call(a, w, layer):
    rows, k = a.shape
    n = w.shape[-1]
    tm = min(rows, ROW_TILE)
    tn = 1024
    return pl.pallas_call(
        _mm_kernel,
        grid=(rows // tm, n // tn),
        in_specs=[pl.BlockSpec((tm, k), lambda i, j: (i, 0)),
                  pl.BlockSpec((None, k, tn), lambda i, j: (layer, 0, j))],
        out_specs=pl.BlockSpec((tm, tn), lambda i, j: (i, j)),
        out_shape=jax.ShapeDtypeStruct((rows, n), BF16),
        compiler_params=_params("parallel", "arbitrary"),
        name="in_proj",
    )(a, w)


def _ret_kernel(q_ref, k_ref, v_ref, g_ref, cos_ref, sin_ref, dmat_ref, qdec_ref, kdec_ref, cdec_ref,
                s0_ref, o_ref, sfin_ref, s_ref, *, hb, nchunks, pos0):
    s_ref[...] = s0_ref[...]

    def body(c, carry):
        r0 = pl.multiple_of(c * CHUNK, CHUNK)
        rows = pl.ds(r0, CHUNK)
        cos = cos_ref[rows, :]
        sin = sin_ref[rows, :]
        if pos0 < 0:
            valid = (lax.broadcasted_iota(jnp.int32, (CHUNK, 1), 0) + (r0 + pos0)) >= 0
        for j in range(hb):
            q = q_ref[rows, j * DK:(j + 1) * DK].astype(F32)
            k = k_ref[rows, j * DK:(j + 1) * DK].astype(F32)
            qr = q * cos + pltpu.roll(q, DK // 2, 1) * sin
            kr = (k * cos + pltpu.roll(k, DK // 2, 1) * sin) * (DK ** -0.5)
            if pos0 < 0:
                kr = jnp.where(valid, kr, 0.0)
            v = v_ref[rows, j * DV:(j + 1) * DV]
            scores = lax.dot_general(qr.astype(BF16), kr.astype(BF16), NT_DIMS,
                                     preferred_element_type=F32) * dmat_ref[j]
            s = s_ref[j]
            y = jnp.dot(scores.astype(BF16), v, preferred_element_type=F32)
            y = y + jnp.dot((qr * qdec_ref[j]).astype(BF16), s.astype(BF16), preferred_element_type=F32)
            s_ref[j] = s * cdec_ref[j] + lax.dot_general((kr * kdec_ref[j]).astype(BF16), v, TN_DIMS,
                                                         preferred_element_type=F32)
            g = g_ref[rows, j * DV:(j + 1) * DV].astype(F32)
            o_ref[rows, j * DV:(j + 1) * DV] = (_rms_scale(y) * (g * _sigmoid(g))).astype(o_ref.dtype)
        return carry

    lax.fori_loop(0, nchunks, body, 0)
    sfin_ref[...] = s_ref[...]


def _ret_tables(pos):
    half = DK // 2
    inv = ROPE_BASE ** (-jnp.arange(half, dtype=F32) / half)
    ang = pos.astype(F32)[:, None] * inv[None, :]
    cos, sin = jnp.cos(ang), jnp.sin(ang)
    cos2 = jnp.concatenate([cos, cos], axis=1)
    sin2 = jnp.concatenate([-sin, sin], axis=1)
    log_g = jnp.log1p(-jnp.exp2(-5.0 - jnp.arange(HEADS, dtype=F32)))
    idx = jnp.arange(CHUNK, dtype=F32)
    diff = idx[:, None] - idx[None, :]
    dmat = jnp.where(diff[None] >= 0, jnp.exp(jnp.maximum(diff, 0.0)[None] * log_g[:, None, None]), 0.0)
    qdec = jnp.exp((idx + 1)[None, :] * log_g[:, None])
    kdec = jnp.exp((CHUNK - 1 - idx)[None, :] * log_g[:, None])
    cdec = jnp.exp(CHUNK * log_g)
    qdec = jnp.broadcast_to(qdec[:, :, None], (HEADS, CHUNK, DK))
    kdec = jnp.broadcast_to(kdec[:, :, None], (HEADS, CHUNK, DK))
    cdec = jnp.broadcast_to(cdec[:, None, None], (HEADS, 1, DV))
    return cos2, sin2, dmat, qdec, kdec, cdec


def _ret_call(proj, tables, s0, batch, seq, pos0):
    cos2, sin2, dmat, qdec, kdec, cdec = tables
    hb = HEADS_PER_STEP
    nchunks = seq // CHUNK
    qk_w, v_w = HEADS * DK, HEADS * DV
    qb, kb = 0, qk_w // (hb * DK)
    vb, gb = 2 * qk_w // (hb * DV), (2 * qk_w + v_w) // (hb * DV)
    kern = functools.partial(_ret_kernel, hb=hb, nchunks=nchunks, pos0=pos0)
    const2 = lambda b, h: (0, 0)
    per_head = lambda b, h: (h, 0, 0)
    return pl.pallas_call(
        kern,
        grid=(batch, HEADS // hb),
        in_specs=[
            pl.BlockSpec((seq, hb * DK), lambda b, h: (b, qb + h)),
            pl.BlockSpec((seq, hb * DK), lambda b, h: (b, kb + h)),
            pl.BlockSpec((seq, hb * DV), lambda b, h: (b, vb + h)),
            pl.BlockSpec((seq, hb * DV), lambda b, h: (b, gb + h)),
            pl.BlockSpec((seq, DK), const2),
            pl.BlockSpec((seq, DK), const2),
            pl.BlockSpec((hb, CHUNK, CHUNK), per_head),
            pl.BlockSpec((hb, CHUNK, DK), per_head),
            pl.BlockSpec((hb, CHUNK, DK), per_head),
            pl.BlockSpec((hb, 1, DV), per_head),
            pl.BlockSpec((hb, DK, DV), per_head),
        ],
        out_specs=[
            pl.BlockSpec((seq, hb * DV), lambda b, h: (b, h)),
            pl.BlockSpec((None, hb, DK, DV), lambda b, h: (b, h, 0, 0)),
        ],
        out_shape=[jax.ShapeDtypeStruct((batch * seq, v_w), BF16),
                   jax.ShapeDtypeStruct((batch, HEADS, DK, DV), F32)],
        scratch_shapes=[pltpu.VMEM((hb, DK, DV), F32)],
        compiler_params=_params("parallel", "parallel"),
        name="retention",
    )(proj, proj, proj, proj, cos2, sin2, dmat, qdec, kdec, cdec, s0)


def _hg_tables():
    t = np.arange(CHUNK)[:, None]
    u = np.arange(CHUNK)[None, :]
    mats, masks = [], []
    for j in range(LEVELS):
        m = 1 << j
        upper = ((t >> j) & 1) == 1
        q_part = upper & (u >= (t & ~(m - 1))) & (u <= t)
        k_part = (~upper) & (u > t) & (u <= (t | (m - 1)))
        mats.append(q_part | k_part)
        masks.append(((t >> (j + 1)) == (u >> (j + 1))) & upper & (((u >> j) & 1) == 0))
    mats.append(u <= t)
    masks.append(t == u)
    mstack = jnp.asarray(np.concatenate(mats, axis=0).astype(np.float32), dtype=BF16)
    return mstack, jnp.asarray(np.stack(masks).astype(np.float32))


def _hg_kernel(q_ref, f_ref, i_ref, g_ref, lb_ref, nw_ref, mstack_ref, masks_ref, s0_ref,
               o_ref, sfin_ref, st_ref, *, hb, nchunks, pos0):
    st_ref[...] = s0_ref[...]

    def body(c, carry):
        r0 = pl.multiple_of(c * CHUNK, CHUNK)
        rows = pl.ds(r0, CHUNK)
        if pos0 < 0:
            valid = (lax.broadcasted_iota(jnp.int32, (CHUNK, 1), 0) + (r0 + pos0)) >= 0
        for j in range(hb):
            z = f_ref[rows, j * DK:(j + 1) * DK].astype(F32)
            one_minus_f = (1.0 - lb_ref[:, j * DK:(j + 1) * DK]) / (1.0 + jnp.exp(z))
            log_f = jnp.log1p(-one_minus_f)
            kin = one_minus_f
            if pos0 < 0:
                log_f = jnp.where(valid, log_f, 0.0)
                kin = jnp.where(valid, kin, 0.0)
            hi = log_f.astype(BF16)
            lo = (log_f - hi.astype(F32)).astype(BF16)
            parts = jnp.dot(mstack_ref[...], jnp.concatenate([hi, lo], axis=1), preferred_element_type=F32)
            expo = parts[:, :DK] + parts[:, DK:]
            x = q_ref[rows, j * DK:(j + 1) * DK].astype(F32)
            q = x * _sigmoid(x)
            a = lax.dot_general(q.astype(BF16), kin.astype(BF16), NT_DIMS,
                                preferred_element_type=F32) * masks_ref[LEVELS]
            for lev in range(LEVELS):
                e = jnp.exp(expo[lev * CHUNK:(lev + 1) * CHUNK])
                a = a + lax.dot_general((q * e).astype(BF16), (kin * e).astype(BF16), NT_DIMS,
                                        preferred_element_type=F32) * masks_ref[lev]
            b = expo[LEVELS * CHUNK:]
            total = b[CHUNK - 1:CHUNK, :]
            v = i_ref[rows, j * DV:(j + 1) * DV]
            st = st_ref[j]
            y = jnp.dot(a.astype(BF16), v, preferred_element_type=F32)
            y = y + lax.dot_general((q * jnp.exp(b)).astype(BF16), st.astype(BF16), NT_DIMS,
                                    preferred_element_type=F32)
            k_end = (kin * jnp.exp(total - b)).astype(BF16)
            st_ref[j] = st * jnp.exp(total) + lax.dot_general(v, k_end, TN_DIMS, preferred_element_type=F32)
            g = g_ref[rows, j * DV:(j + 1) * DV].astype(F32)
            o_ref[rows, j * DV:(j + 1) * DV] = (_rms_scale(y) * nw_ref[...] * (g * _sigmoid(g))).astype(o_ref.dtype)
        return carry

    lax.fori_loop(0, nchunks, body, 0)
    sfin_ref[...] = st_ref[...]


def _hg_call(proj, lb_row, nw_row, tables, s0, batch, seq, pos0):
    mstack, masks = tables
    hb = HEADS_PER_STEP
    nchunks = seq // CHUNK
    qk_w, v_w = HEADS * DK, HEADS * DV
    base = 2 * qk_w + 2 * v_w
    qb, fb = base // (hb * DK), (base + qk_w) // (hb * DK)
    ib, gb = (base + 2 * qk_w) // (hb * DV), (base + 2 * qk_w + v_w) // (hb * DV)
    kern = functools.partial(_hg_kernel, hb=hb, nchunks=nchunks, pos0=pos0)
    return pl.pallas_call(
        kern,
        grid=(batch, HEADS // hb),
        in_specs=[
            pl.BlockSpec((seq, hb * DK), lambda b, h: (b, qb + h)),
            pl.BlockSpec((seq, hb * DK), lambda b, h: (b, fb + h)),
            pl.BlockSpec((seq, hb * DV), lambda b, h: (b, ib + h)),
            pl.BlockSpec((seq, hb * DV), lambda b, h: (b, gb + h)),
            pl.BlockSpec((1, hb * DK), lambda b, h: (0, h)),
            pl.BlockSpec((1, DV), lambda b, h: (0, 0)),
            pl.BlockSpec(mstack.shape, lambda b, h: (0, 0)),
            pl.BlockSpec(masks.shape, lambda b, h: (0, 0, 0)),
            pl.BlockSpec((hb, DV, DK), lambda b, h: (h, 0, 0)),
        ],
        out_specs=[
            pl.BlockSpec((seq, hb * DV), lambda b, h: (b, h)),
            pl.BlockSpec((None, hb, DV, DK), lambda b, h: (b, h, 0, 0)),
        ],
        out_shape=[jax.ShapeDtypeStruct((batch * seq, v_w), BF16),
                   jax.ShapeDtypeStruct((batch, HEADS, DV, DK), F32)],
        scratch_shapes=[pltpu.VMEM((hb, DV, DK), F32)],
        compiler_params=_params("parallel", "parallel"),
        name="hgrn2",
    )(proj, proj, proj, proj, lb_row, nw_row, mstack, masks, s0)


def _merge_kernel(yr_ref, yh_ref, wr_ref, wh_ref, gr_ref, gh_ref, o_ref):
    a = jnp.dot(yr_ref[...], wr_ref[...], preferred_element_type=F32)
    b = jnp.dot(yh_ref[...], wh_ref[...], preferred_element_type=F32)
    o_ref[...] = (_sigmoid(gr_ref[...].astype(F32)) * a + _sigmoid(gh_ref[...].astype(F32)) * b).astype(o_ref.dtype)


def _merge_call(yr, yh, proj, w_ret, w_hg, layer):
    rows, k = yr.shape
    d = w_ret.shape[-1]
    tm = min(rows, ROW_TILE)
    tn = 512
    gate_ret_col = proj.shape[1] - 2 * d
    gr_blk, gh_blk = gate_ret_col // tn, (gate_ret_col + d) // tn
    return pl.pallas_call(
        _merge_kernel,
        grid=(rows // tm, d // tn),
        in_specs=[
            pl.BlockSpec((tm, k), lambda i, j: (i, 0)),
            pl.BlockSpec((tm, k), lambda i, j: (i, 0)),
            pl.BlockSpec((None, k, tn), lambda i, j: (layer, 0, j)),
            pl.BlockSpec((None, k, tn), lambda i, j: (layer, 0, j)),
            pl.BlockSpec((tm, tn), lambda i, j: (i, gr_blk + j)),
            pl.BlockSpec((tm, tn), lambda i, j: (i, gh_blk + j)),
        ],
        out_specs=pl.BlockSpec((tm, tn), lambda i, j: (i, j)),
        out_shape=jax.ShapeDtypeStruct((rows, d), BF16),
        compiler_params=_params("parallel", "arbitrary"),
        name="branch_merge",
    )(yr, yh, w_ret, w_hg, proj, proj)


def _out_kernel(y_ref, w_ref, h_ref, post_ref, nxt_ref, hn_ref, xn_ref):
    m = jnp.dot(y_ref[...], w_ref[...], preferred_element_type=F32)
    hn = h_ref[...] + _rms_scale(m) * post_ref[...]
    hn_ref[...] = hn
    xn_ref[...] = (_rms_scale(hn) * nxt_ref[...]).astype(xn_ref.dtype)


def _out_call(y, w_out, h, post_row, next_row, layer):
    rows, d = h.shape
    tm = min(rows, 512)
    row_blk = pl.BlockSpec((tm, d), lambda i: (i, 0))
    vec_blk = pl.BlockSpec((1, d), lambda i: (0, 0))
    return pl.pallas_call(
        _out_kernel,
        grid=(rows // tm,),
        in_specs=[row_blk, pl.BlockSpec((None, d, d), lambda i: (layer, 0, 0)), row_blk, vec_blk, vec_blk],
        out_specs=[row_blk, row_blk],
        out_shape=[jax.ShapeDtypeStruct((rows, d), F32), jax.ShapeDtypeStruct((rows, d), BF16)],
        compiler_params=_params("parallel"),
        name="out_proj",
    )(y, w_out, h, post_row, next_row)


def _ffn_kernel(x_ref, wg_ref, wu_ref, wd_ref, h_ref, post_ref, nxt_ref, hn_ref, *rest, with_next):
    if with_next:
        xn_ref, acc_ref = rest
    else:
        (acc_ref,) = rest
    f = pl.program_id(1)

    @pl.when(f == 0)
    def _():
        acc_ref[...] = jnp.zeros_like(acc_ref)

    x = x_ref[...]
    g = jnp.dot(x, wg_ref[...], preferred_element_type=F32)
    u = jnp.dot(x, wu_ref[...], preferred_element_type=F32)
    act = (g * _sigmoid(g) * u).astype(BF16)
    acc_ref[...] += jnp.dot(act, wd_ref[...], preferred_element_type=F32)

    @pl.when(f == pl.num_programs(1) - 1)
    def _():
        hn = h_ref[...] + _rms_scale(acc_ref[...]) * post_ref[...]
        hn_ref[...] = hn
        if with_next:
            xn_ref[...] = (_rms_scale(hn) * nxt_ref[...]).astype(xn_ref.dtype)


def _ffn_call(xn, w_gate, w_up, w_down, h, post_row, next_row, layer, with_next):
    rows, d = h.shape
    d_ff = w_gate.shape[-1]
    tm = min(rows, 512)
    tf = 512
    row_blk = pl.BlockSpec((tm, d), lambda i, f: (i, 0))
    vec_blk = pl.BlockSpec((1, d), lambda i, f: (0, 0))
    out_specs = [row_blk]
    out_shape = [jax.ShapeDtypeStruct((rows, d), F32)]
    if with_next:
        out_specs.append(row_blk)
        out_shape.append(jax.ShapeDtypeStruct((rows, d), BF16))
    res = pl.pallas_call(
        functools.partial(_ffn_kernel, with_next=with_next),
        grid=(rows // tm, d_ff // tf),
        in_specs=[
            row_blk,
            pl.BlockSpec((None, d, tf), lambda i, f: (layer, 0, f)),
            pl.BlockSpec((None, d, tf), lambda i, f: (layer, 0, f)),
            pl.BlockSpec((None, tf, d), lambda i, f: (layer, f, 0)),
            row_blk, vec_blk, vec_blk,
        ],
        out_specs=out_specs,
        out_shape=out_shape,
        scratch_shapes=[pltpu.VMEM((tm, d), F32)],
        compiler_params=_params("parallel", "arbitrary"),
        name="swiglu_ffn",
    )(xn, w_gate, w_up, w_down, h, post_row, next_row)
    return (res[0], res[1]) if with_next else (res[0], None)


def kernel(x, meta_tokens, norm_mix_pre, norm_mix_post, norm_ffn_pre, norm_ffn_post, w_in, hg_lb_logits,
           hg_norm_w, w_br_ret, w_br_hg, w_out, w_ffn_gate, w_ffn_up, w_ffn_down):
    batch, seq, d = x.shape
    depth = w_in.shape[0]
    assert seq % CHUNK == 0 and meta_tokens.shape == (N_META, d)

    lb_sm = jax.nn.softmax(hg_lb_logits.astype(F32), axis=0)
    lbs = jnp.cumsum(lb_sm, axis=0) - lb_sm[0:1]

    wb = [w.astype(BF16) for w in (w_in, w_br_ret, w_br_hg, w_out, w_ffn_gate, w_ffn_up, w_ffn_down)]
    w_in_b, w_ret_b, w_hg_b, w_out_b, w_gate_b, w_up_b, w_down_b = wb

    hg_tables = _hg_tables()
    meta_h = jnp.concatenate([jnp.zeros((PAD, d), F32), meta_tokens.astype(F32)], axis=0)
    row_sets = [
        dict(h=meta_h, batch=1, seq=CHUNK, pos0=-PAD),
        dict(h=x.reshape(batch * seq, d).astype(F32), batch=batch, seq=seq, pos0=N_META),
    ]
    for rs in row_sets:
        rs["ret_tables"] = _ret_tables(jnp.arange(rs["seq"], dtype=jnp.int32) + rs["pos0"])
        rs["xn"] = _norm_call(rs["h"], norm_mix_pre[0][None])

    for l in range(depth):
        last = l == depth - 1
        ret_state = jnp.zeros((HEADS, DK, DV), F32)
        hg_state = jnp.zeros((HEADS, DV, DK), F32)
        for rs in row_sets:
            is_meta = rs["pos0"] < 0
            proj = _proj_call(rs["xn"], w_in_b, l)
            yr, ret_fin = _ret_call(proj, rs["ret_tables"], ret_state, rs["batch"], rs["seq"], rs["pos0"])
            yh, hg_fin = _hg_call(proj, lbs[l][None], hg_norm_w[l][None], hg_tables, hg_state,
                                  rs["batch"], rs["seq"], rs["pos0"])
            if is_meta:
                ret_state, hg_state = ret_fin[0], hg_fin[0]
                if last:
                    continue
            y = _merge_call(yr, yh, proj, w_ret_b, w_hg_b, l)
            h_mid, xn_ffn = _out_call(y, w_out_b, rs["h"], norm_mix_post[l][None], norm_ffn_pre[l][None], l)
            next_row = norm_mix_pre[min(l + 1, depth - 1)][None]
            rs["h"], rs["xn"] = _ffn_call(xn_ffn, w_gate_b, w_up_b, w_down_b, h_mid, norm_ffn_post[l][None],
                                          next_row, l, with_next=not last)
    return row_sets[1]["h"].reshape(batch, seq, d)
```

```python
import functools

import numpy as np
import jax
import jax.numpy as jnp
from jax import lax
from jax.experimental import pallas as pl
from jax.experimental.pallas import tpu as pltpu

N_META = 16
HEADS = 8
DK = 128
DV = 256
CHUNK = 128
PAD = CHUNK - N_META
RMS_EPS = 1e-6
ROPE_BASE = 10000.0
LEVELS = 7
SUBLANES = 8
QK_W = HEADS * DK
V_W = HEADS * DV

VMEM_LIMIT_BYTES = 56 * 1024 * 1024
ROW_TILE = 1024
IN_TILE = 1024
HEADS_PER_STEP = 4
MIXER_ROWS = 1024

_IN_WIDTHS = (QK_W, QK_W, V_W, V_W, QK_W, QK_W, V_W, V_W, V_W, V_W)
_IN_STARTS = tuple(sum(_IN_WIDTHS[:i]) // IN_TILE for i in range(len(_IN_WIDTHS)))


def _tiles(*parts):
    return tuple(t for p in parts for t in range(_IN_STARTS[p], _IN_STARTS[p] + _IN_WIDTHS[p] // IN_TILE))


ROT_TILES = _tiles(0, 1)
IDENT_TILES = _tiles(2, 6)
SILU_TILES = _tiles(3, 4, 7)
FORGET_TILES = _tiles(5)
SIGMOID_TILES = _tiles(8, 9)

F32 = jnp.float32
BF16 = jnp.bfloat16
NT_DIMS = (((1,), (1,)), ((), ()))
TN_DIMS = (((0,), (0,)), ((), ()))


def _params(*semantics):
    return pltpu.CompilerParams(dimension_semantics=semantics, vmem_limit_bytes=VMEM_LIMIT_BYTES)


def _sigmoid(x):
    return 1.0 / (1.0 + jnp.exp(-x))


def _rms_scale(x):
    return x * lax.rsqrt(jnp.mean(x * x, axis=-1, keepdims=True) + RMS_EPS)


def _tile_lookup(tiles):
    def lookup(j):
        out = tiles[-1]
        for idx in range(len(tiles) - 2, -1, -1):
            out = jnp.where(j == idx, tiles[idx], out)
        return out
    return lookup


def _norm_kernel(h_ref, w_ref, o_ref):
    o_ref[...] = (_rms_scale(h_ref[...]) * w_ref[...]).astype(o_ref.dtype)


def _norm_call(h, w_row):
    rows, d = h.shape
    tm = min(rows, 512)
    return pl.pallas_call(
        _norm_kernel,
        grid=(rows // tm,),
        in_specs=[pl.BlockSpec((tm, d), lambda i: (i, 0)), pl.BlockSpec((1, d), lambda i: (0, 0))],
        out_specs=pl.BlockSpec((tm, d), lambda i: (i, 0)),
        out_shape=jax.ShapeDtypeStruct((rows, d), BF16),
        compiler_params=_params("parallel"),
        name="rms_norm",
    )(h, w_row)


def _proj_kernel(x_ref, w_ref, *rest, mode, pos0):
    wb_ref = rest[-1]
    i = pl.program_id(1)

    @pl.when(i == 0)
    def _():
        wb_ref[...] = w_ref[...].astype(BF16)

    acc = jnp.dot(x_ref[...], wb_ref[...], preferred_element_type=F32)
    if mode == "ident":
        rest[0][...] = acc.astype(BF16)
    elif mode == "silu":
        rest[0][...] = (acc * _sigmoid(acc)).astype(BF16)
    elif mode == "sigmoid":
        rest[0][...] = _sigmoid(acc).astype(BF16)
    elif mode == "rot":
        cos_ref, sin_ref, o_ref = rest[:3]
        cos, sin = cos_ref[...], sin_ref[...]
        for g in range(IN_TILE // DK):
            xg = acc[:, g * DK:(g + 1) * DK]
            o_ref[:, g * DK:(g + 1) * DK] = (xg * cos + pltpu.roll(xg, DK // 2, 1) * sin).astype(BF16)
    elif mode == "forget":
        lb_ref, kin_ref, hi_ref, lo_ref = rest[:4]
        one_minus_f = (1.0 - lb_ref[...]) / (1.0 + jnp.exp(acc))
        log_f = jnp.log1p(-one_minus_f)
        if pos0 < 0:
            tm = acc.shape[0]
            valid = (lax.broadcasted_iota(jnp.int32, (tm, 1), 0) + (i * tm + pos0)) >= 0
            log_f = jnp.where(valid, log_f, 0.0)
            one_minus_f = jnp.where(valid, one_minus_f, 0.0)
        kin_ref[...] = one_minus_f.astype(BF16)
        hi = log_f.astype(BF16)
        hi_ref[...] = hi
        lo_ref[...] = (log_f - hi.astype(F32)).astype(BF16)


def _proj_call(xn, w_in, layer, tiles, mode, seq, pos0, extra=()):
    rows, k = xn.shape
    tm = min(rows, ROW_TILE)
    lookup = _tile_lookup(tiles)
    n_out = 3 if mode == "forget" else 1
    out_cols = len(tiles) * IN_TILE
    in_specs = [pl.BlockSpec((tm, k), lambda j, i: (i, 0)),
                pl.BlockSpec((None, k, IN_TILE), lambda j, i: (layer, 0, lookup(j)))]
    if mode == "rot":
        blocks_per_seq = seq // tm
        tab = pl.BlockSpec((None, tm, DK), lambda j, i: (j, i % blocks_per_seq, 0))
        in_specs += [tab, tab]
    elif mode == "forget":
        in_specs += [pl.BlockSpec((1, IN_TILE), lambda j, i: (0, 0))]
    out_spec = pl.BlockSpec((tm, IN_TILE), lambda j, i: (i, j))
    res = pl.pallas_call(
        functools.partial(_proj_kernel, mode=mode, pos0=pos0),
        grid=(len(tiles), rows // tm),
        in_specs=in_specs,
        out_specs=[out_spec] * n_out,
        out_shape=[jax.ShapeDtypeStruct((rows, out_cols), BF16)] * n_out,
        scratch_shapes=[pltpu.VMEM((k, IN_TILE), BF16)],
        compiler_params=_params("parallel", "arbitrary"),
        name="in_proj_" + mode,
    )(xn, w_in, *extra)
    return res if n_out > 1 else res[0]


def _rot_tables(seq, pos0):
    half = DK // 2
    inv = ROPE_BASE ** (-jnp.arange(half, dtype=F32) / half)
    pos = jnp.arange(seq, dtype=jnp.int32) + pos0
    ang = pos.astype(F32)[:, None] * inv[None, :]
    cos, sin = jnp.cos(ang), jnp.sin(ang)
    cos2 = jnp.concatenate([cos, cos], axis=1)
    sin2 = jnp.concatenate([-sin, sin], axis=1)
    scale = DK ** -0.5
    return jnp.stack([cos2, cos2 * scale]), jnp.stack([sin2, sin2 * scale])


def _ret_kernel(q_ref, k_ref, v_ref, g_ref, dmat_ref, qdec_ref, kdec_ref, cdec_ref, s0_ref,
                o_ref, sfin_ref, s_ref, *, hb, nchunks, pos0):
    @pl.when(pl.program_id(2) == 0)
    def _():
        s_ref[...] = s0_ref[...]

    def body(c, carry):
        r0 = pl.multiple_of(c * CHUNK, CHUNK)
        rows = pl.ds(r0, CHUNK)
        if pos0 < 0:
            valid = (lax.broadcasted_iota(jnp.int32, (CHUNK, 1), 0) + (r0 + pos0)) >= 0
        for j in range(hb):
            q = q_ref[rows, j * DK:(j + 1) * DK]
            k = k_ref[rows, j * DK:(j + 1) * DK]
            if pos0 < 0:
                k = jnp.where(valid, k, jnp.zeros_like(k))
            v = v_ref[rows, j * DV:(j + 1) * DV]
            scores = lax.dot_general(q, k, NT_DIMS, preferred_element_type=F32) * dmat_ref[j]
            s = s_ref[j]
            lhs = jnp.concatenate([scores.astype(BF16), q * qdec_ref[j]], axis=1)
            rhs = jnp.concatenate([v, s.astype(BF16)], axis=0)
            y = jnp.dot(lhs, rhs, preferred_element_type=F32)
            s_ref[j] = s * cdec_ref[j] + lax.dot_general(k * kdec_ref[j], v, TN_DIMS, preferred_element_type=F32)
            g = g_ref[rows, j * DV:(j + 1) * DV].astype(F32)
            o_ref[rows, j * DV:(j + 1) * DV] = (_rms_scale(y) * g).astype(o_ref.dtype)
        return carry

    lax.fori_loop(0, nchunks, body, 0)
    sfin_ref[...] = s_ref[...]


def _ret_tables():
    log_g = jnp.log1p(-jnp.exp2(-5.0 - jnp.arange(HEADS, dtype=F32)))
    idx = jnp.arange(CHUNK, dtype=F32)
    diff = idx[:, None] - idx[None, :]
    dmat = jnp.where(diff[None] >= 0, jnp.exp(jnp.maximum(diff, 0.0)[None] * log_g[:, None, None]), 0.0)
    qdec = jnp.exp((idx + 1)[None, :] * log_g[:, None])
    kdec = jnp.exp((CHUNK - 1 - idx)[None, :] * log_g[:, None])
    cdec = jnp.exp(CHUNK * log_g)
    qdec = jnp.broadcast_to(qdec[:, :, None], (HEADS, CHUNK, DK)).astype(BF16)
    kdec = jnp.broadcast_to(kdec[:, :, None], (HEADS, CHUNK, DK)).astype(BF16)
    cdec = jnp.broadcast_to(cdec[:, None, None], (HEADS, 1, DV))
    return dmat, qdec, kdec, cdec


def _mixer_grid(batch, seq):
    rb = min(seq, MIXER_ROWS)
    return rb, seq // rb


def _ret_call(qk, vi, act, tables, s0, batch, seq, pos0):
    dmat, qdec, kdec, cdec = tables
    hb = HEADS_PER_STEP
    rb, nblk = _mixer_grid(batch, seq)
    k_blk = QK_W // (hb * DK)
    kern = functools.partial(_ret_kernel, hb=hb, nchunks=rb // CHUNK, pos0=pos0)
    per_head = lambda b, h, r: (h, 0, 0)
    return pl.pallas_call(
        kern,
        grid=(batch, HEADS // hb, nblk),
        in_specs=[
            pl.BlockSpec((rb, hb * DK), lambda b, h, r: (b * nblk + r, h)),
            pl.BlockSpec((rb, hb * DK), lambda b, h, r: (b * nblk + r, k_blk + h)),
            pl.BlockSpec((rb, hb * DV), lambda b, h, r: (b * nblk + r, h)),
            pl.BlockSpec((rb, hb * DV), lambda b, h, r: (b * nblk + r, h)),
            pl.BlockSpec((hb, CHUNK, CHUNK), per_head),
            pl.BlockSpec((hb, CHUNK, DK), per_head),
            pl.BlockSpec((hb, CHUNK, DK), per_head),
            pl.BlockSpec((hb, 1, DV), per_head),
            pl.BlockSpec((hb, DK, DV), per_head),
        ],
        out_specs=[
            pl.BlockSpec((rb, hb * DV), lambda b, h, r: (b * nblk + r, h)),
            pl.BlockSpec((None, hb, DK, DV), lambda b, h, r: (b, h, 0, 0)),
        ],
        out_shape=[jax.ShapeDtypeStruct((batch * seq, V_W), BF16),
                   jax.ShapeDtypeStruct((batch, HEADS, DK, DV), F32)],
        scratch_shapes=[pltpu.VMEM((hb, DK, DV), F32)],
        compiler_params=_params("parallel", "parallel", "arbitrary"),
        name="retention",
    )(qk, qk, vi, act, dmat, qdec, kdec, cdec, s0)


def _hg_tables():
    t = np.arange(CHUNK)[:, None]
    u = np.arange(CHUNK)[None, :]
    mats, masks = [], []
    for j in range(LEVELS):
        m = 1 << j
        upper = ((t >> j) & 1) == 1
        q_part = upper & (u >= (t & ~(m - 1))) & (u <= t)
        k_part = (~upper) & (u > t) & (u <= (t | (m - 1)))
        mats.append(q_part | k_part)
        masks.append(((t >> (j + 1)) == (u >> (j + 1))) & upper & (((u >> j) & 1) == 0))
    mats.append(u <= t)
    masks.append(t == u)
    mstack = np.concatenate(mats, axis=0).astype(np.float32)
    mstack = np.concatenate([mstack, mstack], axis=1)
    return jnp.asarray(mstack, dtype=BF16), jnp.asarray(np.stack(masks).astype(np.float32))


def _hg_kernel(q_ref, kin_ref, hi_ref, lo_ref, v_ref, g_ref, nw_ref, mstack_ref, masks_ref, s0_ref,
               o_ref, sfin_ref, st_ref, a_ref, *, hb, nchunks):
    @pl.when(pl.program_id(2) == 0)
    def _():
        st_ref[...] = s0_ref[...]

    row = lax.broadcasted_iota(jnp.int32, (CHUNK, 1), 0)

    def body(c, carry):
        rows = pl.ds(pl.multiple_of(c * CHUNK, CHUNK), CHUNK)
        for pair in range(hb // 2):
            lanes = slice(pair * 2 * DK, (pair + 1) * 2 * DK)
            pieces = jnp.concatenate([hi_ref[rows, lanes], lo_ref[rows, lanes]], axis=0)
            expo2 = jnp.dot(mstack_ref[...], pieces, preferred_element_type=F32)
            for jj in range(2):
                j = 2 * pair + jj
                expo = expo2[:, jj * DK:(jj + 1) * DK]
                qb = q_ref[rows, j * DK:(j + 1) * DK]
                kb = kin_ref[rows, j * DK:(j + 1) * DK]
                q, kin = qb.astype(F32), kb.astype(F32)
                a = lax.dot_general(qb, kb, NT_DIMS, preferred_element_type=F32) * masks_ref[LEVELS]
                for lev in range(LEVELS):
                    e = jnp.exp(expo[lev * CHUNK:(lev + 1) * CHUNK])
                    m = 1 << lev
                    if m < SUBLANES:
                        x = (jnp.where(((row >> lev) & 1) == 1, q, kin) * e).astype(BF16)
                        a = a + lax.dot_general(x, x, NT_DIMS, preferred_element_type=F32) * masks_ref[lev]
                    else:
                        if lev == 3:
                            a_ref[j] = a
                        ke = (kin * e).astype(BF16)
                        upper = [slice(s, s + m) for s in range(m, CHUNK, 2 * m)]
                        qe = jnp.concatenate([q[sl] * e[sl] for sl in upper], axis=0).astype(BF16)
                        p = lax.dot_general(qe, ke, NT_DIMS, preferred_element_type=F32)
                        for idx, sl in enumerate(upper):
                            a_ref[j, sl, :] += p[idx * m:(idx + 1) * m] * masks_ref[lev, sl, :]
                b = expo[LEVELS * CHUNK:]
                total = b[CHUNK - 1:CHUNK, :]
                v = v_ref[rows, j * DV:(j + 1) * DV]
                st = st_ref[j]
                y = jnp.dot(a_ref[j].astype(BF16), v, preferred_element_type=F32)
                y = y + lax.dot_general((q * jnp.exp(b)).astype(BF16), st.astype(BF16), NT_DIMS,
                                        preferred_element_type=F32)
                k_end = (kin * jnp.exp(total - b)).astype(BF16)
                st_ref[j] = st * jnp.exp(total) + lax.dot_general(v, k_end, TN_DIMS, preferred_element_type=F32)
                g = g_ref[rows, j * DV:(j + 1) * DV].astype(F32)
                o_ref[rows, j * DV:(j + 1) * DV] = (_rms_scale(y) * nw_ref[...] * g).astype(o_ref.dtype)
        return carry

    lax.fori_loop(0, nchunks, body, 0)
    sfin_ref[...] = st_ref[...]


def _hg_call(act, kin, lf_hi, lf_lo, vi, nw_row, tables, s0, batch, seq):
    mstack, masks = tables
    hb = HEADS_PER_STEP
    rb, nblk = _mixer_grid(batch, seq)
    q_blk = V_W // (hb * DK)
    g_blk = (V_W + QK_W) // (hb * DV)
    i_blk = V_W // (hb * DV)
    kern = functools.partial(_hg_kernel, hb=hb, nchunks=rb // CHUNK)
    row_blk = lambda off: (lambda b, h, r: (b * nblk + r, off + h))
    return pl.pallas_call(
        kern,
        grid=(batch, HEADS // hb, nblk),
        in_specs=[
            pl.BlockSpec((rb, hb * DK), row_blk(q_blk)),
            pl.BlockSpec((rb, hb * DK), row_blk(0)),
            pl.BlockSpec((rb, hb * DK), row_blk(0)),
            pl.BlockSpec((rb, hb * DK), row_blk(0)),
            pl.BlockSpec((rb, hb * DV), row_blk(i_blk)),
            pl.BlockSpec((rb, hb * DV), row_blk(g_blk)),
            pl.BlockSpec((1, DV), lambda b, h, r: (0, 0)),
            pl.BlockSpec(mstack.shape, lambda b, h, r: (0, 0)),
            pl.BlockSpec(masks.shape, lambda b, h, r: (0, 0, 0)),
            pl.BlockSpec((hb, DV, DK), lambda b, h, r: (h, 0, 0)),
        ],
        out_specs=[
            pl.BlockSpec((rb, hb * DV), row_blk(0)),
            pl.BlockSpec((None, hb, DV, DK), lambda b, h, r: (b, h, 0, 0)),
        ],
        out_shape=[jax.ShapeDtypeStruct((batch * seq, V_W), BF16),
                   jax.ShapeDtypeStruct((batch, HEADS, DV, DK), F32)],
        scratch_shapes=[pltpu.VMEM((hb, DV, DK), F32), pltpu.VMEM((hb, CHUNK, CHUNK), F32)],
        compiler_params=_params("parallel", "parallel", "arbitrary"),
        name="hgrn2",
    )(act, kin, lf_hi, lf_lo, vi, act, nw_row, mstack, masks, s0)


def _merge_kernel(yr_ref, yh_ref, wr_ref, wh_ref, gr_ref, gh_ref, o_ref):
    a = jnp.dot(yr_ref[...], wr_ref[...], preferred_element_type=F32)
    b = jnp.dot(yh_ref[...], wh_ref[...], preferred_element_type=F32)
    o_ref[...] = (gr_ref[...].astype(F32) * a + gh_ref[...].astype(F32) * b).astype(o_ref.dtype)


def _merge_call(yr, yh, gates, w_ret, w_hg, layer):
    rows, k = yr.shape
    d = w_ret.shape[-1]
    tm = min(rows, ROW_TILE)
    tn = 512
    gh_blk = d // tn
    return pl.pallas_call(
        _merge_kernel,
        grid=(rows // tm, d // tn),
        in_specs=[
            pl.BlockSpec((tm, k), lambda i, j: (i, 0)),
            pl.BlockSpec((tm, k), lambda i, j: (i, 0)),
            pl.BlockSpec((None, k, tn), lambda i, j: (layer, 0, j)),
            pl.BlockSpec((None, k, tn), lambda i, j: (layer, 0, j)),
            pl.BlockSpec((tm, tn), lambda i, j: (i, j)),
            pl.BlockSpec((tm, tn), lambda i, j: (i, gh_blk + j)),
        ],
        out_specs=pl.BlockSpec((tm, tn), lambda i, j: (i, j)),
        out_shape=jax.ShapeDtypeStruct((rows, d), BF16),
        compiler_params=_params("parallel", "arbitrary"),
        name="branch_merge",
    )(yr, yh, w_ret, w_hg, gates, gates)


def _out_kernel(y_ref, w_ref, h_ref, post_ref, nxt_ref, hn_ref, xn_ref):
    m = jnp.dot(y_ref[...], w_ref[...], preferred_element_type=F32)
    hn = h_ref[...] + _rms_scale(m) * post_ref[...]
    hn_ref[...] = hn
    xn_ref[...] = (_rms_scale(hn) * nxt_ref[...]).astype(xn_ref.dtype)


def _out_call(y, w_out, h, post_row, next_row, layer):
    rows, d = h.shape
    tm = min(rows, 512)
    row_blk = pl.BlockSpec((tm, d), lambda i: (i, 0))
    vec_blk = pl.BlockSpec((1, d), lambda i: (0, 0))
    return pl.pallas_call(
        _out_kernel,
        grid=(rows // tm,),
        in_specs=[row_blk, pl.BlockSpec((None, d, d), lambda i: (layer, 0, 0)), row_blk, vec_blk, vec_blk],
        out_specs=[row_blk, row_blk],
        out_shape=[jax.ShapeDtypeStruct((rows, d), F32), jax.ShapeDtypeStruct((rows, d), BF16)],
        compiler_params=_params("parallel"),
        name="out_proj",
    )(y, w_out, h, post_row, next_row)


def _ffn_kernel(x_ref, wg_ref, wu_ref, wd_ref, h_ref, post_ref, nxt_ref, hn_ref, *rest, with_next):
    if with_next:
        xn_ref, acc_ref = rest
    else:
        (acc_ref,) = rest
    f = pl.program_id(1)

    @pl.when(f == 0)
    def _():
        acc_ref[...] = jnp.zeros_like(acc_ref)

    x = x_ref[...]
    g = jnp.dot(x, wg_ref[...], preferred_element_type=F32)
    u = jnp.dot(x, wu_ref[...], preferred_element_type=F32)
    act = (g * _sigmoid(g) * u).astype(BF16)
    acc_ref[...] += jnp.dot(act, wd_ref[...], preferred_element_type=F32)

    @pl.when(f == pl.num_programs(1) - 1)
    def _():
        hn = h_ref[...] + _rms_scale(acc_ref[...]) * post_ref[...]
        hn_ref[...] = hn
        if with_next:
            xn_ref[...] = (_rms_scale(hn) * nxt_ref[...]).astype(xn_ref.dtype)


def _ffn_call(xn, w_gate, w_up, w_down, h, post_row, next_row, layer, with_next):
    rows, d = h.shape
    d_ff = w_gate.shape[-1]
    tm = min(rows, 512)
    tf = 512
    row_blk = pl.BlockSpec((tm, d), lambda i, f: (i, 0))
    vec_blk = pl.BlockSpec((1, d), lambda i, f: (0, 0))
    out_specs = [row_blk]
    out_shape = [jax.ShapeDtypeStruct((rows, d), F32)]
    if with_next:
        out_specs.append(row_blk)
        out_shape.append(jax.ShapeDtypeStruct((rows, d), BF16))
    res = pl.pallas_call(
        functools.partial(_ffn_kernel, with_next=with_next),
        grid=(rows // tm, d_ff // tf),
        in_specs=[
            row_blk,
            pl.BlockSpec((None, d, tf), lambda i, f: (layer, 0, f)),
            pl.BlockSpec((None, d, tf), lambda i, f: (layer, 0, f)),
            pl.BlockSpec((None, tf, d), lambda i, f: (layer, f, 0)),
            row_blk, vec_blk, vec_blk,
        ],
        out_specs=out_specs,
        out_shape=out_shape,
        scratch_shapes=[pltpu.VMEM((tm, d), F32)],
        compiler_params=_params("parallel", "arbitrary"),
        name="swiglu_ffn",
    )(xn, w_gate, w_up, w_down, h, post_row, next_row)
    return (res[0], res[1]) if with_next else (res[0], None)


def kernel(x, meta_tokens, norm_mix_pre, norm_mix_post, norm_ffn_pre, norm_ffn_post, w_in, hg_lb_logits,
           hg_norm_w, w_br_ret, w_br_hg, w_out, w_ffn_gate, w_ffn_up, w_ffn_down):
    batch, seq, d = x.shape
    depth = w_in.shape[0]
    assert seq % CHUNK == 0 and meta_tokens.shape == (N_META, d)
    assert w_in.shape[-1] == sum(_IN_WIDTHS) and d == V_W

    lb_sm = jax.nn.softmax(hg_lb_logits.astype(F32), axis=0)
    lbs = jnp.cumsum(lb_sm, axis=0) - lb_sm[0:1]

    wb = [w.astype(BF16) for w in (w_br_ret, w_br_hg, w_out, w_ffn_gate, w_ffn_up, w_ffn_down)]
    w_ret_b, w_hg_b, w_out_b, w_gate_b, w_up_b, w_down_b = wb

    hg_tables = _hg_tables()
    ret_tables = _ret_tables()
    meta_h = jnp.concatenate([jnp.zeros((PAD, d), F32), meta_tokens.astype(F32)], axis=0)
    row_sets = [
        dict(h=meta_h, batch=1, seq=CHUNK, pos0=-PAD),
        dict(h=x.reshape(batch * seq, d).astype(F32), batch=batch, seq=seq, pos0=N_META),
    ]
    for rs in row_sets:
        rs["rot"] = _rot_tables(rs["seq"], rs["pos0"])
        rs["xn"] = _norm_call(rs["h"], norm_mix_pre[0][None])

    for l in range(depth):
        last = l == depth - 1
        ret_state = jnp.zeros((HEADS, DK, DV), F32)
        hg_state = jnp.zeros((HEADS, DV, DK), F32)
        for rs in row_sets:
            is_meta = rs["pos0"] < 0
            proj = functools.partial(_proj_call, rs["xn"], w_in, l, seq=rs["seq"], pos0=rs["pos0"])
            qk = proj(ROT_TILES, "rot", extra=rs["rot"])
            vi = proj(IDENT_TILES, "ident")
            act = proj(SILU_TILES, "silu")
            kin, lf_hi, lf_lo = proj(FORGET_TILES, "forget", extra=(lbs[l][None],))
            yr, ret_fin = _ret_call(qk, vi, act, ret_tables, ret_state, rs["batch"], rs["seq"], rs["pos0"])
            yh, hg_fin = _hg_call(act, kin, lf_hi, lf_lo, vi, hg_norm_w[l][None], hg_tables, hg_state,
                                  rs["batch"], rs["seq"])
            if is_meta:
                ret_state, hg_state = ret_fin[0], hg_fin[0]
                if last:
                    continue
            gates = proj(SIGMOID_TILES, "sigmoid")
            y = _merge_call(yr, yh, gates, w_ret_b, w_hg_b, l)
            h_mid, xn_ffn = _out_call(y, w_out_b, rs["h"], norm_mix_post[l][None], norm_ffn_pre[l][None], l)
            next_row = norm_mix_pre[min(l + 1, depth - 1)][None]
            rs["h"], rs["xn"] = _ffn_call(xn_ffn, w_gate_b, w_up_b, w_down_b, h_mid, norm_ffn_post[l][None],
                                          next_row, l, with_next=not last)
    return row_sets[1]["h"].reshape(batch, seq, d)
```

```python
import functools

import numpy as np
import jax
import jax.numpy as jnp
from jax import lax
from jax.experimental import pallas as pl
from jax.experimental.pallas import tpu as pltpu

N_META = 16
HEADS = 8
DK = 128
DV = 256
CHUNK = 128
PAD = CHUNK - N_META
RMS_EPS = 1e-6
ROPE_BASE = 10000.0
LEVELS = 7
SUBLANES = 8
QK_W = HEADS * DK
V_W = HEADS * DV

VMEM_LIMIT_BYTES = 56 * 1024 * 1024
ROW_TILE = 1024
IN_TILE = 1024
PROJ_SLAB = 256
HEADS_PER_STEP = 4
MIXER_ROWS = 1024
MIXER_UNROLL = 2

_IN_WIDTHS = (QK_W, QK_W, V_W, V_W, QK_W, QK_W, V_W, V_W, V_W, V_W)
_IN_STARTS = tuple(sum(_IN_WIDTHS[:i]) // IN_TILE for i in range(len(_IN_WIDTHS)))


def _tiles(*parts):
    return tuple(t for p in parts for t in range(_IN_STARTS[p], _IN_STARTS[p] + _IN_WIDTHS[p] // IN_TILE))


ROT_TILES = _tiles(0, 1)
IDENT_TILES = _tiles(2, 6)
SILU_TILES = _tiles(3, 4, 7)
FORGET_TILES = _tiles(5)
SIGMOID_TILES = _tiles(8, 9)

F32 = jnp.float32
BF16 = jnp.bfloat16
NT_DIMS = (((1,), (1,)), ((), ()))
TN_DIMS = (((0,), (0,)), ((), ()))


def _params(*semantics, flags=None):
    return pltpu.CompilerParams(dimension_semantics=semantics, vmem_limit_bytes=VMEM_LIMIT_BYTES, flags=flags)


MIXER_FLAGS = None


def _sigmoid(x):
    return 1.0 / (1.0 + jnp.exp(-x))


def _rms_scale(x):
    return x * lax.rsqrt(jnp.mean(x * x, axis=-1, keepdims=True) + RMS_EPS)


def _tile_lookup(tiles):
    def lookup(j):
        out = tiles[-1]
        for idx in range(len(tiles) - 2, -1, -1):
            out = jnp.where(j == idx, tiles[idx], out)
        return out
    return lookup


def _norm_kernel(h_ref, w_ref, o_ref):
    o_ref[...] = (_rms_scale(h_ref[...]) * w_ref[...]).astype(o_ref.dtype)


def _norm_call(h, w_row):
    rows, d = h.shape
    tm = min(rows, 512)
    return pl.pallas_call(
        _norm_kernel,
        grid=(rows // tm,),
        in_specs=[pl.BlockSpec((tm, d), lambda i: (i, 0)), pl.BlockSpec((1, d), lambda i: (0, 0))],
        out_specs=pl.BlockSpec((tm, d), lambda i: (i, 0)),
        out_shape=jax.ShapeDtypeStruct((rows, d), BF16),
        compiler_params=_params("parallel"),
        name="rms_norm",
    )(h, w_row)


def _proj_kernel(x_ref, w_ref, *rest, mode, pos0):
    wb_ref = rest[-1]
    i = pl.program_id(1)

    @pl.when(i == 0)
    def _():
        wb_ref[...] = w_ref[...].astype(BF16)

    tm = x_ref.shape[0]
    slab = min(tm, PROJ_SLAB)
    for s0 in range(0, tm, slab):
        r = slice(s0, s0 + slab)
        acc = jnp.dot(x_ref[r, :], wb_ref[...], preferred_element_type=F32)
        if mode == "ident":
            rest[0][r, :] = acc.astype(BF16)
        elif mode == "silu":
            rest[0][r, :] = (acc * _sigmoid(acc)).astype(BF16)
        elif mode == "sigmoid":
            rest[0][r, :] = _sigmoid(acc).astype(BF16)
        elif mode == "rot":
            cos_ref, sin_ref, o_ref = rest[:3]
            cos, sin = cos_ref[r, :], sin_ref[r, :]
            for g in range(IN_TILE // DK):
                xg = acc[:, g * DK:(g + 1) * DK]
                o_ref[r, g * DK:(g + 1) * DK] = (xg * cos + pltpu.roll(xg, DK // 2, 1) * sin).astype(BF16)
        elif mode == "forget":
            lb_ref, kin_ref, hi_ref, lo_ref = rest[:4]
            one_minus_f = (1.0 - lb_ref[...]) / (1.0 + jnp.exp(acc))
            log_f = jnp.log1p(-one_minus_f)
            if pos0 < 0:
                valid = (lax.broadcasted_iota(jnp.int32, (slab, 1), 0) + (i * tm + s0 + pos0)) >= 0
                log_f = jnp.where(valid, log_f, 0.0)
                one_minus_f = jnp.where(valid, one_minus_f, 0.0)
            kin_ref[r, :] = one_minus_f.astype(BF16)
            hi = log_f.astype(BF16)
            hi_ref[r, :] = hi
            lo_ref[r, :] = (log_f - hi.astype(F32)).astype(BF16)


def _proj_call(xn, w_in, layer, tiles, mode, seq, pos0, extra=()):
    rows, k = xn.shape
    tm = min(rows, ROW_TILE)
    lookup = _tile_lookup(tiles)
    n_out = 3 if mode == "forget" else 1
    out_cols = len(tiles) * IN_TILE
    in_specs = [pl.BlockSpec((tm, k), lambda j, i: (i, 0)),
                pl.BlockSpec((None, k, IN_TILE), lambda j, i: (layer, 0, lookup(j)))]
    if mode == "rot":
        blocks_per_seq = seq // tm
        tab = pl.BlockSpec((None, tm, DK), lambda j, i: (j, i % blocks_per_seq, 0))
        in_specs += [tab, tab]
    elif mode == "forget":
        in_specs += [pl.BlockSpec((1, IN_TILE), lambda j, i: (0, 0))]
    out_spec = pl.BlockSpec((tm, IN_TILE), lambda j, i: (i, j))
    res = pl.pallas_call(
        functools.partial(_proj_kernel, mode=mode, pos0=pos0),
        grid=(len(tiles), rows // tm),
        in_specs=in_specs,
        out_specs=[out_spec] * n_out,
        out_shape=[jax.ShapeDtypeStruct((rows, out_cols), BF16)] * n_out,
        scratch_shapes=[pltpu.VMEM((k, IN_TILE), BF16)],
        compiler_params=_params("parallel", "arbitrary"),
        name="in_proj_" + mode,
    )(xn, w_in, *extra)
    return res if n_out > 1 else res[0]


def _rot_tables(seq, pos0):
    half = DK // 2
    inv = ROPE_BASE ** (-jnp.arange(half, dtype=F32) / half)
    pos = jnp.arange(seq, dtype=jnp.int32) + pos0
    ang = pos.astype(F32)[:, None] * inv[None, :]
    cos, sin = jnp.cos(ang), jnp.sin(ang)
    cos2 = jnp.concatenate([cos, cos], axis=1)
    sin2 = jnp.concatenate([-sin, sin], axis=1)
    scale = DK ** -0.5
    return jnp.stack([cos2, cos2 * scale]), jnp.stack([sin2, sin2 * scale])


def _ret_kernel(q_ref, k_ref, v_ref, g_ref, dmat_ref, qdec_ref, kdec_ref, cdec_ref, s0_ref,
                o_ref, sfin_ref, s_ref, *, hb, nchunks, pos0):
    @pl.when(pl.program_id(2) == 0)
    def _():
        s_ref[...] = s0_ref[...]

    def body(c, carry):
        r0 = pl.multiple_of(c * CHUNK, CHUNK)
        rows = pl.ds(r0, CHUNK)
        if pos0 < 0:
            valid = (lax.broadcasted_iota(jnp.int32, (CHUNK, 1), 0) + (r0 + pos0)) >= 0
        for j in range(hb):
            q = q_ref[rows, j * DK:(j + 1) * DK]
            k = k_ref[rows, j * DK:(j + 1) * DK]
            if pos0 < 0:
                k = jnp.where(valid, k, jnp.zeros_like(k))
            v = v_ref[rows, j * DV:(j + 1) * DV]
            scores = lax.dot_general(q, k, NT_DIMS, preferred_element_type=F32) * dmat_ref[j]
            s = s_ref[j]
            lhs = jnp.concatenate([scores.astype(BF16), q * qdec_ref[j]], axis=1)
            rhs = jnp.concatenate([v, s.astype(BF16)], axis=0)
            y = jnp.dot(lhs, rhs, preferred_element_type=F32)
            s_ref[j] = s * cdec_ref[j] + lax.dot_general(k * kdec_ref[j], v, TN_DIMS, preferred_element_type=F32)
            g = g_ref[rows, j * DV:(j + 1) * DV].astype(F32)
            o_ref[rows, j * DV:(j + 1) * DV] = (_rms_scale(y) * g).astype(o_ref.dtype)
        return carry

    lax.fori_loop(0, nchunks, body, 0, unroll=MIXER_UNROLL)
    sfin_ref[...] = s_ref[...]


def _ret_tables():
    log_g = jnp.log1p(-jnp.exp2(-5.0 - jnp.arange(HEADS, dtype=F32)))
    idx = jnp.arange(CHUNK, dtype=F32)
    diff = idx[:, None] - idx[None, :]
    dmat = jnp.where(diff[None] >= 0, jnp.exp(jnp.maximum(diff, 0.0)[None] * log_g[:, None, None]), 0.0)
    qdec = jnp.exp((idx + 1)[None, :] * log_g[:, None])
    kdec = jnp.exp((CHUNK - 1 - idx)[None, :] * log_g[:, None])
    cdec = jnp.exp(CHUNK * log_g)
    qdec = jnp.broadcast_to(qdec[:, :, None], (HEADS, CHUNK, DK)).astype(BF16)
    kdec = jnp.broadcast_to(kdec[:, :, None], (HEADS, CHUNK, DK)).astype(BF16)
    cdec = jnp.broadcast_to(cdec[:, None, None], (HEADS, 1, DV))
    return dmat, qdec, kdec, cdec


def _mixer_grid(batch, seq):
    rb = min(seq, MIXER_ROWS)
    return rb, seq // rb


def _ret_call(qk, vi, act, tables, s0, batch, seq, pos0):
    dmat, qdec, kdec, cdec = tables
    hb = HEADS_PER_STEP
    rb, nblk = _mixer_grid(batch, seq)
    k_blk = QK_W // (hb * DK)
    kern = functools.partial(_ret_kernel, hb=hb, nchunks=rb // CHUNK, pos0=pos0)
    per_head = lambda b, h, r: (h, 0, 0)
    return pl.pallas_call(
        kern,
        grid=(batch, HEADS // hb, nblk),
        in_specs=[
            pl.BlockSpec((rb, hb * DK), lambda b, h, r: (b * nblk + r, h)),
            pl.BlockSpec((rb, hb * DK), lambda b, h, r: (b * nblk + r, k_blk + h)),
            pl.BlockSpec((rb, hb * DV), lambda b, h, r: (b * nblk + r, h)),
            pl.BlockSpec((rb, hb * DV), lambda b, h, r: (b * nblk + r, h)),
            pl.BlockSpec((hb, CHUNK, CHUNK), per_head),
            pl.BlockSpec((hb, CHUNK, DK), per_head),
            pl.BlockSpec((hb, CHUNK, DK), per_head),
            pl.BlockSpec((hb, 1, DV), per_head),
            pl.BlockSpec((hb, DK, DV), per_head),
        ],
        out_specs=[
            pl.BlockSpec((rb, hb * DV), lambda b, h, r: (b * nblk + r, h)),
            pl.BlockSpec((None, hb, DK, DV), lambda b, h, r: (b, h, 0, 0)),
        ],
        out_shape=[jax.ShapeDtypeStruct((batch * seq, V_W), BF16),
                   jax.ShapeDtypeStruct((batch, HEADS, DK, DV), F32)],
        scratch_shapes=[pltpu.VMEM((hb, DK, DV), F32)],
        compiler_params=_params("parallel", "parallel", "arbitrary", flags=MIXER_FLAGS),
        name="retention",
    )(qk, qk, vi, act, dmat, qdec, kdec, cdec, s0)


def _hg_tables():
    t = np.arange(CHUNK)[:, None]
    u = np.arange(CHUNK)[None, :]
    mats, masks = [], []
    for j in range(LEVELS):
        m = 1 << j
        upper = ((t >> j) & 1) == 1
        q_part = upper & (u >= (t & ~(m - 1))) & (u <= t)
        k_part = (~upper) & (u > t) & (u <= (t | (m - 1)))
        mats.append(q_part | k_part)
        masks.append(((t >> (j + 1)) == (u >> (j + 1))) & upper & (((u >> j) & 1) == 0))
    mats.append(u <= t)
    masks.append(t == u)
    mstack = np.concatenate(mats, axis=0).astype(np.float32)
    mstack = np.concatenate([mstack, mstack], axis=1)
    return jnp.asarray(mstack, dtype=BF16), jnp.asarray(np.stack(masks).astype(np.float32))


def _hg_kernel(q_ref, kin_ref, hi_ref, lo_ref, v_ref, g_ref, nw_ref, mstack_ref, masks_ref, s0_ref,
               o_ref, sfin_ref, st_ref, *, hb, nchunks):
    @pl.when(pl.program_id(2) == 0)
    def _():
        st_ref[...] = s0_ref[...]

    row = lax.broadcasted_iota(jnp.int32, (CHUNK, 1), 0)

    def body(c, carry):
        rows = pl.ds(pl.multiple_of(c * CHUNK, CHUNK), CHUNK)
        pieces = jnp.concatenate([hi_ref[rows, :], lo_ref[rows, :]], axis=0)
        expo_all = jnp.dot(mstack_ref[...], pieces, preferred_element_type=F32)
        for j in range(hb):
            expo = expo_all[:, j * DK:(j + 1) * DK]
            qb = q_ref[rows, j * DK:(j + 1) * DK]
            kb = kin_ref[rows, j * DK:(j + 1) * DK]
            q, kin = qb.astype(F32), kb.astype(F32)
            a = lax.dot_general(qb, kb, NT_DIMS, preferred_element_type=F32) * masks_ref[LEVELS]
            a_rows = None
            for lev in range(LEVELS):
                e = jnp.exp(expo[lev * CHUNK:(lev + 1) * CHUNK])
                m = 1 << lev
                if m < SUBLANES:
                    x = (jnp.where(((row >> lev) & 1) == 1, q, kin) * e).astype(BF16)
                    a = a + lax.dot_general(x, x, NT_DIMS, preferred_element_type=F32) * masks_ref[lev]
                else:
                    if a_rows is None:
                        a_rows = [a[s:s + SUBLANES] for s in range(0, CHUNK, SUBLANES)]
                    ke = (kin * e).astype(BF16)
                    upper = [s for s0 in range(m, CHUNK, 2 * m) for s in range(s0, s0 + m, SUBLANES)]
                    qe = jnp.concatenate([q[s:s + SUBLANES] * e[s:s + SUBLANES] for s in upper], axis=0)
                    p = lax.dot_general(qe.astype(BF16), ke, NT_DIMS, preferred_element_type=F32)
                    for idx, s in enumerate(upper):
                        a_rows[s // SUBLANES] = (a_rows[s // SUBLANES] + p[idx * SUBLANES:(idx + 1) * SUBLANES]
                                                 * masks_ref[lev, s:s + SUBLANES, :])
            a_bf = jnp.concatenate(a_rows, axis=0).astype(BF16)
            b = expo[LEVELS * CHUNK:]
            total = b[CHUNK - 1:CHUNK, :]
            v = v_ref[rows, j * DV:(j + 1) * DV]
            st = st_ref[j]
            y = jnp.dot(a_bf, v, preferred_element_type=F32)
            y = y + lax.dot_general((q * jnp.exp(b)).astype(BF16), st.astype(BF16), NT_DIMS,
                                    preferred_element_type=F32)
            k_end = (kin * jnp.exp(total - b)).astype(BF16)
            st_ref[j] = st * jnp.exp(total) + lax.dot_general(v, k_end, TN_DIMS, preferred_element_type=F32)
            g = g_ref[rows, j * DV:(j + 1) * DV].astype(F32)
            o_ref[rows, j * DV:(j + 1) * DV] = (_rms_scale(y) * nw_ref[...] * g).astype(o_ref.dtype)
        return carry

    lax.fori_loop(0, nchunks, body, 0, unroll=MIXER_UNROLL)
    sfin_ref[...] = st_ref[...]


def _hg_call(act, kin, lf_hi, lf_lo, vi, nw_row, tables, s0, batch, seq):
    mstack, masks = tables
    hb = HEADS_PER_STEP
    rb, nblk = _mixer_grid(batch, seq)
    q_blk = V_W // (hb * DK)
    g_blk = (V_W + QK_W) // (hb * DV)
    i_blk = V_W // (hb * DV)
    kern = functools.partial(_hg_kernel, hb=hb, nchunks=rb // CHUNK)
    row_blk = lambda off: (lambda b, h, r: (b * nblk + r, off + h))
    return pl.pallas_call(
        kern,
        grid=(batch, HEADS // hb, nblk),
        in_specs=[
            pl.BlockSpec((rb, hb * DK), row_blk(q_blk)),
            pl.BlockSpec((rb, hb * DK), row_blk(0)),
            pl.BlockSpec((rb, hb * DK), row_blk(0)),
            pl.BlockSpec((rb, hb * DK), row_blk(0)),
            pl.BlockSpec((rb, hb * DV), row_blk(i_blk)),
            pl.BlockSpec((rb, hb * DV), row_blk(g_blk)),
            pl.BlockSpec((1, DV), lambda b, h, r: (0, 0)),
            pl.BlockSpec(mstack.shape, lambda b, h, r: (0, 0)),
            pl.BlockSpec(masks.shape, lambda b, h, r: (0, 0, 0)),
            pl.BlockSpec((hb, DV, DK), lambda b, h, r: (h, 0, 0)),
        ],
        out_specs=[
            pl.BlockSpec((rb, hb * DV), row_blk(0)),
            pl.BlockSpec((None, hb, DV, DK), lambda b, h, r: (b, h, 0, 0)),
        ],
        out_shape=[jax.ShapeDtypeStruct((batch * seq, V_W), BF16),
                   jax.ShapeDtypeStruct((batch, HEADS, DV, DK), F32)],
        scratch_shapes=[pltpu.VMEM((hb, DV, DK), F32)],
        compiler_params=_params("parallel", "parallel", "arbitrary", flags=MIXER_FLAGS),
        name="hgrn2",
    )(act, kin, lf_hi, lf_lo, vi, act, nw_row, mstack, masks, s0)


def _merge_kernel(yr_ref, yh_ref, wr_ref, wh_ref, gr_ref, gh_ref, o_ref):
    a = jnp.dot(yr_ref[...], wr_ref[...], preferred_element_type=F32)
    b = jnp.dot(yh_ref[...], wh_ref[...], preferred_element_type=F32)
    o_ref[...] = (gr_ref[...].astype(F32) * a + gh_ref[...].astype(F32) * b).astype(o_ref.dtype)


def _merge_call(yr, yh, gates, w_ret, w_hg, layer):
    rows, k = yr.shape
    d = w_ret.shape[-1]
    tm = min(rows, ROW_TILE)
    tn = 512
    gh_blk = d // tn
    return pl.pallas_call(
        _merge_kernel,
        grid=(rows // tm, d // tn),
        in_specs=[
            pl.BlockSpec((tm, k), lambda i, j: (i, 0)),
            pl.BlockSpec((tm, k), lambda i, j: (i, 0)),
            pl.BlockSpec((None, k, tn), lambda i, j: (layer, 0, j)),
            pl.BlockSpec((None, k, tn), lambda i, j: (layer, 0, j)),
            pl.BlockSpec((tm, tn), lambda i, j: (i, j)),
            pl.BlockSpec((tm, tn), lambda i, j: (i, gh_blk + j)),
        ],
        out_specs=pl.BlockSpec((tm, tn), lambda i, j: (i, j)),
        out_shape=jax.ShapeDtypeStruct((rows, d), BF16),
        compiler_params=_params("parallel", "arbitrary"),
        name="branch_merge",
    )(yr, yh, w_ret, w_hg, gates, gates)


def _out_kernel(y_ref, w_ref, h_ref, post_ref, nxt_ref, hn_ref, xn_ref):
    m = jnp.dot(y_ref[...], w_ref[...], preferred_element_type=F32)
    hn = h_ref[...] + _rms_scale(m) * post_ref[...]
    hn_ref[...] = hn
    xn_ref[...] = (_rms_scale(hn) * nxt_ref[...]).astype(xn_ref.dtype)


def _out_call(y, w_out, h, post_row, next_row, layer):
    rows, d = h.shape
    tm = min(rows, 512)
    row_blk = pl.BlockSpec((tm, d), lambda i: (i, 0))
    vec_blk = pl.BlockSpec((1, d), lambda i: (0, 0))
    return pl.pallas_call(
        _out_kernel,
        grid=(rows // tm,),
        in_specs=[row_blk, pl.BlockSpec((None, d, d), lambda i: (layer, 0, 0)), row_blk, vec_blk, vec_blk],
        out_specs=[row_blk, row_blk],
        out_shape=[jax.ShapeDtypeStruct((rows, d), F32), jax.ShapeDtypeStruct((rows, d), BF16)],
        compiler_params=_params("parallel"),
        name="out_proj",
    )(y, w_out, h, post_row, next_row)


def _ffn_kernel(x_ref, wg_ref, wu_ref, wd_ref, h_ref, post_ref, nxt_ref, hn_ref, *rest, with_next):
    if with_next:
        xn_ref, acc_ref = rest
    else:
        (acc_ref,) = rest
    f = pl.program_id(1)

    @pl.when(f == 0)
    def _():
        acc_ref[...] = jnp.zeros_like(acc_ref)

    x = x_ref[...]
    g = jnp.dot(x, wg_ref[...], preferred_element_type=F32)
    u = jnp.dot(x, wu_ref[...], preferred_element_type=F32)
    act = (g * _sigmoid(g) * u).astype(BF16)
    acc_ref[...] += jnp.dot(act, wd_ref[...], preferred_element_type=F32)

    @pl.when(f == pl.num_programs(1) - 1)
    def _():
        hn = h_ref[...] + _rms_scale(acc_ref[...]) * post_ref[...]
        hn_ref[...] = hn
        if with_next:
            xn_ref[...] = (_rms_scale(hn) * nxt_ref[...]).astype(xn_ref.dtype)


def _ffn_call(xn, w_gate, w_up, w_down, h, post_row, next_row, layer, with_next):
    rows, d = h.shape
    d_ff = w_gate.shape[-1]
    tm = min(rows, 512)
    tf = 512
    row_blk = pl.BlockSpec((tm, d), lambda i, f: (i, 0))
    vec_blk = pl.BlockSpec((1, d), lambda i, f: (0, 0))
    out_specs = [row_blk]
    out_shape = [jax.ShapeDtypeStruct((rows, d), F32)]
    if with_next:
        out_specs.append(row_blk)
        out_shape.append(jax.ShapeDtypeStruct((rows, d), BF16))
    res = pl.pallas_call(
        functools.partial(_ffn_kernel, with_next=with_next),
        grid=(rows // tm, d_ff // tf),
        in_specs=[
            row_blk,
            pl.BlockSpec((None, d, tf), lambda i, f: (layer, 0, f)),
            pl.BlockSpec((None, d, tf), lambda i, f: (layer, 0, f)),
            pl.BlockSpec((None, tf, d), lambda i, f: (layer, f, 0)),
            row_blk, vec_blk, vec_blk,
        ],
        out_specs=out_specs,
        out_shape=out_shape,
        scratch_shapes=[pltpu.VMEM((tm, d), F32)],
        compiler_params=_params("parallel", "arbitrary"),
        name="swiglu_ffn",
    )(xn, w_gate, w_up, w_down, h, post_row, next_row)
    return (res[0], res[1]) if with_next else (res[0], None)


def kernel(x, meta_tokens, norm_mix_pre, norm_mix_post, norm_ffn_pre, norm_ffn_post, w_in, hg_lb_logits,
           hg_norm_w, w_br_ret, w_br_hg, w_out, w_ffn_gate, w_ffn_up, w_ffn_down):
    batch, seq, d = x.shape
    depth = w_in.shape[0]
    assert seq % CHUNK == 0 and meta_tokens.shape == (N_META, d)
    assert w_in.shape[-1] == sum(_IN_WIDTHS) and d == V_W

    lb_sm = jax.nn.softmax(hg_lb_logits.astype(F32), axis=0)
    lbs = jnp.cumsum(lb_sm, axis=0) - lb_sm[0:1]

    wb = [w.astype(BF16) for w in (w_br_ret, w_br_hg, w_out, w_ffn_gate, w_ffn_up, w_ffn_down)]
    w_ret_b, w_hg_b, w_out_b, w_gate_b, w_up_b, w_down_b = wb

    hg_tables = _hg_tables()
    ret_tables = _ret_tables()
    meta_h = jnp.concatenate([jnp.zeros((PAD, d), F32), meta_tokens.astype(F32)], axis=0)
    row_sets = [
        dict(h=meta_h, batch=1, seq=CHUNK, pos0=-PAD),
        dict(h=x.reshape(batch * seq, d).astype(F32), batch=batch, seq=seq, pos0=N_META),
    ]
    for rs in row_sets:
        rs["rot"] = _rot_tables(rs["seq"], rs["pos0"])
        rs["xn"] = _norm_call(rs["h"], norm_mix_pre[0][None])

    for l in range(depth):
        last = l == depth - 1
        ret_state = jnp.zeros((HEADS, DK, DV), F32)
        hg_state = jnp.zeros((HEADS, DV, DK), F32)
        for rs in row_sets:
            is_meta = rs["pos0"] < 0
            proj = functools.partial(_proj_call, rs["xn"], w_in, l, seq=rs["seq"], pos0=rs["pos0"])
            qk = proj(ROT_TILES, "rot", extra=rs["rot"])
            vi = proj(IDENT_TILES, "ident")
            act = proj(SILU_TILES, "silu")
            kin, lf_hi, lf_lo = proj(FORGET_TILES, "forget", extra=(lbs[l][None],))
            yr, ret_fin = _ret_call(qk, vi, act, ret_tables, ret_state, rs["batch"], rs["seq"], rs["pos0"])
            yh, hg_fin = _hg_call(act, kin, lf_hi, lf_lo, vi, hg_norm_w[l][None], hg_tables, hg_state,
                                  rs["batch"], rs["seq"])
            if is_meta:
                ret_state, hg_state = ret_fin[0], hg_fin[0]
                if last:
                    continue
            gates = proj(SIGMOID_TILES, "sigmoid")
            y = _merge_call(yr, yh, gates, w_ret_b, w_hg_b, l)
            h_mid, xn_ffn = _out_call(y, w_out_b, rs["h"], norm_mix_post[l][None], norm_ffn_pre[l][None], l)
            next_row = norm_mix_pre[min(l + 1, depth - 1)][None]
            rs["h"], rs["xn"] = _ffn_call(xn_ffn, w_gate_b, w_up_b, w_down_b, h_mid, norm_ffn_post[l][None],
                                          next_row, l, with_next=not last)
    return row_sets[1]["h"].reshape(batch, seq, d)
```

```python
import functools

import numpy as np
import jax
import jax.numpy as jnp
from jax import lax
from jax.experimental import pallas as pl
from jax.experimental.pallas import tpu as pltpu

N_META = 16
HEADS = 8
DK = 128
DV = 256
CHUNK = 128
PAD = CHUNK - N_META
RMS_EPS = 1e-6
ROPE_BASE = 10000.0
LEVELS = 7
SUBLANES = 8
SMALL_LEVELS = 3
QK_W = HEADS * DK
V_W = HEADS * DV

VMEM_LIMIT_BYTES = 56 * 1024 * 1024
ROW_TILE = 1024
IN_TILE = 1024
PROJ_SLAB = 512
PROJ_VMEM_BUDGET = 50 * 1024 * 1024
HEADS_PER_STEP = 4
MIXER_ROWS = 1024
MIXER_UNROLL = 2

_IN_WIDTHS = (QK_W, QK_W, V_W, V_W, QK_W, QK_W, V_W, V_W, V_W, V_W)
_IN_STARTS = tuple(sum(_IN_WIDTHS[:i]) // IN_TILE for i in range(len(_IN_WIDTHS)))


def _tiles(*parts):
    return tuple(t for p in parts for t in range(_IN_STARTS[p], _IN_STARTS[p] + _IN_WIDTHS[p] // IN_TILE))


ROT_TILES = _tiles(0, 1)
IDENT_TILES = _tiles(2, 6)
SILU_TILES = _tiles(3, 4, 7)
FORGET_TILES = _tiles(5)
SIGMOID_TILES = _tiles(8, 9)

F32 = jnp.float32
BF16 = jnp.bfloat16
NT_DIMS = (((1,), (1,)), ((), ()))
TN_DIMS = (((0,), (0,)), ((), ()))


def _params(*semantics, flags=None):
    return pltpu.CompilerParams(dimension_semantics=semantics, vmem_limit_bytes=VMEM_LIMIT_BYTES, flags=flags)


MIXER_FLAGS = None


def _sigmoid(x):
    return 1.0 / (1.0 + jnp.exp(-x))


def _rms_scale(x):
    return x * lax.rsqrt(jnp.mean(x * x, axis=-1, keepdims=True) + RMS_EPS)


def _tile_lookup(tiles):
    def lookup(j):
        out = tiles[-1]
        for idx in range(len(tiles) - 2, -1, -1):
            out = jnp.where(j == idx, tiles[idx], out)
        return out
    return lookup


def _norm_kernel(h_ref, w_ref, o_ref):
    o_ref[...] = (_rms_scale(h_ref[...]) * w_ref[...]).astype(o_ref.dtype)


def _norm_call(h, w_row):
    rows, d = h.shape
    tm = min(rows, 512)
    return pl.pallas_call(
        _norm_kernel,
        grid=(rows // tm,),
        in_specs=[pl.BlockSpec((tm, d), lambda i: (i, 0)), pl.BlockSpec((1, d), lambda i: (0, 0))],
        out_specs=pl.BlockSpec((tm, d), lambda i: (i, 0)),
        out_shape=jax.ShapeDtypeStruct((rows, d), BF16),
        compiler_params=_params("parallel"),
        name="rms_norm",
    )(h, w_row)


def _proj_kernel(x_ref, w_ref, *rest, mode, pos0):
    wb_ref = rest[-1]
    i = pl.program_id(1)

    @pl.when(i == 0)
    def _():
        wb_ref[...] = w_ref[...].astype(BF16)

    tm = x_ref.shape[0]
    slab = min(tm, PROJ_SLAB)
    for s0 in range(0, tm, slab):
        r = slice(s0, s0 + slab)
        acc = jnp.dot(x_ref[r, :], wb_ref[...], preferred_element_type=F32)
        if mode == "ident":
            rest[0][r, :] = acc.astype(BF16)
        elif mode == "silu":
            rest[0][r, :] = (acc * _sigmoid(acc)).astype(BF16)
        elif mode == "sigmoid":
            rest[0][r, :] = _sigmoid(acc).astype(BF16)
        elif mode == "rot":
            cos_ref, sin_ref, o_ref = rest[:3]
            cos, sin = cos_ref[r, :], sin_ref[r, :]
            for g in range(IN_TILE // DK):
                xg = acc[:, g * DK:(g + 1) * DK]
                o_ref[r, g * DK:(g + 1) * DK] = (xg * cos + pltpu.roll(xg, DK // 2, 1) * sin).astype(BF16)
        elif mode == "forget":
            lb_ref, kin_ref, hi_ref, lo_ref = rest[:4]
            one_minus_f = (1.0 - lb_ref[...]) / (1.0 + jnp.exp(acc))
            log_f = jnp.log1p(-one_minus_f)
            if pos0 < 0:
                valid = (lax.broadcasted_iota(jnp.int32, (slab, 1), 0) + (i * tm + s0 + pos0)) >= 0
                log_f = jnp.where(valid, log_f, 0.0)
                one_minus_f = jnp.where(valid, one_minus_f, 0.0)
            kin_ref[r, :] = one_minus_f.astype(BF16)
            hi = log_f.astype(BF16)
            hi_ref[r, :] = hi
            lo_ref[r, :] = (log_f - hi.astype(F32)).astype(BF16)


def _proj_row_tile(rows, k, n_out):
    for tm in (2 * ROW_TILE, ROW_TILE):
        x_bytes = 2 * tm * k * 2
        w_bytes = 2 * k * IN_TILE * 4 + k * IN_TILE * 2
        out_bytes = n_out * 2 * tm * IN_TILE * 2
        acc_bytes = 2 * PROJ_SLAB * IN_TILE * 4
        if tm <= rows and rows % tm == 0 and x_bytes + w_bytes + out_bytes + acc_bytes <= PROJ_VMEM_BUDGET:
            return tm
    return min(rows, ROW_TILE)


def _proj_call(xn, w_in, layer, tiles, mode, seq, pos0, extra=()):
    rows, k = xn.shape
    n_out = 3 if mode == "forget" else 1
    tm = _proj_row_tile(rows, k, n_out)
    lookup = _tile_lookup(tiles)
    out_cols = len(tiles) * IN_TILE
    in_specs = [pl.BlockSpec((tm, k), lambda j, i: (i, 0)),
                pl.BlockSpec((None, k, IN_TILE), lambda j, i: (layer, 0, lookup(j)))]
    if mode == "rot":
        blocks_per_seq = seq // tm
        tab = pl.BlockSpec((None, tm, DK), lambda j, i: (j, i % blocks_per_seq, 0))
        in_specs += [tab, tab]
    elif mode == "forget":
        in_specs += [pl.BlockSpec((1, IN_TILE), lambda j, i: (0, 0))]
    out_spec = pl.BlockSpec((tm, IN_TILE), lambda j, i: (i, j))
    res = pl.pallas_call(
        functools.partial(_proj_kernel, mode=mode, pos0=pos0),
        grid=(len(tiles), rows // tm),
        in_specs=in_specs,
        out_specs=[out_spec] * n_out,
        out_shape=[jax.ShapeDtypeStruct((rows, out_cols), BF16)] * n_out,
        scratch_shapes=[pltpu.VMEM((k, IN_TILE), BF16)],
        compiler_params=_params("parallel", "arbitrary"),
        name="in_proj_" + mode,
    )(xn, w_in, *extra)
    return res if n_out > 1 else res[0]


def _rot_tables(seq, pos0):
    half = DK // 2
    inv = ROPE_BASE ** (-jnp.arange(half, dtype=F32) / half)
    pos = jnp.arange(seq, dtype=jnp.int32) + pos0
    ang = pos.astype(F32)[:, None] * inv[None, :]
    cos, sin = jnp.cos(ang), jnp.sin(ang)
    cos2 = jnp.concatenate([cos, cos], axis=1)
    sin2 = jnp.concatenate([-sin, sin], axis=1)
    scale = DK ** -0.5
    return jnp.stack([cos2, cos2 * scale]), jnp.stack([sin2, sin2 * scale])


def _ret_kernel(q_ref, k_ref, v_ref, g_ref, dmat_ref, qdec_ref, kdec_ref, cdec_ref, s0_ref,
                o_ref, sfin_ref, s_ref, *, hb, nchunks, pos0):
    @pl.when(pl.program_id(2) == 0)
    def _():
        s_ref[...] = s0_ref[...]

    def body(c, carry):
        r0 = pl.multiple_of(c * CHUNK, CHUNK)
        rows = pl.ds(r0, CHUNK)
        if pos0 < 0:
            valid = (lax.broadcasted_iota(jnp.int32, (CHUNK, 1), 0) + (r0 + pos0)) >= 0
        for j in range(hb):
            q = q_ref[rows, j * DK:(j + 1) * DK]
            k = k_ref[rows, j * DK:(j + 1) * DK]
            if pos0 < 0:
                k = jnp.where(valid, k, jnp.zeros_like(k))
            v = v_ref[rows, j * DV:(j + 1) * DV]
            scores = jnp.dot(q, k.T, preferred_element_type=F32) * dmat_ref[j]
            s = s_ref[j]
            lhs = jnp.concatenate([scores.astype(BF16), q * qdec_ref[j]], axis=1)
            rhs = jnp.concatenate([v, s.astype(BF16)], axis=0)
            y = jnp.dot(lhs, rhs, preferred_element_type=F32)
            s_ref[j] = s * cdec_ref[j] + lax.dot_general(k * kdec_ref[j], v, TN_DIMS, preferred_element_type=F32)
            g = g_ref[rows, j * DV:(j + 1) * DV].astype(F32)
            o_ref[rows, j * DV:(j + 1) * DV] = (_rms_scale(y) * g).astype(o_ref.dtype)
        return carry

    lax.fori_loop(0, nchunks, body, 0, unroll=MIXER_UNROLL)
    sfin_ref[...] = s_ref[...]


def _ret_tables():
    log_g = jnp.log1p(-jnp.exp2(-5.0 - jnp.arange(HEADS, dtype=F32)))
    idx = jnp.arange(CHUNK, dtype=F32)
    diff = idx[:, None] - idx[None, :]
    dmat = jnp.where(diff[None] >= 0, jnp.exp(jnp.maximum(diff, 0.0)[None] * log_g[:, None, None]), 0.0)
    qdec = jnp.exp((idx + 1)[None, :] * log_g[:, None])
    kdec = jnp.exp((CHUNK - 1 - idx)[None, :] * log_g[:, None])
    cdec = jnp.exp(CHUNK * log_g)
    qdec = jnp.broadcast_to(qdec[:, :, None], (HEADS, CHUNK, DK)).astype(BF16)
    kdec = jnp.broadcast_to(kdec[:, :, None], (HEADS, CHUNK, DK)).astype(BF16)
    cdec = jnp.broadcast_to(cdec[:, None, None], (HEADS, 1, DV))
    return dmat, qdec, kdec, cdec


def _mixer_grid(batch, seq):
    rb = min(seq, MIXER_ROWS)
    return rb, seq // rb


def _ret_call(qk, vi, act, tables, s0, batch, seq, pos0):
    dmat, qdec, kdec, cdec = tables
    hb = HEADS_PER_STEP
    rb, nblk = _mixer_grid(batch, seq)
    k_blk = QK_W // (hb * DK)
    kern = functools.partial(_ret_kernel, hb=hb, nchunks=rb // CHUNK, pos0=pos0)
    per_head = lambda b, h, r: (h, 0, 0)
    return pl.pallas_call(
        kern,
        grid=(batch, HEADS // hb, nblk),
        in_specs=[
            pl.BlockSpec((rb, hb * DK), lambda b, h, r: (b * nblk + r, h)),
            pl.BlockSpec((rb, hb * DK), lambda b, h, r: (b * nblk + r, k_blk + h)),
            pl.BlockSpec((rb, hb * DV), lambda b, h, r: (b * nblk + r, h)),
            pl.BlockSpec((rb, hb * DV), lambda b, h, r: (b * nblk + r, h)),
            pl.BlockSpec((hb, CHUNK, CHUNK), per_head),
            pl.BlockSpec((hb, CHUNK, DK), per_head),
            pl.BlockSpec((hb, CHUNK, DK), per_head),
            pl.BlockSpec((hb, 1, DV), per_head),
            pl.BlockSpec((hb, DK, DV), per_head),
        ],
        out_specs=[
            pl.BlockSpec((rb, hb * DV), lambda b, h, r: (b * nblk + r, h)),
            pl.BlockSpec((None, hb, DK, DV), lambda b, h, r: (b, h, 0, 0)),
        ],
        out_shape=[jax.ShapeDtypeStruct((batch * seq, V_W), BF16),
                   jax.ShapeDtypeStruct((batch, HEADS, DK, DV), F32)],
        scratch_shapes=[pltpu.VMEM((hb, DK, DV), F32)],
        compiler_params=_params("parallel", "parallel", "arbitrary", flags=MIXER_FLAGS),
        name="retention",
    )(qk, qk, vi, act, dmat, qdec, kdec, cdec, s0)


def _hg_tables():
    t = np.arange(CHUNK)[:, None]
    u = np.arange(CHUNK)[None, :]
    mats, masks = [], []
    for j in range(LEVELS):
        m = 1 << j
        upper = ((t >> j) & 1) == 1
        q_part = upper & (u >= (t & ~(m - 1))) & (u <= t)
        k_part = (~upper) & (u > t) & (u <= (t | (m - 1)))
        if m < SUBLANES:
            mats.append(q_part | k_part)
        masks.append(((t >> (j + 1)) == (u >> (j + 1))) & upper & (((u >> j) & 1) == 0))
    mats.append(u <= t)
    masks.append(t == u)
    mstack = np.concatenate(mats, axis=0).astype(np.float32)
    mstack = np.concatenate([mstack, mstack], axis=1)
    return jnp.asarray(mstack, dtype=BF16), jnp.asarray(np.stack(masks).astype(np.float32))


def _hg_kernel(q_ref, kin_ref, hi_ref, lo_ref, v_ref, g_ref, nw_ref, mstack_ref, masks_ref, s0_ref,
               o_ref, sfin_ref, st_ref, e_ref, *, hb, nchunks):
    @pl.when(pl.program_id(2) == 0)
    def _():
        st_ref[...] = s0_ref[...]

    row = lax.broadcasted_iota(jnp.int32, (CHUNK, 1), 0)

    def exponents(c, slot, pair):
        rows = pl.ds(pl.multiple_of(c * CHUNK, CHUNK), CHUNK)
        lanes = slice(pair * 2 * DK, (pair + 1) * 2 * DK)
        pieces = jnp.concatenate([hi_ref[rows, lanes], lo_ref[rows, lanes]], axis=0)
        e_ref[slot, :, lanes] = jnp.dot(mstack_ref[...], pieces, preferred_element_type=F32)

    def process(c, slot, nxt):
        rows = pl.ds(pl.multiple_of(c * CHUNK, CHUNK), CHUNK)
        for j in range(hb):
            if nxt is not None and j % 2 == 0:
                exponents(nxt, 1 - slot, j // 2)
            expo = e_ref.at[slot, :, j * DK:(j + 1) * DK]
            qb = q_ref[rows, j * DK:(j + 1) * DK]
            q, kin = qb.astype(F32), kin_ref[rows, j * DK:(j + 1) * DK].astype(F32)
            b = expo[SMALL_LEVELS * CHUNK:, :]
            a = jnp.dot(qb, kin.T.astype(BF16), preferred_element_type=F32) * masks_ref[LEVELS]
            a_rows = None
            for lev in range(LEVELS):
                m = 1 << lev
                if m < SUBLANES:
                    e = jnp.exp(expo[lev * CHUNK:(lev + 1) * CHUNK, :])
                    x = jnp.where(((row >> lev) & 1) == 1, q, kin) * e
                    a = a + jnp.dot(x.astype(BF16), x.T.astype(BF16), preferred_element_type=F32) * masks_ref[lev]
                else:
                    if a_rows is None:
                        a_rows = [a[s:s + SUBLANES] for s in range(0, CHUNK, SUBLANES)]
                    b_ref_rows = jnp.concatenate(
                        [jnp.broadcast_to(b[s0 + m - 1:s0 + m, :], (2 * m, DK)) for s0 in range(0, CHUNK, 2 * m)],
                        axis=0)
                    e = jnp.exp(-jnp.abs(b - b_ref_rows))
                    ke_t = (kin * e).T.astype(BF16)
                    upper = [s for s0 in range(m, CHUNK, 2 * m) for s in range(s0, s0 + m, SUBLANES)]
                    qe = jnp.concatenate([q[s:s + SUBLANES] * e[s:s + SUBLANES] for s in upper], axis=0)
                    p = jnp.dot(qe.astype(BF16), ke_t, preferred_element_type=F32)
                    for idx, s in enumerate(upper):
                        a_rows[s // SUBLANES] = (a_rows[s // SUBLANES] + p[idx * SUBLANES:(idx + 1) * SUBLANES]
                                                 * masks_ref[lev, s:s + SUBLANES, :])
            a_bf = jnp.concatenate(a_rows, axis=0).astype(BF16)
            total = b[CHUNK - 1:CHUNK, :]
            v = v_ref[rows, j * DV:(j + 1) * DV]
            st = st_ref[j]
            lhs = jnp.concatenate([a_bf, (q * jnp.exp(b)).astype(BF16)], axis=1)
            y = jnp.dot(lhs, jnp.concatenate([v, st.astype(BF16)], axis=0), preferred_element_type=F32)
            k_end = (kin * jnp.exp(total - b)).astype(BF16)
            decay = jnp.broadcast_to(jnp.exp(total), (CHUNK, DK)).T
            st_ref[j] = (st * jnp.concatenate([decay] * (DV // DK), axis=1)
                         + lax.dot_general(k_end, v, TN_DIMS, preferred_element_type=F32))
            g = g_ref[rows, j * DV:(j + 1) * DV].astype(F32)
            o_ref[rows, j * DV:(j + 1) * DV] = (_rms_scale(y) * nw_ref[...] * g).astype(o_ref.dtype)

    for pr in range(hb // 2):
        exponents(0, 0, pr)
    if nchunks == 1:
        process(0, 0, None)
    else:
        assert nchunks % 2 == 0

        def pair(p, carry):
            c = 2 * p
            process(c, 0, c + 1)
            process(c + 1, 1, jnp.minimum(c + 2, nchunks - 1))
            return carry

        lax.fori_loop(0, nchunks // 2, pair, 0)
    sfin_ref[...] = st_ref[...]


def _hg_call(act, kin, lf_hi, lf_lo, vi, nw_row, tables, s0, batch, seq):
    mstack, masks = tables
    hb = HEADS_PER_STEP
    rb, nblk = _mixer_grid(batch, seq)
    q_blk = V_W // (hb * DK)
    g_blk = (V_W + QK_W) // (hb * DV)
    i_blk = V_W // (hb * DV)
    kern = functools.partial(_hg_kernel, hb=hb, nchunks=rb // CHUNK)
    row_blk = lambda off: (lambda b, h, r: (b * nblk + r, off + h))
    return pl.pallas_call(
        kern,
        grid=(batch, HEADS // hb, nblk),
        in_specs=[
            pl.BlockSpec((rb, hb * DK), row_blk(q_blk)),
            pl.BlockSpec((rb, hb * DK), row_blk(0)),
            pl.BlockSpec((rb, hb * DK), row_blk(0)),
            pl.BlockSpec((rb, hb * DK), row_blk(0)),
            pl.BlockSpec((rb, hb * DV), row_blk(i_blk)),
            pl.BlockSpec((rb, hb * DV), row_blk(g_blk)),
            pl.BlockSpec((1, DV), lambda b, h, r: (0, 0)),
            pl.BlockSpec(mstack.shape, lambda b, h, r: (0, 0)),
            pl.BlockSpec(masks.shape, lambda b, h, r: (0, 0, 0)),
            pl.BlockSpec((hb, DK, DV), lambda b, h, r: (h, 0, 0)),
        ],
        out_specs=[
            pl.BlockSpec((rb, hb * DV), row_blk(0)),
            pl.BlockSpec((None, hb, DK, DV), lambda b, h, r: (b, h, 0, 0)),
        ],
        out_shape=[jax.ShapeDtypeStruct((batch * seq, V_W), BF16),
                   jax.ShapeDtypeStruct((batch, HEADS, DK, DV), F32)],
        scratch_shapes=[pltpu.VMEM((hb, DK, DV), F32),
                        pltpu.VMEM((2, (SMALL_LEVELS + 1) * CHUNK, hb * DK), F32)],
        compiler_params=_params("parallel", "parallel", "arbitrary", flags=MIXER_FLAGS),
        name="hgrn2",
    )(act, kin, lf_hi, lf_lo, vi, act, nw_row, mstack, masks, s0)


def _merge_kernel(yr_ref, yh_ref, wr_ref, wh_ref, gr_ref, gh_ref, o_ref):
    a = jnp.dot(yr_ref[...], wr_ref[...], preferred_element_type=F32)
    b = jnp.dot(yh_ref[...], wh_ref[...], preferred_element_type=F32)
    o_ref[...] = (gr_ref[...].astype(F32) * a + gh_ref[...].astype(F32) * b).astype(o_ref.dtype)


def _merge_call(yr, yh, gates, w_ret, w_hg, layer):
    rows, k = yr.shape
    d = w_ret.shape[-1]
    tm = min(rows, ROW_TILE)
    tn = 512
    gh_blk = d // tn
    return pl.pallas_call(
        _merge_kernel,
        grid=(rows // tm, d // tn),
        in_specs=[
            pl.BlockSpec((tm, k), lambda i, j: (i, 0)),
            pl.BlockSpec((tm, k), lambda i, j: (i, 0)),
            pl.BlockSpec((None, k, tn), lambda i, j: (layer, 0, j)),
            pl.BlockSpec((None, k, tn), lambda i, j: (layer, 0, j)),
            pl.BlockSpec((tm, tn), lambda i, j: (i, j)),
            pl.BlockSpec((tm, tn), lambda i, j: (i, gh_blk + j)),
        ],
        out_specs=pl.BlockSpec((tm, tn), lambda i, j: (i, j)),
        out_shape=jax.ShapeDtypeStruct((rows, d), BF16),
        compiler_params=_params("parallel", "arbitrary"),
        name="branch_merge",
    )(yr, yh, w_ret, w_hg, gates, gates)


def _out_kernel(y_ref, w_ref, h_ref, post_ref, nxt_ref, hn_ref, xn_ref):
    m = jnp.dot(y_ref[...], w_ref[...], preferred_element_type=F32)
    hn = h_ref[...] + _rms_scale(m) * post_ref[...]
    hn_ref[...] = hn
    xn_ref[...] = (_rms_scale(hn) * nxt_ref[...]).astype(xn_ref.dtype)


def _out_call(y, w_out, h, post_row, next_row, layer):
    rows, d = h.shape
    tm = min(rows, 512)
    row_blk = pl.BlockSpec((tm, d), lambda i: (i, 0))
    vec_blk = pl.BlockSpec((1, d), lambda i: (0, 0))
    return pl.pallas_call(
        _out_kernel,
        grid=(rows // tm,),
        in_specs=[row_blk, pl.BlockSpec((None, d, d), lambda i: (layer, 0, 0)), row_blk, vec_blk, vec_blk],
        out_specs=[row_blk, row_blk],
        out_shape=[jax.ShapeDtypeStruct((rows, d), F32), jax.ShapeDtypeStruct((rows, d), BF16)],
        compiler_params=_params("parallel"),
        name="out_proj",
    )(y, w_out, h, post_row, next_row)


def _ffn_kernel(x_ref, wg_ref, wu_ref, wd_ref, h_ref, post_ref, nxt_ref, hn_ref, *rest, with_next):
    if with_next:
        xn_ref, acc_ref = rest
    else:
        (acc_ref,) = rest
    f = pl.program_id(1)

    @pl.when(f == 0)
    def _():
        acc_ref[...] = jnp.zeros_like(acc_ref)

    x = x_ref[...]
    g = jnp.dot(x, wg_ref[...], preferred_element_type=F32)
    u = jnp.dot(x, wu_ref[...], preferred_element_type=F32)
    act = (g * _sigmoid(g) * u).astype(BF16)
    acc_ref[...] += jnp.dot(act, wd_ref[...], preferred_element_type=F32)

    @pl.when(f == pl.num_programs(1) - 1)
    def _():
        hn = h_ref[...] + _rms_scale(acc_ref[...]) * post_ref[...]
        hn_ref[...] = hn
        if with_next:
            xn_ref[...] = (_rms_scale(hn) * nxt_ref[...]).astype(xn_ref.dtype)


def _ffn_call(xn, w_gate, w_up, w_down, h, post_row, next_row, layer, with_next):
    rows, d = h.shape
    d_ff = w_gate.shape[-1]
    tm = min(rows, 512)
    tf = 512
    row_blk = pl.BlockSpec((tm, d), lambda i, f: (i, 0))
    vec_blk = pl.BlockSpec((1, d), lambda i, f: (0, 0))
    out_specs = [row_blk]
    out_shape = [jax.ShapeDtypeStruct((rows, d), F32)]
    if with_next:
        out_specs.append(row_blk)
        out_shape.append(jax.ShapeDtypeStruct((rows, d), BF16))
    res = pl.pallas_call(
        functools.partial(_ffn_kernel, with_next=with_next),
        grid=(rows // tm, d_ff // tf),
        in_specs=[
            row_blk,
            pl.BlockSpec((None, d, tf), lambda i, f: (layer, 0, f)),
            pl.BlockSpec((None, d, tf), lambda i, f: (layer, 0, f)),
            pl.BlockSpec((None, tf, d), lambda i, f: (layer, f, 0)),
            row_blk, vec_blk, vec_blk,
        ],
        out_specs=out_specs,
        out_shape=out_shape,
        scratch_shapes=[pltpu.VMEM((tm, d), F32)],
        compiler_params=_params("parallel", "arbitrary"),
        name="swiglu_ffn",
    )(xn, w_gate, w_up, w_down, h, post_row, next_row)
    return (res[0], res[1]) if with_next else (res[0], None)


def kernel(x, meta_tokens, norm_mix_pre, norm_mix_post, norm_ffn_pre, norm_ffn_post, w_in, hg_lb_logits,
           hg_norm_w, w_br_ret, w_br_hg, w_out, w_ffn_gate, w_ffn_up, w_ffn_down):
    batch, seq, d = x.shape
    depth = w_in.shape[0]
    assert seq % CHUNK == 0 and meta_tokens.shape == (N_META, d)
    assert w_in.shape[-1] == sum(_IN_WIDTHS) and d == V_W

    lb_sm = jax.nn.softmax(hg_lb_logits.astype(F32), axis=0)
    lbs = jnp.cumsum(lb_sm, axis=0) - lb_sm[0:1]

    wb = [w.astype(BF16) for w in (w_br_ret, w_br_hg, w_out, w_ffn_gate, w_ffn_up, w_ffn_down)]
    w_ret_b, w_hg_b, w_out_b, w_gate_b, w_up_b, w_down_b = wb

    hg_tables = _hg_tables()
    ret_tables = _ret_tables()
    meta_h = jnp.concatenate([jnp.zeros((PAD, d), F32), meta_tokens.astype(F32)], axis=0)
    row_sets = [
        dict(h=meta_h, batch=1, seq=CHUNK, pos0=-PAD),
        dict(h=x.reshape(batch * seq, d).astype(F32), batch=batch, seq=seq, pos0=N_META),
    ]
    for rs in row_sets:
        rs["rot"] = _rot_tables(rs["seq"], rs["pos0"])
        rs["xn"] = _norm_call(rs["h"], norm_mix_pre[0][None])

    for l in range(depth):
        last = l == depth - 1
        ret_state = jnp.zeros((HEADS, DK, DV), F32)
        hg_state = jnp.zeros((HEADS, DK, DV), F32)
        for rs in row_sets:
            is_meta = rs["pos0"] < 0
            proj = functools.partial(_proj_call, rs["xn"], w_in, l, seq=rs["seq"], pos0=rs["pos0"])
            qk = proj(ROT_TILES, "rot", extra=rs["rot"])
            vi = proj(IDENT_TILES, "ident")
            act = proj(SILU_TILES, "silu")
            kin, lf_hi, lf_lo = proj(FORGET_TILES, "forget", extra=(lbs[l][None],))
            yr, ret_fin = _ret_call(qk, vi, act, ret_tables, ret_state, rs["batch"], rs["seq"], rs["pos0"])
            yh, hg_fin = _hg_call(act, kin, lf_hi, lf_lo, vi, hg_norm_w[l][None], hg_tables, hg_state,
                                  rs["batch"], rs["seq"])
            if is_meta:
                ret_state, hg_state = ret_fin[0], hg_fin[0]
                if last:
                    continue
            gates = proj(SIGMOID_TILES, "sigmoid")
            y = _merge_call(yr, yh, gates, w_ret_b, w_hg_b, l)
            h_mid, xn_ffn = _out_call(y, w_out_b, rs["h"], norm_mix_post[l][None], norm_ffn_pre[l][None], l)
            next_row = norm_mix_pre[min(l + 1, depth - 1)][None]
            rs["h"], rs["xn"] = _ffn_call(xn_ffn, w_gate_b, w_up_b, w_down_b, h_mid, norm_ffn_post[l][None],
                                          next_row, l, with_next=not last)
    return row_sets[1]["h"].reshape(batch, seq, d)
```

```python
import functools

import numpy as np
import jax
import jax.numpy as jnp
from jax import lax
from jax.experimental import pallas as pl
from jax.experimental.pallas import tpu as pltpu

N_META = 16
HEADS = 8
DK = 128
DV = 256
CHUNK = 128
PAD = CHUNK - N_META
RMS_EPS = 1e-6
ROPE_BASE = 10000.0
LEVELS = 7
SUBLANES = 8
SMALL_LEVELS = 3
QK_W = HEADS * DK
V_W = HEADS * DV

VMEM_LIMIT_BYTES = 56 * 1024 * 1024
ROW_TILE = 1024
IN_TILE = 1024
PROJ_SLAB = 512
PROJ_VMEM_BUDGET = 50 * 1024 * 1024
HEADS_PER_STEP = 4
MIXER_ROWS = 1024
MIXER_UNROLL = 2

_IN_WIDTHS = (QK_W, QK_W, V_W, V_W, QK_W, QK_W, V_W, V_W, V_W, V_W)
_IN_STARTS = tuple(sum(_IN_WIDTHS[:i]) // IN_TILE for i in range(len(_IN_WIDTHS)))


def _tiles(*parts):
    return tuple(t for p in parts for t in range(_IN_STARTS[p], _IN_STARTS[p] + _IN_WIDTHS[p] // IN_TILE))


ROT_TILES = _tiles(0, 1)
IDENT_TILES = _tiles(2, 6)
SILU_TILES = _tiles(3, 4, 7)
FORGET_TILES = _tiles(5)
SIGMOID_TILES = _tiles(8, 9)

F32 = jnp.float32
BF16 = jnp.bfloat16
NT_DIMS = (((1,), (1,)), ((), ()))
TN_DIMS = (((0,), (0,)), ((), ()))


def _params(*semantics, flags=None):
    return pltpu.CompilerParams(dimension_semantics=semantics, vmem_limit_bytes=VMEM_LIMIT_BYTES, flags=flags)


MIXER_FLAGS = None


def _sigmoid(x):
    return 0.5 + 0.5 * jnp.tanh(0.5 * x)


def _silu(x):
    h = 0.5 * x
    return h + h * jnp.tanh(h)


def _rms_scale(x):
    return x * lax.rsqrt(jnp.mean(x * x, axis=-1, keepdims=True) + RMS_EPS)


def _tile_lookup(tiles):
    def lookup(j):
        out = tiles[-1]
        for idx in range(len(tiles) - 2, -1, -1):
            out = jnp.where(j == idx, tiles[idx], out)
        return out
    return lookup


def _norm_kernel(h_ref, w_ref, o_ref):
    o_ref[...] = (_rms_scale(h_ref[...]) * w_ref[...]).astype(o_ref.dtype)


def _norm_call(h, w_row):
    rows, d = h.shape
    tm = min(rows, 512)
    return pl.pallas_call(
        _norm_kernel,
        grid=(rows // tm,),
        in_specs=[pl.BlockSpec((tm, d), lambda i: (i, 0)), pl.BlockSpec((1, d), lambda i: (0, 0))],
        out_specs=pl.BlockSpec((tm, d), lambda i: (i, 0)),
        out_shape=jax.ShapeDtypeStruct((rows, d), BF16),
        compiler_params=_params("parallel"),
        name="rms_norm",
    )(h, w_row)


def _proj_kernel(x_ref, w_ref, *rest, mode, pos0):
    wb_ref = rest[-1]
    i = pl.program_id(1)

    @pl.when(i == 0)
    def _():
        wb_ref[...] = w_ref[...].astype(BF16)

    tm = x_ref.shape[0]
    slab = min(tm, PROJ_SLAB)
    for s0 in range(0, tm, slab):
        r = slice(s0, s0 + slab)
        acc = jnp.dot(x_ref[r, :], wb_ref[...], preferred_element_type=F32)
        if mode == "ident":
            rest[0][r, :] = acc.astype(BF16)
        elif mode == "silu":
            rest[0][r, :] = _silu(acc).astype(BF16)
        elif mode == "sigmoid":
            rest[0][r, :] = _sigmoid(acc).astype(BF16)
        elif mode == "rot":
            cos_ref, sin_ref, o_ref = rest[:3]
            cos, sin = cos_ref[r, :], sin_ref[r, :]
            for g in range(IN_TILE // DK):
                xg = acc[:, g * DK:(g + 1) * DK]
                o_ref[r, g * DK:(g + 1) * DK] = (xg * cos + pltpu.roll(xg, DK // 2, 1) * sin).astype(BF16)
        elif mode == "forget":
            lb_ref, kin_ref, hi_ref, lo_ref = rest[:4]
            one_minus_f = (1.0 - lb_ref[...]) * (0.5 - 0.5 * jnp.tanh(0.5 * acc))
            log_f = jnp.log(1.0 - one_minus_f)
            if pos0 < 0:
                valid = (lax.broadcasted_iota(jnp.int32, (slab, 1), 0) + (i * tm + s0 + pos0)) >= 0
                log_f = jnp.where(valid, log_f, 0.0)
                one_minus_f = jnp.where(valid, one_minus_f, 0.0)
            kin_ref[r, :] = one_minus_f.astype(BF16)
            hi = log_f.astype(BF16)
            hi_ref[r, :] = hi
            lo_ref[r, :] = (log_f - hi.astype(F32)).astype(BF16)


def _proj_row_tile(rows, k, n_out):
    for tm in (2 * ROW_TILE, ROW_TILE):
        x_bytes = 2 * tm * k * 2
        w_bytes = 2 * k * IN_TILE * 4 + k * IN_TILE * 2
        out_bytes = n_out * 2 * tm * IN_TILE * 2
        acc_bytes = 2 * PROJ_SLAB * IN_TILE * 4
        if tm <= rows and rows % tm == 0 and x_bytes + w_bytes + out_bytes + acc_bytes <= PROJ_VMEM_BUDGET:
            return tm
    return min(rows, ROW_TILE)


def _proj_call(xn, w_in, layer, tiles, mode, seq, pos0, extra=()):
    rows, k = xn.shape
    n_out = 3 if mode == "forget" else 1
    tm = _proj_row_tile(rows, k, n_out)
    lookup = _tile_lookup(tiles)
    out_cols = len(tiles) * IN_TILE
    in_specs = [pl.BlockSpec((tm, k), lambda j, i: (i, 0)),
                pl.BlockSpec((None, k, IN_TILE), lambda j, i: (layer, 0, lookup(j)))]
    if mode == "rot":
        blocks_per_seq = seq // tm
        tab = pl.BlockSpec((None, tm, DK), lambda j, i: (j, i % blocks_per_seq, 0))
        in_specs += [tab, tab]
    elif mode == "forget":
        in_specs += [pl.BlockSpec((1, IN_TILE), lambda j, i: (0, 0))]
    out_spec = pl.BlockSpec((tm, IN_TILE), lambda j, i: (i, j))
    res = pl.pallas_call(
        functools.partial(_proj_kernel, mode=mode, pos0=pos0),
        grid=(len(tiles), rows // tm),
        in_specs=in_specs,
        out_specs=[out_spec] * n_out,
        out_shape=[jax.ShapeDtypeStruct((rows, out_cols), BF16)] * n_out,
        scratch_shapes=[pltpu.VMEM((k, IN_TILE), BF16)],
        compiler_params=_params("parallel", "arbitrary"),
        name="in_proj_" + mode,
    )(xn, w_in, *extra)
    return res if n_out > 1 else res[0]


def _rot_tables(seq, pos0):
    half = DK // 2
    inv = ROPE_BASE ** (-jnp.arange(half, dtype=F32) / half)
    pos = jnp.arange(seq, dtype=jnp.int32) + pos0
    ang = pos.astype(F32)[:, None] * inv[None, :]
    cos, sin = jnp.cos(ang), jnp.sin(ang)
    cos2 = jnp.concatenate([cos, cos], axis=1)
    sin2 = jnp.concatenate([-sin, sin], axis=1)
    scale = DK ** -0.5
    return jnp.stack([cos2, cos2 * scale]), jnp.stack([sin2, sin2 * scale])


def _ret_kernel(q_ref, k_ref, v_ref, g_ref, dmat_ref, qdec_ref, kdec_ref, cdec_ref, s0_ref,
                o_ref, sfin_ref, s_ref, *, hb, nchunks, pos0):
    @pl.when(pl.program_id(2) == 0)
    def _():
        s_ref[...] = s0_ref[...]

    def body(c, carry):
        r0 = pl.multiple_of(c * CHUNK, CHUNK)
        rows = pl.ds(r0, CHUNK)
        if pos0 < 0:
            valid = (lax.broadcasted_iota(jnp.int32, (CHUNK, 1), 0) + (r0 + pos0)) >= 0
        for j in range(hb):
            q = q_ref[rows, j * DK:(j + 1) * DK]
            k = k_ref[rows, j * DK:(j + 1) * DK]
            if pos0 < 0:
                k = jnp.where(valid, k, jnp.zeros_like(k))
            v = v_ref[rows, j * DV:(j + 1) * DV]
            scores = jnp.dot(q, k.T, preferred_element_type=F32) * dmat_ref[j]
            s = s_ref[j]
            lhs = jnp.concatenate([scores.astype(BF16), q * qdec_ref[j]], axis=1)
            rhs = jnp.concatenate([v, s.astype(BF16)], axis=0)
            y = jnp.dot(lhs, rhs, preferred_element_type=F32)
            s_ref[j] = s * cdec_ref[j] + lax.dot_general(k * kdec_ref[j], v, TN_DIMS, preferred_element_type=F32)
            g = g_ref[rows, j * DV:(j + 1) * DV].astype(F32)
            o_ref[rows, j * DV:(j + 1) * DV] = (_rms_scale(y) * g).astype(o_ref.dtype)
        return carry

    lax.fori_loop(0, nchunks, body, 0, unroll=MIXER_UNROLL)
    sfin_ref[...] = s_ref[...]


def _ret_tables():
    log_g = jnp.log1p(-jnp.exp2(-5.0 - jnp.arange(HEADS, dtype=F32)))
    idx = jnp.arange(CHUNK, dtype=F32)
    diff = idx[:, None] - idx[None, :]
    dmat = jnp.where(diff[None] >= 0, jnp.exp(jnp.maximum(diff, 0.0)[None] * log_g[:, None, None]), 0.0)
    qdec = jnp.exp((idx + 1)[None, :] * log_g[:, None])
    kdec = jnp.exp((CHUNK - 1 - idx)[None, :] * log_g[:, None])
    cdec = jnp.exp(CHUNK * log_g)
    qdec = jnp.broadcast_to(qdec[:, :, None], (HEADS, CHUNK, DK)).astype(BF16)
    kdec = jnp.broadcast_to(kdec[:, :, None], (HEADS, CHUNK, DK)).astype(BF16)
    cdec = jnp.broadcast_to(cdec[:, None, None], (HEADS, 1, DV))
    return dmat, qdec, kdec, cdec


def _mixer_grid(batch, seq):
    rb = min(seq, MIXER_ROWS)
    return rb, seq // rb


def _ret_call(qk, vi, act, tables, s0, batch, seq, pos0):
    dmat, qdec, kdec, cdec = tables
    hb = HEADS_PER_STEP
    rb, nblk = _mixer_grid(batch, seq)
    k_blk = QK_W // (hb * DK)
    kern = functools.partial(_ret_kernel, hb=hb, nchunks=rb // CHUNK, pos0=pos0)
    per_head = lambda b, h, r: (h, 0, 0)
    return pl.pallas_call(
        kern,
        grid=(batch, HEADS // hb, nblk),
        in_specs=[
            pl.BlockSpec((rb, hb * DK), lambda b, h, r: (b * nblk + r, h)),
            pl.BlockSpec((rb, hb * DK), lambda b, h, r: (b * nblk + r, k_blk + h)),
            pl.BlockSpec((rb, hb * DV), lambda b, h, r: (b * nblk + r, h)),
            pl.BlockSpec((rb, hb * DV), lambda b, h, r: (b * nblk + r, h)),
            pl.BlockSpec((hb, CHUNK, CHUNK), per_head),
            pl.BlockSpec((hb, CHUNK, DK), per_head),
            pl.BlockSpec((hb, CHUNK, DK), per_head),
            pl.BlockSpec((hb, 1, DV), per_head),
            pl.BlockSpec((hb, DK, DV), per_head),
        ],
        out_specs=[
            pl.BlockSpec((rb, hb * DV), lambda b, h, r: (b * nblk + r, h)),
            pl.BlockSpec((None, hb, DK, DV), lambda b, h, r: (b, h, 0, 0)),
        ],
        out_shape=[jax.ShapeDtypeStruct((batch * seq, V_W), BF16),
                   jax.ShapeDtypeStruct((batch, HEADS, DK, DV), F32)],
        scratch_shapes=[pltpu.VMEM((hb, DK, DV), F32)],
        compiler_params=_params("parallel", "parallel", "arbitrary", flags=MIXER_FLAGS),
        name="retention",
    )(qk, qk, vi, act, dmat, qdec, kdec, cdec, s0)


def _hg_tables():
    t = np.arange(CHUNK)[:, None]
    u = np.arange(CHUNK)[None, :]
    mats, masks = [], []
    for j in range(LEVELS):
        m = 1 << j
        upper = ((t >> j) & 1) == 1
        q_part = upper & (u >= (t & ~(m - 1))) & (u <= t)
        k_part = (~upper) & (u > t) & (u <= (t | (m - 1)))
        if m < SUBLANES:
            mats.append(q_part | k_part)
        masks.append(((t >> (j + 1)) == (u >> (j + 1))) & upper & (((u >> j) & 1) == 0))
    mats.append(u <= t)
    masks.append(t == u)
    mstack = np.concatenate(mats, axis=0).astype(np.float32)
    mstack = np.concatenate([mstack, mstack], axis=1)
    return jnp.asarray(mstack, dtype=BF16), jnp.asarray(np.stack(masks).astype(np.float32))


def _hg_kernel(q_ref, kin_ref, hi_ref, lo_ref, v_ref, g_ref, nw_ref, mstack_ref, masks_ref, s0_ref,
               o_ref, sfin_ref, st_ref, e_ref, *, hb, nchunks):
    @pl.when(pl.program_id(2) == 0)
    def _():
        st_ref[...] = s0_ref[...]

    row = lax.broadcasted_iota(jnp.int32, (CHUNK, 1), 0)

    def exponents(c, slot, pair):
        rows = pl.ds(pl.multiple_of(c * CHUNK, CHUNK), CHUNK)
        lanes = slice(pair * 2 * DK, (pair + 1) * 2 * DK)
        pieces = jnp.concatenate([hi_ref[rows, lanes], lo_ref[rows, lanes]], axis=0)
        e_ref[slot, :, lanes] = jnp.dot(mstack_ref[...], pieces, preferred_element_type=F32)

    def process(c, slot, nxt):
        rows = pl.ds(pl.multiple_of(c * CHUNK, CHUNK), CHUNK)
        for j in range(hb):
            if nxt is not None and j % 2 == 0:
                exponents(nxt, 1 - slot, j // 2)
            expo = e_ref.at[slot, :, j * DK:(j + 1) * DK]
            qb = q_ref[rows, j * DK:(j + 1) * DK]
            q, kin = qb.astype(F32), kin_ref[rows, j * DK:(j + 1) * DK].astype(F32)
            b = expo[SMALL_LEVELS * CHUNK:, :]
            a = jnp.dot(qb, kin.T.astype(BF16), preferred_element_type=F32) * masks_ref[LEVELS]
            a_rows = None
            for lev in range(LEVELS):
                m = 1 << lev
                if m < SUBLANES:
                    e = jnp.exp(expo[lev * CHUNK:(lev + 1) * CHUNK, :])
                    x = jnp.where(((row >> lev) & 1) == 1, q, kin) * e
                    a = a + jnp.dot(x.astype(BF16), x.T.astype(BF16), preferred_element_type=F32) * masks_ref[lev]
                else:
                    if a_rows is None:
                        a_rows = [a[s:s + SUBLANES] for s in range(0, CHUNK, SUBLANES)]
                    b_ref_rows = jnp.concatenate(
                        [jnp.broadcast_to(b[s0 + m - 1:s0 + m, :], (2 * m, DK)) for s0 in range(0, CHUNK, 2 * m)],
                        axis=0)
                    e = jnp.exp(-jnp.abs(b - b_ref_rows))
                    ke_t = (kin * e).T.astype(BF16)
                    upper = [s for s0 in range(m, CHUNK, 2 * m) for s in range(s0, s0 + m, SUBLANES)]
                    qe = jnp.concatenate([q[s:s + SUBLANES] * e[s:s + SUBLANES] for s in upper], axis=0)
                    p = jnp.dot(qe.astype(BF16), ke_t, preferred_element_type=F32)
                    for idx, s in enumerate(upper):
                        a_rows[s // SUBLANES] = (a_rows[s // SUBLANES] + p[idx * SUBLANES:(idx + 1) * SUBLANES]
                                                 * masks_ref[lev, s:s + SUBLANES, :])
            a_bf = jnp.concatenate(a_rows, axis=0).astype(BF16)
            total = b[CHUNK - 1:CHUNK, :]
            v = v_ref[rows, j * DV:(j + 1) * DV]
            st = st_ref[j]
            lhs = jnp.concatenate([a_bf, (q * jnp.exp(b)).astype(BF16)], axis=1)
            y = jnp.dot(lhs, jnp.concatenate([v, st.astype(BF16)], axis=0), preferred_element_type=F32)
            k_end = (kin * jnp.exp(total - b)).astype(BF16)
            decay = jnp.broadcast_to(jnp.exp(total), (CHUNK, DK)).T
            st_ref[j] = (st * jnp.concatenate([decay] * (DV // DK), axis=1)
                         + lax.dot_general(k_end, v, TN_DIMS, preferred_element_type=F32))
            g = g_ref[rows, j * DV:(j + 1) * DV].astype(F32)
            o_ref[rows, j * DV:(j + 1) * DV] = (_rms_scale(y) * nw_ref[...] * g).astype(o_ref.dtype)

    for pr in range(hb // 2):
        exponents(0, 0, pr)
    if nchunks == 1:
        process(0, 0, None)
    else:
        assert nchunks % 2 == 0

        def pair(p, carry):
            c = 2 * p
            process(c, 0, c + 1)
            process(c + 1, 1, jnp.minimum(c + 2, nchunks - 1))
            return carry

        lax.fori_loop(0, nchunks // 2, pair, 0)
    sfin_ref[...] = st_ref[...]


def _hg_call(act, kin, lf_hi, lf_lo, vi, nw_row, tables, s0, batch, seq):
    mstack, masks = tables
    hb = HEADS_PER_STEP
    rb, nblk = _mixer_grid(batch, seq)
    q_blk = V_W // (hb * DK)
    g_blk = (V_W + QK_W) // (hb * DV)
    i_blk = V_W // (hb * DV)
    kern = functools.partial(_hg_kernel, hb=hb, nchunks=rb // CHUNK)
    row_blk = lambda off: (lambda b, h, r: (b * nblk + r, off + h))
    return pl.pallas_call(
        kern,
        grid=(batch, HEADS // hb, nblk),
        in_specs=[
            pl.BlockSpec((rb, hb * DK), row_blk(q_blk)),
            pl.BlockSpec((rb, hb * DK), row_blk(0)),
            pl.BlockSpec((rb, hb * DK), row_blk(0)),
            pl.BlockSpec((rb, hb * DK), row_blk(0)),
            pl.BlockSpec((rb, hb * DV), row_blk(i_blk)),
            pl.BlockSpec((rb, hb * DV), row_blk(g_blk)),
            pl.BlockSpec((1, DV), lambda b, h, r: (0, 0)),
            pl.BlockSpec(mstack.shape, lambda b, h, r: (0, 0)),
            pl.BlockSpec(masks.shape, lambda b, h, r: (0, 0, 0)),
            pl.BlockSpec((hb, DK, DV), lambda b, h, r: (h, 0, 0)),
        ],
        out_specs=[
            pl.BlockSpec((rb, hb * DV), row_blk(0)),
            pl.BlockSpec((None, hb, DK, DV), lambda b, h, r: (b, h, 0, 0)),
        ],
        out_shape=[jax.ShapeDtypeStruct((batch * seq, V_W), BF16),
                   jax.ShapeDtypeStruct((batch, HEADS, DK, DV), F32)],
        scratch_shapes=[pltpu.VMEM((hb, DK, DV), F32),
                        pltpu.VMEM((2, (SMALL_LEVELS + 1) * CHUNK, hb * DK), F32)],
        compiler_params=_params("parallel", "parallel", "arbitrary", flags=MIXER_FLAGS),
        name="hgrn2",
    )(act, kin, lf_hi, lf_lo, vi, act, nw_row, mstack, masks, s0)


def _merge_kernel(yr_ref, yh_ref, wr_ref, wh_ref, gr_ref, gh_ref, o_ref):
    a = jnp.dot(yr_ref[...], wr_ref[...], preferred_element_type=F32)
    b = jnp.dot(yh_ref[...], wh_ref[...], preferred_element_type=F32)
    o_ref[...] = (gr_ref[...].astype(F32) * a + gh_ref[...].astype(F32) * b).astype(o_ref.dtype)


def _merge_call(yr, yh, gates, w_ret, w_hg, layer):
    rows, k = yr.shape
    d = w_ret.shape[-1]
    tm = min(rows, ROW_TILE)
    tn = 512
    gh_blk = d // tn
    return pl.pallas_call(
        _merge_kernel,
        grid=(rows // tm, d // tn),
        in_specs=[
            pl.BlockSpec((tm, k), lambda i, j: (i, 0)),
            pl.BlockSpec((tm, k), lambda i, j: (i, 0)),
            pl.BlockSpec((None, k, tn), lambda i, j: (layer, 0, j)),
            pl.BlockSpec((None, k, tn), lambda i, j: (layer, 0, j)),
            pl.BlockSpec((tm, tn), lambda i, j: (i, j)),
            pl.BlockSpec((tm, tn), lambda i, j: (i, gh_blk + j)),
        ],
        out_specs=pl.BlockSpec((tm, tn), lambda i, j: (i, j)),
        out_shape=jax.ShapeDtypeStruct((rows, d), BF16),
        compiler_params=_params("parallel", "arbitrary"),
        name="branch_merge",
    )(yr, yh, w_ret, w_hg, gates, gates)


def _out_kernel(y_ref, w_ref, h_ref, post_ref, nxt_ref, hn_ref, xn_ref):
    m = jnp.dot(y_ref[...], w_ref[...], preferred_element_type=F32)
    hn = h_ref[...] + _rms_scale(m) * post_ref[...]
    hn_ref[...] = hn
    xn_ref[...] = (_rms_scale(hn) * nxt_ref[...]).astype(xn_ref.dtype)


def _out_call(y, w_out, h, post_row, next_row, layer):
    rows, d = h.shape
    tm = min(rows, 512)
    row_blk = pl.BlockSpec((tm, d), lambda i: (i, 0))
    vec_blk = pl.BlockSpec((1, d), lambda i: (0, 0))
    return pl.pallas_call(
        _out_kernel,
        grid=(rows // tm,),
        in_specs=[row_blk, pl.BlockSpec((None, d, d), lambda i: (layer, 0, 0)), row_blk, vec_blk, vec_blk],
        out_specs=[row_blk, row_blk],
        out_shape=[jax.ShapeDtypeStruct((rows, d), F32), jax.ShapeDtypeStruct((rows, d), BF16)],
        compiler_params=_params("parallel"),
        name="out_proj",
    )(y, w_out, h, post_row, next_row)


def _ffn_kernel(x_ref, wg_ref, wu_ref, wd_ref, h_ref, post_ref, nxt_ref, hn_ref, *rest, with_next):
    if with_next:
        xn_ref, acc_ref = rest
    else:
        (acc_ref,) = rest
    f = pl.program_id(1)

    @pl.when(f == 0)
    def _():
        acc_ref[...] = jnp.zeros_like(acc_ref)

    x = x_ref[...]
    g = jnp.dot(x, wg_ref[...], preferred_element_type=F32)
    u = jnp.dot(x, wu_ref[...], preferred_element_type=F32)
    act = (_silu(g) * u).astype(BF16)
    acc_ref[...] += jnp.dot(act, wd_ref[...], preferred_element_type=F32)

    @pl.when(f == pl.num_programs(1) - 1)
    def _():
        hn = h_ref[...] + _rms_scale(acc_ref[...]) * post_ref[...]
        hn_ref[...] = hn
        if with_next:
            xn_ref[...] = (_rms_scale(hn) * nxt_ref[...]).astype(xn_ref.dtype)


def _ffn_call(xn, w_gate, w_up, w_down, h, post_row, next_row, layer, with_next):
    rows, d = h.shape
    d_ff = w_gate.shape[-1]
    tm = min(rows, 512)
    tf = 512
    row_blk = pl.BlockSpec((tm, d), lambda i, f: (i, 0))
    vec_blk = pl.BlockSpec((1, d), lambda i, f: (0, 0))
    out_specs = [row_blk]
    out_shape = [jax.ShapeDtypeStruct((rows, d), F32)]
    if with_next:
        out_specs.append(row_blk)
        out_shape.append(jax.ShapeDtypeStruct((rows, d), BF16))
    res = pl.pallas_call(
        functools.partial(_ffn_kernel, with_next=with_next),
        grid=(rows // tm, d_ff // tf),
        in_specs=[
            row_blk,
            pl.BlockSpec((None, d, tf), lambda i, f: (layer, 0, f)),
            pl.BlockSpec((None, d, tf), lambda i, f: (layer, 0, f)),
            pl.BlockSpec((None, tf, d), lambda i, f: (layer, f, 0)),
            row_blk, vec_blk, vec_blk,
        ],
        out_specs=out_specs,
        out_shape=out_shape,
        scratch_shapes=[pltpu.VMEM((tm, d), F32)],
        compiler_params=_params("parallel", "arbitrary"),
        name="swiglu_ffn",
    )(xn, w_gate, w_up, w_down, h, post_row, next_row)
    return (res[0], res[1]) if with_next else (res[0], None)


def kernel(x, meta_tokens, norm_mix_pre, norm_mix_post, norm_ffn_pre, norm_ffn_post, w_in, hg_lb_logits,
           hg_norm_w, w_br_ret, w_br_hg, w_out, w_ffn_gate, w_ffn_up, w_ffn_down):
    batch, seq, d = x.shape
    depth = w_in.shape[0]
    assert seq % CHUNK == 0 and meta_tokens.shape == (N_META, d)
    assert w_in.shape[-1] == sum(_IN_WIDTHS) and d == V_W

    lb_sm = jax.nn.softmax(hg_lb_logits.astype(F32), axis=0)
    lbs = jnp.cumsum(lb_sm, axis=0) - lb_sm[0:1]

    wb = [w.astype(BF16) for w in (w_br_ret, w_br_hg, w_out, w_ffn_gate, w_ffn_up, w_ffn_down)]
    w_ret_b, w_hg_b, w_out_b, w_gate_b, w_up_b, w_down_b = wb

    hg_tables = _hg_tables()
    ret_tables = _ret_tables()
    meta_h = jnp.concatenate([jnp.zeros((PAD, d), F32), meta_tokens.astype(F32)], axis=0)
    row_sets = [
        dict(h=meta_h, batch=1, seq=CHUNK, pos0=-PAD),
        dict(h=x.reshape(batch * seq, d).astype(F32), batch=batch, seq=seq, pos0=N_META),
    ]
    for rs in row_sets:
        rs["rot"] = _rot_tables(rs["seq"], rs["pos0"])
        rs["xn"] = _norm_call(rs["h"], norm_mix_pre[0][None])

    for l in range(depth):
        last = l == depth - 1
        ret_state = jnp.zeros((HEADS, DK, DV), F32)
        hg_state = jnp.zeros((HEADS, DK, DV), F32)
        for rs in row_sets:
            is_meta = rs["pos0"] < 0
            proj = functools.partial(_proj_call, rs["xn"], w_in, l, seq=rs["seq"], pos0=rs["pos0"])
            qk = proj(ROT_TILES, "rot", extra=rs["rot"])
            vi = proj(IDENT_TILES, "ident")
            act = proj(SILU_TILES, "silu")
            kin, lf_hi, lf_lo = proj(FORGET_TILES, "forget", extra=(lbs[l][None],))
            yr, ret_fin = _ret_call(qk, vi, act, ret_tables, ret_state, rs["batch"], rs["seq"], rs["pos0"])
            yh, hg_fin = _hg_call(act, kin, lf_hi, lf_lo, vi, hg_norm_w[l][None], hg_tables, hg_state,
                                  rs["batch"], rs["seq"])
            if is_meta:
                ret_state, hg_state = ret_fin[0], hg_fin[0]
                if last:
                    continue
            gates = proj(SIGMOID_TILES, "sigmoid")
            y = _merge_call(yr, yh, gates, w_ret_b, w_hg_b, l)
            h_mid, xn_ffn = _out_call(y, w_out_b, rs["h"], norm_mix_post[l][None], norm_ffn_pre[l][None], l)
            next_row = norm_mix_pre[min(l + 1, depth - 1)][None]
            rs["h"], rs["xn"] = _ffn_call(xn_ffn, w_gate_b, w_up_b, w_down_b, h_mid, norm_ffn_post[l][None],
                                          next_row, l, with_next=not last)
    return row_sets[1]["h"].reshape(batch, seq, d)
```

```python
import functools

import numpy as np
import jax
import jax.numpy as jnp
from jax import lax
from jax.experimental import pallas as pl
from jax.experimental.pallas import tpu as pltpu

N_META = 16
HEADS = 8
DK = 128
DV = 256
CHUNK = 128
PAD = CHUNK - N_META
RMS_EPS = 1e-6
ROPE_BASE = 10000.0
LEVELS = 7
SUBLANES = 8
SMALL_LEVELS = 3
QK_W = HEADS * DK
V_W = HEADS * DV

VMEM_LIMIT_BYTES = 56 * 1024 * 1024
ROW_TILE = 1024
IN_TILE = 1024
PROJ_SLAB = 512
PROJ_VMEM_BUDGET = 50 * 1024 * 1024
HEADS_PER_STEP = 4
MIXER_ROWS = 1024
MIXER_UNROLL = 2
HG_GROUP = 4

_IN_WIDTHS = (QK_W, QK_W, V_W, V_W, QK_W, QK_W, V_W, V_W, V_W, V_W)
_IN_STARTS = tuple(sum(_IN_WIDTHS[:i]) // IN_TILE for i in range(len(_IN_WIDTHS)))


def _tiles(*parts):
    return tuple(t for p in parts for t in range(_IN_STARTS[p], _IN_STARTS[p] + _IN_WIDTHS[p] // IN_TILE))


ROT_TILES = _tiles(0, 1)
IDENT_TILES = _tiles(2, 6)
SILU_TILES = _tiles(3, 4, 7)
FORGET_TILES = _tiles(5)
SIGMOID_TILES = _tiles(8, 9)

F32 = jnp.float32
BF16 = jnp.bfloat16
NT_DIMS = (((1,), (1,)), ((), ()))
TN_DIMS = (((0,), (0,)), ((), ()))


def _params(*semantics, flags=None):
    return pltpu.CompilerParams(dimension_semantics=semantics, vmem_limit_bytes=VMEM_LIMIT_BYTES, flags=flags)


MIXER_FLAGS = None


def _sigmoid(x):
    return 0.5 + 0.5 * jnp.tanh(0.5 * x)


def _silu(x):
    h = 0.5 * x
    return h + h * jnp.tanh(h)


def _rms_scale(x):
    return x * lax.rsqrt(jnp.mean(x * x, axis=-1, keepdims=True) + RMS_EPS)


def _tile_lookup(tiles):
    def lookup(j):
        out = tiles[-1]
        for idx in range(len(tiles) - 2, -1, -1):
            out = jnp.where(j == idx, tiles[idx], out)
        return out
    return lookup


def _norm_kernel(h_ref, w_ref, o_ref):
    o_ref[...] = (_rms_scale(h_ref[...]) * w_ref[...]).astype(o_ref.dtype)


def _norm_call(h, w_row):
    rows, d = h.shape
    tm = min(rows, 512)
    return pl.pallas_call(
        _norm_kernel,
        grid=(rows // tm,),
        in_specs=[pl.BlockSpec((tm, d), lambda i: (i, 0)), pl.BlockSpec((1, d), lambda i: (0, 0))],
        out_specs=pl.BlockSpec((tm, d), lambda i: (i, 0)),
        out_shape=jax.ShapeDtypeStruct((rows, d), BF16),
        compiler_params=_params("parallel"),
        name="rms_norm",
    )(h, w_row)


def _proj_epilogue(acc, mode, extras, outs, r, first_pos):
    if mode == "ident":
        outs[0][r, :] = acc.astype(BF16)
    elif mode == "silu":
        outs[0][r, :] = _silu(acc).astype(BF16)
    elif mode == "sigmoid":
        outs[0][r, :] = _sigmoid(acc).astype(BF16)
    elif mode == "rot":
        cos, sin = extras[0][r, :], extras[1][r, :]
        for g in range(IN_TILE // DK):
            xg = acc[:, g * DK:(g + 1) * DK]
            outs[0][r, g * DK:(g + 1) * DK] = (xg * cos + pltpu.roll(xg, DK // 2, 1) * sin).astype(BF16)
    elif mode == "forget":
        kin_ref, hi_ref, lo_ref = outs
        one_minus_f = (1.0 - extras[0][...]) * (0.5 - 0.5 * jnp.tanh(0.5 * acc))
        log_f = jnp.log2(1.0 - one_minus_f)
        if first_pos is not None:
            valid = (lax.broadcasted_iota(jnp.int32, (acc.shape[0], 1), 0) + first_pos) >= 0
            log_f = jnp.where(valid, log_f, 0.0)
            one_minus_f = jnp.where(valid, one_minus_f, 0.0)
        kin_ref[r, :] = one_minus_f.astype(BF16)
        hi = log_f.astype(BF16)
        hi_ref[r, :] = hi
        lo_ref[r, :] = (log_f - hi.astype(F32)).astype(BF16)


def _proj_kernel(*refs, mode, with_meta):
    n_extra = {"rot": 2, "forget": 1}.get(mode, 0)
    n_out = 3 if mode == "forget" else 1
    x_ref, w_ref = refs[:2]
    extras = refs[2:2 + n_extra]
    pos = 2 + n_extra
    if with_meta:
        xm_ref = refs[pos]
        n_meta_extra = 2 if mode == "rot" else 0
        meta_extras = refs[pos + 1:pos + 1 + n_meta_extra] if n_meta_extra else extras
        pos += 1 + n_meta_extra
    outs = refs[pos:pos + n_out]
    meta_outs = refs[pos + n_out:pos + 2 * n_out]
    wb_ref = refs[-1]

    @pl.when(pl.program_id(1) == 0)
    def _():
        wb_ref[...] = w_ref[...].astype(BF16)
        if with_meta:
            acc = jnp.dot(xm_ref[...], wb_ref[...], preferred_element_type=F32)
            _proj_epilogue(acc, mode, meta_extras, meta_outs, slice(None), -PAD)

    tm = x_ref.shape[0]
    slab = min(tm, PROJ_SLAB)
    for s0 in range(0, tm, slab):
        r = slice(s0, s0 + slab)
        acc = jnp.dot(x_ref[r, :], wb_ref[...], preferred_element_type=F32)
        _proj_epilogue(acc, mode, extras, outs, r, None)


def _proj_row_tile(rows, k, n_out):
    for tm in (2 * ROW_TILE, ROW_TILE):
        x_bytes = 2 * tm * k * 2
        w_bytes = 2 * k * IN_TILE * 4 + k * IN_TILE * 2
        out_bytes = n_out * 2 * tm * IN_TILE * 2
        acc_bytes = 2 * PROJ_SLAB * IN_TILE * 4
        if tm <= rows and rows % tm == 0 and x_bytes + w_bytes + out_bytes + acc_bytes <= PROJ_VMEM_BUDGET:
            return tm
    return min(rows, ROW_TILE)


def _proj_call(xn, xn_meta, w_in, layer, tiles, mode, seq, extra=(), meta_extra=()):
    rows, k = xn.shape
    n_out = 3 if mode == "forget" else 1
    with_meta = xn_meta is not None
    tm = _proj_row_tile(rows, k, n_out)
    lookup = _tile_lookup(tiles)
    out_cols = len(tiles) * IN_TILE
    in_specs = [pl.BlockSpec((tm, k), lambda j, i: (i, 0)),
                pl.BlockSpec((None, k, IN_TILE), lambda j, i: (layer, 0, lookup(j)))]
    if mode == "rot":
        blocks_per_seq = seq // tm
        tab = pl.BlockSpec((None, tm, DK), lambda j, i: (j, i % blocks_per_seq, 0))
        in_specs += [tab, tab]
    elif mode == "forget":
        in_specs += [pl.BlockSpec((1, IN_TILE), lambda j, i: (0, 0))]
    operands = [xn, w_in, *extra]
    out_specs = [pl.BlockSpec((tm, IN_TILE), lambda j, i: (i, j))] * n_out
    out_shape = [jax.ShapeDtypeStruct((rows, out_cols), BF16)] * n_out
    if with_meta:
        in_specs += [pl.BlockSpec((CHUNK, k), lambda j, i: (0, 0))]
        operands += [xn_meta]
        if mode == "rot":
            in_specs += [pl.BlockSpec((None, CHUNK, DK), lambda j, i: (j, 0, 0))] * 2
            operands += list(meta_extra)
        out_specs += [pl.BlockSpec((CHUNK, IN_TILE), lambda j, i: (0, j))] * n_out
        out_shape += [jax.ShapeDtypeStruct((CHUNK, out_cols), BF16)] * n_out
    res = pl.pallas_call(
        functools.partial(_proj_kernel, mode=mode, with_meta=with_meta),
        grid=(len(tiles), rows // tm),
        in_specs=in_specs,
        out_specs=out_specs,
        out_shape=out_shape,
        scratch_shapes=[pltpu.VMEM((k, IN_TILE), BF16)],
        compiler_params=_params("parallel", "arbitrary"),
        name="in_proj_" + mode,
    )(*operands)
    main, meta = res[:n_out], res[n_out:]
    if n_out == 1:
        return main[0], (meta[0] if with_meta else None)
    return main, (meta if with_meta else None)


def _rot_tables(seq, pos0):
    half = DK // 2
    inv = ROPE_BASE ** (-jnp.arange(half, dtype=F32) / half)
    pos = jnp.arange(seq, dtype=jnp.int32) + pos0
    ang = pos.astype(F32)[:, None] * inv[None, :]
    cos, sin = jnp.cos(ang), jnp.sin(ang)
    cos2 = jnp.concatenate([cos, cos], axis=1)
    sin2 = jnp.concatenate([-sin, sin], axis=1)
    scale = DK ** -0.5
    return jnp.stack([cos2, cos2 * scale]), jnp.stack([sin2, sin2 * scale])


def _ret_kernel(q_ref, k_ref, v_ref, g_ref, dmat_ref, qdec_ref, kdec_ref, cdec_ref, s0_ref,
                o_ref, sfin_ref, s_ref, *, hb, nchunks, pos0):
    @pl.when(pl.program_id(2) == 0)
    def _():
        s_ref[...] = s0_ref[...]

    def body(c, carry):
        r0 = pl.multiple_of(c * CHUNK, CHUNK)
        rows = pl.ds(r0, CHUNK)
        if pos0 < 0:
            valid = (lax.broadcasted_iota(jnp.int32, (CHUNK, 1), 0) + (r0 + pos0)) >= 0
        heads = range(hb)
        q = [q_ref[rows, j * DK:(j + 1) * DK] for j in heads]
        k = [k_ref[rows, j * DK:(j + 1) * DK] for j in heads]
        if pos0 < 0:
            k = [jnp.where(valid, kj, jnp.zeros_like(kj)) for kj in k]
        v = [v_ref[rows, j * DV:(j + 1) * DV] for j in heads]
        scores = [jnp.dot(q[j], k[j].astype(F32).T.astype(BF16), preferred_element_type=F32) for j in heads]
        s = [s_ref[j] for j in heads]
        new_s = [lax.dot_general(k[j] * kdec_ref[j], v[j], TN_DIMS, preferred_element_type=F32) for j in heads]
        y = [jnp.dot(jnp.concatenate([(scores[j] * dmat_ref[j]).astype(BF16), q[j] * qdec_ref[j]], axis=1),
                     jnp.concatenate([v[j], s[j].astype(BF16)], axis=0), preferred_element_type=F32)
             for j in heads]
        for j in heads:
            s_ref[j] = s[j] * cdec_ref[j] + new_s[j]
            g = g_ref[rows, j * DV:(j + 1) * DV].astype(F32)
            o_ref[rows, j * DV:(j + 1) * DV] = (_rms_scale(y[j]) * g).astype(o_ref.dtype)
        return carry

    lax.fori_loop(0, nchunks, body, 0, unroll=MIXER_UNROLL)
    sfin_ref[...] = s_ref[...]


def _ret_tables():
    log_g = jnp.log1p(-jnp.exp2(-5.0 - jnp.arange(HEADS, dtype=F32)))
    idx = jnp.arange(CHUNK, dtype=F32)
    diff = idx[:, None] - idx[None, :]
    dmat = jnp.where(diff[None] >= 0, jnp.exp(jnp.maximum(diff, 0.0)[None] * log_g[:, None, None]), 0.0)
    qdec = jnp.exp((idx + 1)[None, :] * log_g[:, None])
    kdec = jnp.exp((CHUNK - 1 - idx)[None, :] * log_g[:, None])
    cdec = jnp.exp(CHUNK * log_g)
    qdec = jnp.broadcast_to(qdec[:, :, None], (HEADS, CHUNK, DK)).astype(BF16)
    kdec = jnp.broadcast_to(kdec[:, :, None], (HEADS, CHUNK, DK)).astype(BF16)
    cdec = jnp.broadcast_to(cdec[:, None, None], (HEADS, 1, DV))
    return dmat, qdec, kdec, cdec


def _mixer_grid(batch, seq):
    rb = min(seq, MIXER_ROWS)
    return rb, seq // rb


def _ret_call(qk, vi, act, tables, s0, batch, seq, pos0):
    dmat, qdec, kdec, cdec = tables
    hb = HEADS_PER_STEP
    rb, nblk = _mixer_grid(batch, seq)
    k_blk = QK_W // (hb * DK)
    kern = functools.partial(_ret_kernel, hb=hb, nchunks=rb // CHUNK, pos0=pos0)
    per_head = lambda b, h, r: (h, 0, 0)
    return pl.pallas_call(
        kern,
        grid=(batch, HEADS // hb, nblk),
        in_specs=[
            pl.BlockSpec((rb, hb * DK), lambda b, h, r: (b * nblk + r, h)),
            pl.BlockSpec((rb, hb * DK), lambda b, h, r: (b * nblk + r, k_blk + h)),
            pl.BlockSpec((rb, hb * DV), lambda b, h, r: (b * nblk + r, h)),
            pl.BlockSpec((rb, hb * DV), lambda b, h, r: (b * nblk + r, h)),
            pl.BlockSpec((hb, CHUNK, CHUNK), per_head),
            pl.BlockSpec((hb, CHUNK, DK), per_head),
            pl.BlockSpec((hb, CHUNK, DK), per_head),
            pl.BlockSpec((hb, 1, DV), per_head),
            pl.BlockSpec((hb, DK, DV), per_head),
        ],
        out_specs=[
            pl.BlockSpec((rb, hb * DV), lambda b, h, r: (b * nblk + r, h)),
            pl.BlockSpec((None, hb, DK, DV), lambda b, h, r: (b, h, 0, 0)),
        ],
        out_shape=[jax.ShapeDtypeStruct((batch * seq, V_W), BF16),
                   jax.ShapeDtypeStruct((batch, HEADS, DK, DV), F32)],
        scratch_shapes=[pltpu.VMEM((hb, DK, DV), F32)],
        compiler_params=_params("parallel", "parallel", "arbitrary", flags=MIXER_FLAGS),
        name="retention",
    )(qk, qk, vi, act, dmat, qdec, kdec, cdec, s0)


def _hg_tables():
    t = np.arange(CHUNK)[:, None]
    u = np.arange(CHUNK)[None, :]
    mats, masks = [], []
    for j in range(LEVELS):
        m = 1 << j
        upper = ((t >> j) & 1) == 1
        q_part = upper & (u >= (t & ~(m - 1))) & (u <= t)
        k_part = (~upper) & (u > t) & (u <= (t | (m - 1)))
        if m < SUBLANES:
            mats.append(q_part | k_part)
        masks.append(((t >> (j + 1)) == (u >> (j + 1))) & upper & (((u >> j) & 1) == 0))
    mats.append(u <= t)
    masks.append(t == u)
    mstack = np.concatenate(mats, axis=0).astype(np.float32)
    mstack = np.concatenate([mstack, mstack], axis=1)
    return jnp.asarray(mstack, dtype=BF16), jnp.asarray(np.stack(masks).astype(np.float32))


def _hg_kernel(q_ref, kin_ref, hi_ref, lo_ref, v_ref, g_ref, nw_ref, mstack_ref, masks_ref, s0_ref,
               o_ref, sfin_ref, st_ref, e_ref, *, hb, nchunks):
    @pl.when(pl.program_id(2) == 0)
    def _():
        st_ref[...] = s0_ref[...]

    row = lax.broadcasted_iota(jnp.int32, (CHUNK, 1), 0)

    def exponents(c, slot, pair):
        rows = pl.ds(pl.multiple_of(c * CHUNK, CHUNK), CHUNK)
        lanes = slice(pair * 2 * DK, (pair + 1) * 2 * DK)
        pieces = jnp.concatenate([hi_ref[rows, lanes], lo_ref[rows, lanes]], axis=0)
        e_ref[slot, :, lanes] = jnp.dot(mstack_ref[...], pieces, preferred_element_type=F32)

    def process(c, slot, nxt):
        for g0 in range(0, hb, HG_GROUP):
            process_group(c, slot, nxt, range(g0, g0 + HG_GROUP))

    def process_group(c, slot, nxt, heads):
        rows = pl.ds(pl.multiple_of(c * CHUNK, CHUNK), CHUNK)
        expo = {j: e_ref.at[slot, :, j * DK:(j + 1) * DK] for j in heads}
        qb = {j: q_ref[rows, j * DK:(j + 1) * DK] for j in heads}
        q = {j: qb[j].astype(F32) for j in heads}
        kin = {j: kin_ref[rows, j * DK:(j + 1) * DK].astype(F32) for j in heads}
        b = {j: expo[j][SMALL_LEVELS * CHUNK:, :] for j in heads}
        a = {j: jnp.dot(qb[j], kin[j].T.astype(BF16), preferred_element_type=F32) * masks_ref[LEVELS]
             for j in heads}
        a_rows = None
        issue_at = {1 + 3 * i: p for i, p in enumerate(range(heads[0] // 2, heads[-1] // 2 + 1))}
        for lev in range(LEVELS):
            m = 1 << lev
            if nxt is not None and lev in issue_at:
                exponents(nxt, 1 - slot, issue_at[lev])
            if m < SUBLANES:
                for j in heads:
                    e = jnp.exp2(expo[j][lev * CHUNK:(lev + 1) * CHUNK, :])
                    x = jnp.where(((row >> lev) & 1) == 1, q[j], kin[j]) * e
                    a[j] = a[j] + (jnp.dot(x.astype(BF16), x.T.astype(BF16), preferred_element_type=F32)
                                   * masks_ref[lev])
            else:
                if a_rows is None:
                    a_rows = {j: [a[j][s:s + SUBLANES] for s in range(0, CHUNK, SUBLANES)] for j in heads}
                upper = [s for s0 in range(m, CHUNK, 2 * m) for s in range(s0, s0 + m, SUBLANES)]
                for j in heads:
                    parts = []
                    for s0 in range(0, CHUNK, 2 * m):
                        b_mid = jnp.broadcast_to(b[j][s0 + m - 1:s0 + m, :], (m, DK))
                        parts += [b_mid - b[j][s0:s0 + m], b[j][s0 + m:s0 + 2 * m] - b_mid]
                    e = jnp.exp2(jnp.concatenate(parts, axis=0))
                    ke_t = (kin[j] * e).T.astype(BF16)
                    qe = jnp.concatenate([q[j][s:s + SUBLANES] * e[s:s + SUBLANES] for s in upper], axis=0)
                    p = jnp.dot(qe.astype(BF16), ke_t, preferred_element_type=F32)
                    for idx, s in enumerate(upper):
                        a_rows[j][s // SUBLANES] = (a_rows[j][s // SUBLANES]
                                                    + p[idx * SUBLANES:(idx + 1) * SUBLANES]
                                                    * masks_ref[lev, s:s + SUBLANES, :])
        total = {j: b[j][CHUNK - 1:CHUNK, :] for j in heads}
        v = {j: v_ref[rows, j * DV:(j + 1) * DV] for j in heads}
        st = {j: st_ref[j] for j in heads}
        new_st = {j: lax.dot_general((kin[j] * jnp.exp2(total[j] - b[j])).astype(BF16), v[j], TN_DIMS,
                                     preferred_element_type=F32) for j in heads}
        y = {j: jnp.dot(jnp.concatenate([jnp.concatenate(a_rows[j], axis=0).astype(BF16),
                                         (q[j] * jnp.exp2(b[j])).astype(BF16)], axis=1),
                        jnp.concatenate([v[j], st[j].astype(BF16)], axis=0), preferred_element_type=F32)
             for j in heads}
        for j in heads:
            decay = jnp.broadcast_to(jnp.exp2(total[j]), (CHUNK, DK)).T
            st_ref[j] = st[j] * jnp.concatenate([decay] * (DV // DK), axis=1) + new_st[j]
            g = g_ref[rows, j * DV:(j + 1) * DV].astype(F32)
            o_ref[rows, j * DV:(j + 1) * DV] = (_rms_scale(y[j]) * nw_ref[...] * g).astype(o_ref.dtype)

    for pr in range(hb // 2):
        exponents(0, 0, pr)
    if nchunks == 1:
        process(0, 0, None)
    else:
        assert nchunks % 2 == 0

        def pair(p, carry):
            c = 2 * p
            process(c, 0, c + 1)
            process(c + 1, 1, jnp.minimum(c + 2, nchunks - 1))
            return carry

        lax.fori_loop(0, nchunks // 2, pair, 0)
    sfin_ref[...] = st_ref[...]


def _hg_call(act, kin, lf_hi, lf_lo, vi, nw_row, tables, s0, batch, seq):
    mstack, masks = tables
    hb = HEADS_PER_STEP
    rb, nblk = _mixer_grid(batch, seq)
    q_blk = V_W // (hb * DK)
    g_blk = (V_W + QK_W) // (hb * DV)
    i_blk = V_W // (hb * DV)
    kern = functools.partial(_hg_kernel, hb=hb, nchunks=rb // CHUNK)
    row_blk = lambda off: (lambda b, h, r: (b * nblk + r, off + h))
    return pl.pallas_call(
        kern,
        grid=(batch, HEADS // hb, nblk),
        in_specs=[
            pl.BlockSpec((rb, hb * DK), row_blk(q_blk)),
            pl.BlockSpec((rb, hb * DK), row_blk(0)),
            pl.BlockSpec((rb, hb * DK), row_blk(0)),
            pl.BlockSpec((rb, hb * DK), row_blk(0)),
            pl.BlockSpec((rb, hb * DV), row_blk(i_blk)),
            pl.BlockSpec((rb, hb * DV), row_blk(g_blk)),
            pl.BlockSpec((1, DV), lambda b, h, r: (0, 0)),
            pl.BlockSpec(mstack.shape, lambda b, h, r: (0, 0)),
            pl.BlockSpec(masks.shape, lambda b, h, r: (0, 0, 0)),
            pl.BlockSpec((hb, DK, DV), lambda b, h, r: (h, 0, 0)),
        ],
        out_specs=[
            pl.BlockSpec((rb, hb * DV), row_blk(0)),
            pl.BlockSpec((None, hb, DK, DV), lambda b, h, r: (b, h, 0, 0)),
        ],
        out_shape=[jax.ShapeDtypeStruct((batch * seq, V_W), BF16),
                   jax.ShapeDtypeStruct((batch, HEADS, DK, DV), F32)],
        scratch_shapes=[pltpu.VMEM((hb, DK, DV), F32),
                        pltpu.VMEM((2, (SMALL_LEVELS + 1) * CHUNK, hb * DK), F32)],
        compiler_params=_params("parallel", "parallel", "arbitrary", flags=MIXER_FLAGS),
        name="hgrn2",
    )(act, kin, lf_hi, lf_lo, vi, act, nw_row, mstack, masks, s0)


def _merge_kernel(yr_ref, yh_ref, wr_ref, wh_ref, gr_ref, gh_ref, o_ref):
    a = jnp.dot(yr_ref[...], wr_ref[...], preferred_element_type=F32)
    b = jnp.dot(yh_ref[...], wh_ref[...], preferred_element_type=F32)
    o_ref[...] = (gr_ref[...].astype(F32) * a + gh_ref[...].astype(F32) * b).astype(o_ref.dtype)


def _merge_call(yr, yh, gates, w_ret, w_hg, layer):
    rows, k = yr.shape
    d = w_ret.shape[-1]
    tm = min(rows, ROW_TILE)
    tn = 512
    gh_blk = d // tn
    return pl.pallas_call(
        _merge_kernel,
        grid=(rows // tm, d // tn),
        in_specs=[
            pl.BlockSpec((tm, k), lambda i, j: (i, 0)),
            pl.BlockSpec((tm, k), lambda i, j: (i, 0)),
            pl.BlockSpec((None, k, tn), lambda i, j: (layer, 0, j)),
            pl.BlockSpec((None, k, tn), lambda i, j: (layer, 0, j)),
            pl.BlockSpec((tm, tn), lambda i, j: (i, j)),
            pl.BlockSpec((tm, tn), lambda i, j: (i, gh_blk + j)),
        ],
        out_specs=pl.BlockSpec((tm, tn), lambda i, j: (i, j)),
        out_shape=jax.ShapeDtypeStruct((rows, d), BF16),
        compiler_params=_params("parallel", "arbitrary"),
        name="branch_merge",
    )(yr, yh, w_ret, w_hg, gates, gates)


def _out_kernel(y_ref, w_ref, h_ref, post_ref, nxt_ref, hn_ref, xn_ref):
    m = jnp.dot(y_ref[...], w_ref[...], preferred_element_type=F32)
    hn = h_ref[...] + _rms_scale(m) * post_ref[...]
    hn_ref[...] = hn
    xn_ref[...] = (_rms_scale(hn) * nxt_ref[...]).astype(xn_ref.dtype)


def _out_call(y, w_out, h, post_row, next_row, layer):
    rows, d = h.shape
    tm = min(rows, 512)
    row_blk = pl.BlockSpec((tm, d), lambda i: (i, 0))
    vec_blk = pl.BlockSpec((1, d), lambda i: (0, 0))
    return pl.pallas_call(
        _out_kernel,
        grid=(rows // tm,),
        in_specs=[row_blk, pl.BlockSpec((None, d, d), lambda i: (layer, 0, 0)), row_blk, vec_blk, vec_blk],
        out_specs=[row_blk, row_blk],
        out_shape=[jax.ShapeDtypeStruct((rows, d), F32), jax.ShapeDtypeStruct((rows, d), BF16)],
        compiler_params=_params("parallel"),
        name="out_proj",
    )(y, w_out, h, post_row, next_row)


def _ffn_kernel(x_ref, wg_ref, wu_ref, wd_ref, h_ref, post_ref, nxt_ref, hn_ref, *rest, with_next):
    if with_next:
        xn_ref, acc_ref = rest
    else:
        (acc_ref,) = rest
    f = pl.program_id(1)

    @pl.when(f == 0)
    def _():
        acc_ref[...] = jnp.zeros_like(acc_ref)

    x = x_ref[...]
    g = jnp.dot(x, wg_ref[...], preferred_element_type=F32)
    u = jnp.dot(x, wu_ref[...], preferred_element_type=F32)
    act = (_silu(g) * u).astype(BF16)
    acc_ref[...] += jnp.dot(act, wd_ref[...], preferred_element_type=F32)

    @pl.when(f == pl.num_programs(1) - 1)
    def _():
        hn = h_ref[...] + _rms_scale(acc_ref[...]) * post_ref[...]
        hn_ref[...] = hn
        if with_next:
            xn_ref[...] = (_rms_scale(hn) * nxt_ref[...]).astype(xn_ref.dtype)


def _ffn_call(xn, w_gate, w_up, w_down, h, post_row, next_row, layer, with_next):
    rows, d = h.shape
    d_ff = w_gate.shape[-1]
    tm = min(rows, 512)
    tf = 512
    row_blk = pl.BlockSpec((tm, d), lambda i, f: (i, 0))
    vec_blk = pl.BlockSpec((1, d), lambda i, f: (0, 0))
    out_specs = [row_blk]
    out_shape = [jax.ShapeDtypeStruct((rows, d), F32)]
    if with_next:
        out_specs.append(row_blk)
        out_shape.append(jax.ShapeDtypeStruct((rows, d), BF16))
    res = pl.pallas_call(
        functools.partial(_ffn_kernel, with_next=with_next),
        grid=(rows // tm, d_ff // tf),
        in_specs=[
            row_blk,
            pl.BlockSpec((None, d, tf), lambda i, f: (layer, 0, f)),
            pl.BlockSpec((None, d, tf), lambda i, f: (layer, 0, f)),
            pl.BlockSpec((None, tf, d), lambda i, f: (layer, f, 0)),
            row_blk, vec_blk, vec_blk,
        ],
        out_specs=out_specs,
        out_shape=out_shape,
        scratch_shapes=[pltpu.VMEM((tm, d), F32)],
        compiler_params=_params("parallel", "arbitrary"),
        name="swiglu_ffn",
    )(xn, w_gate, w_up, w_down, h, post_row, next_row)
    return (res[0], res[1]) if with_next else (res[0], None)


def kernel(x, meta_tokens, norm_mix_pre, norm_mix_post, norm_ffn_pre, norm_ffn_post, w_in, hg_lb_logits,
           hg_norm_w, w_br_ret, w_br_hg, w_out, w_ffn_gate, w_ffn_up, w_ffn_down):
    batch, seq, d = x.shape
    depth = w_in.shape[0]
    assert seq % CHUNK == 0 and meta_tokens.shape == (N_META, d)
    assert w_in.shape[-1] == sum(_IN_WIDTHS) and d == V_W

    lb_sm = jax.nn.softmax(hg_lb_logits.astype(F32), axis=0)
    lbs = jnp.cumsum(lb_sm, axis=0) - lb_sm[0:1]

    wb = [w.astype(BF16) for w in (w_br_ret, w_br_hg, w_out, w_ffn_gate, w_ffn_up, w_ffn_down)]
    w_ret_b, w_hg_b, w_out_b, w_gate_b, w_up_b, w_down_b = wb

    hg_tables = _hg_tables()
    ret_tables = _ret_tables()
    meta_h = jnp.concatenate([jnp.zeros((PAD, d), F32), meta_tokens.astype(F32)], axis=0)
    meta = dict(h=meta_h, batch=1, seq=CHUNK, pos0=-PAD)
    main = dict(h=x.reshape(batch * seq, d).astype(F32), batch=batch, seq=seq, pos0=N_META)
    for rs in (meta, main):
        rs["rot"] = _rot_tables(rs["seq"], rs["pos0"])
        rs["xn"] = _norm_call(rs["h"], norm_mix_pre[0][None])

    for l in range(depth):
        last = l == depth - 1
        proj = functools.partial(_proj_call, main["xn"], meta["xn"], w_in, l, seq=seq)
        main["qk"], meta["qk"] = proj(ROT_TILES, "rot", extra=main["rot"], meta_extra=meta["rot"])
        main["vi"], meta["vi"] = proj(IDENT_TILES, "ident")
        main["act"], meta["act"] = proj(SILU_TILES, "silu")
        main["forget"], meta["forget"] = proj(FORGET_TILES, "forget", extra=(lbs[l][None],))
        if last:
            main["gates"], _ = _proj_call(main["xn"], None, w_in, l, SIGMOID_TILES, "sigmoid", seq)
        else:
            main["gates"], meta["gates"] = proj(SIGMOID_TILES, "sigmoid")

        ret_state = jnp.zeros((HEADS, DK, DV), F32)
        hg_state = jnp.zeros((HEADS, DK, DV), F32)
        for rs in (meta, main):
            is_meta = rs["pos0"] < 0
            kin, lf_hi, lf_lo = rs["forget"]
            yr, ret_fin = _ret_call(rs["qk"], rs["vi"], rs["act"], ret_tables, ret_state,
                                    rs["batch"], rs["seq"], rs["pos0"])
            yh, hg_fin = _hg_call(rs["act"], kin, lf_hi, lf_lo, rs["vi"], hg_norm_w[l][None], hg_tables, hg_state,
                                  rs["batch"], rs["seq"])
            if is_meta:
                ret_state, hg_state = ret_fin[0], hg_fin[0]
                if last:
                    continue
            y = _merge_call(yr, yh, rs["gates"], w_ret_b, w_hg_b, l)
            h_mid, xn_ffn = _out_call(y, w_out_b, rs["h"], norm_mix_post[l][None], norm_ffn_pre[l][None], l)
            next_row = norm_mix_pre[min(l + 1, depth - 1)][None]
            rs["h"], rs["xn"] = _ffn_call(xn_ffn, w_gate_b, w_up_b, w_down_b, h_mid, norm_ffn_post[l][None],
                                          next_row, l, with_next=not last)
    return main["h"].reshape(batch, seq, d)
```

```python
import functools

import numpy as np
import jax
import jax.numpy as jnp
from jax import lax
from jax.experimental import pallas as pl
from jax.experimental.pallas import tpu as pltpu

N_META = 16
HEADS = 8
DK = 128
DV = 256
CHUNK = 128
PAD = CHUNK - N_META
RMS_EPS = 1e-6
ROPE_BASE = 10000.0
LEVELS = 7
SUBLANES = 8
SMALL_LEVELS = 3
QK_W = HEADS * DK
V_W = HEADS * DV

VMEM_LIMIT_BYTES = 56 * 1024 * 1024
ROW_TILE = 1024
IN_TILE = 1024
PROJ_SLAB = 512
PROJ_VMEM_BUDGET = 50 * 1024 * 1024
HEADS_PER_STEP = 4
MIXER_ROWS = 1024
MIXER_UNROLL = 2
FFN_ROW_TILE = 512
FFN_COL_TILE = 512
FFN_CAST_COL_TILE = 256
HG_GROUP = 4

_IN_WIDTHS = (QK_W, QK_W, V_W, V_W, QK_W, QK_W, V_W, V_W, V_W, V_W)
_IN_STARTS = tuple(sum(_IN_WIDTHS[:i]) // IN_TILE for i in range(len(_IN_WIDTHS)))


def _tiles(*parts):
    return tuple(t for p in parts for t in range(_IN_STARTS[p], _IN_STARTS[p] + _IN_WIDTHS[p] // IN_TILE))


ROT_TILES = _tiles(0, 1)
IDENT_TILES = _tiles(2, 6)
SILU_TILES = _tiles(3, 4, 7)
FORGET_TILES = _tiles(5)
SIGMOID_TILES = _tiles(8, 9)

F32 = jnp.float32
BF16 = jnp.bfloat16
NT_DIMS = (((1,), (1,)), ((), ()))
TN_DIMS = (((0,), (0,)), ((), ()))


def _params(*semantics, flags=None):
    return pltpu.CompilerParams(dimension_semantics=semantics, vmem_limit_bytes=VMEM_LIMIT_BYTES, flags=flags)


MIXER_FLAGS = None


def _sigmoid(x):
    return 0.5 + 0.5 * jnp.tanh(0.5 * x)


def _silu(x):
    h = 0.5 * x
    return h + h * jnp.tanh(h)


def _rms_scale(x):
    return x * lax.rsqrt(jnp.mean(x * x, axis=-1, keepdims=True) + RMS_EPS)


def _tile_lookup(tiles):
    def lookup(j):
        out = tiles[-1]
        for idx in range(len(tiles) - 2, -1, -1):
            out = jnp.where(j == idx, tiles[idx], out)
        return out
    return lookup


def _norm_kernel(h_ref, w_ref, o_ref):
    o_ref[...] = (_rms_scale(h_ref[...]) * w_ref[...]).astype(o_ref.dtype)


def _norm_call(h, w_row):
    rows, d = h.shape
    tm = min(rows, 512)
    return pl.pallas_call(
        _norm_kernel,
        grid=(rows // tm,),
        in_specs=[pl.BlockSpec((tm, d), lambda i: (i, 0)), pl.BlockSpec((1, d), lambda i: (0, 0))],
        out_specs=pl.BlockSpec((tm, d), lambda i: (i, 0)),
        out_shape=jax.ShapeDtypeStruct((rows, d), BF16),
        compiler_params=_params("parallel"),
        name="rms_norm",
    )(h, w_row)


def _proj_epilogue(acc, mode, extras, outs, r, first_pos):
    if mode == "ident":
        outs[0][r, :] = acc.astype(BF16)
    elif mode == "silu":
        outs[0][r, :] = _silu(acc).astype(BF16)
    elif mode == "sigmoid":
        outs[0][r, :] = _sigmoid(acc).astype(BF16)
    elif mode == "rot":
        cos, sin = extras[0][r, :], extras[1][r, :]
        for g in range(IN_TILE // DK):
            xg = acc[:, g * DK:(g + 1) * DK]
            outs[0][r, g * DK:(g + 1) * DK] = (xg * cos + pltpu.roll(xg, DK // 2, 1) * sin).astype(BF16)
    elif mode == "forget":
        kin_ref, hi_ref, lo_ref = outs
        one_minus_f = (1.0 - extras[0][...]) * (0.5 - 0.5 * jnp.tanh(0.5 * acc))
        log_f = jnp.log2(1.0 - one_minus_f)
        if first_pos is not None:
            valid = (lax.broadcasted_iota(jnp.int32, (acc.shape[0], 1), 0) + first_pos) >= 0
            log_f = jnp.where(valid, log_f, 0.0)
            one_minus_f = jnp.where(valid, one_minus_f, 0.0)
        kin_ref[r, :] = one_minus_f.astype(BF16)
        hi = log_f.astype(BF16)
        hi_ref[r, :] = hi
        lo_ref[r, :] = (log_f - hi.astype(F32)).astype(BF16)


def _proj_kernel(*refs, mode, with_meta):
    n_extra = {"rot": 2, "forget": 1}.get(mode, 0)
    n_out = 3 if mode == "forget" else 1
    x_ref, w_ref = refs[:2]
    extras = refs[2:2 + n_extra]
    pos = 2 + n_extra
    if with_meta:
        xm_ref = refs[pos]
        n_meta_extra = 2 if mode == "rot" else 0
        meta_extras = refs[pos + 1:pos + 1 + n_meta_extra] if n_meta_extra else extras
        pos += 1 + n_meta_extra
    outs = refs[pos:pos + n_out]
    meta_outs = refs[pos + n_out:pos + 2 * n_out]
    wb_ref = refs[-1]

    @pl.when(pl.program_id(1) == 0)
    def _():
        wb_ref[...] = w_ref[...].astype(BF16)
        if with_meta:
            acc = jnp.dot(xm_ref[...], wb_ref[...], preferred_element_type=F32)
            _proj_epilogue(acc, mode, meta_extras, meta_outs, slice(None), -PAD)

    tm = x_ref.shape[0]
    slab = min(tm, PROJ_SLAB)
    pending = None
    for s0 in range(0, tm, slab):
        r = slice(s0, s0 + slab)
        acc = jnp.dot(x_ref[r, :], wb_ref[...], preferred_element_type=F32)
        if pending is not None:
            _proj_epilogue(*pending)
        pending = (acc, mode, extras, outs, r, None)
    _proj_epilogue(*pending)


def _proj_row_tile(rows, k, n_out):
    for tm in (2 * ROW_TILE, ROW_TILE):
        x_bytes = 2 * tm * k * 2
        w_bytes = 2 * k * IN_TILE * 4 + k * IN_TILE * 2
        out_bytes = n_out * 2 * tm * IN_TILE * 2
        acc_bytes = 2 * PROJ_SLAB * IN_TILE * 4
        if tm <= rows and rows % tm == 0 and x_bytes + w_bytes + out_bytes + acc_bytes <= PROJ_VMEM_BUDGET:
            return tm
    return min(rows, ROW_TILE)


def _proj_call(xn, xn_meta, w_in, layer, tiles, mode, seq, extra=(), meta_extra=()):
    rows, k = xn.shape
    n_out = 3 if mode == "forget" else 1
    with_meta = xn_meta is not None
    tm = _proj_row_tile(rows, k, n_out)
    lookup = _tile_lookup(tiles)
    out_cols = len(tiles) * IN_TILE
    in_specs = [pl.BlockSpec((tm, k), lambda j, i: (i, 0)),
                pl.BlockSpec((None, k, IN_TILE), lambda j, i: (layer, 0, lookup(j)))]
    if mode == "rot":
        blocks_per_seq = seq // tm
        tab = pl.BlockSpec((None, tm, DK), lambda j, i: (j, i % blocks_per_seq, 0))
        in_specs += [tab, tab]
    elif mode == "forget":
        in_specs += [pl.BlockSpec((1, IN_TILE), lambda j, i: (0, 0))]
    operands = [xn, w_in, *extra]
    out_specs = [pl.BlockSpec((tm, IN_TILE), lambda j, i: (i, j))] * n_out
    out_shape = [jax.ShapeDtypeStruct((rows, out_cols), BF16)] * n_out
    if with_meta:
        in_specs += [pl.BlockSpec((CHUNK, k), lambda j, i: (0, 0))]
        operands += [xn_meta]
        if mode == "rot":
            in_specs += [pl.BlockSpec((None, CHUNK, DK), lambda j, i: (j, 0, 0))] * 2
            operands += list(meta_extra)
        out_specs += [pl.BlockSpec((CHUNK, IN_TILE), lambda j, i: (0, j))] * n_out
        out_shape += [jax.ShapeDtypeStruct((CHUNK, out_cols), BF16)] * n_out
    res = pl.pallas_call(
        functools.partial(_proj_kernel, mode=mode, with_meta=with_meta),
        grid=(len(tiles), rows // tm),
        in_specs=in_specs,
        out_specs=out_specs,
        out_shape=out_shape,
        scratch_shapes=[pltpu.VMEM((k, IN_TILE), BF16)],
        compiler_params=_params("parallel", "arbitrary"),
        name="in_proj_" + mode,
    )(*operands)
    main, meta = res[:n_out], res[n_out:]
    if n_out == 1:
        return main[0], (meta[0] if with_meta else None)
    return main, (meta if with_meta else None)


def _rot_tables(seq, pos0):
    half = DK // 2
    inv = ROPE_BASE ** (-jnp.arange(half, dtype=F32) / half)
    pos = jnp.arange(seq, dtype=jnp.int32) + pos0
    ang = pos.astype(F32)[:, None] * inv[None, :]
    cos, sin = jnp.cos(ang), jnp.sin(ang)
    cos2 = jnp.concatenate([cos, cos], axis=1)
    sin2 = jnp.concatenate([-sin, sin], axis=1)
    scale = DK ** -0.5
    return jnp.stack([cos2, cos2 * scale]), jnp.stack([sin2, sin2 * scale])


def _ret_kernel(q_ref, k_ref, v_ref, g_ref, dmat_ref, qdec_ref, kdec_ref, cdec_ref, s0_ref,
                o_ref, sfin_ref, s_ref, *, hb, nchunks, pos0):
    @pl.when(pl.program_id(2) == 0)
    def _():
        s_ref[...] = s0_ref[...]

    def body(c, carry):
        r0 = pl.multiple_of(c * CHUNK, CHUNK)
        rows = pl.ds(r0, CHUNK)
        if pos0 < 0:
            valid = (lax.broadcasted_iota(jnp.int32, (CHUNK, 1), 0) + (r0 + pos0)) >= 0
        heads = range(hb)
        q = [q_ref[rows, j * DK:(j + 1) * DK] for j in heads]
        k = [k_ref[rows, j * DK:(j + 1) * DK] for j in heads]
        if pos0 < 0:
            k = [jnp.where(valid, kj, jnp.zeros_like(kj)) for kj in k]
        v = [v_ref[rows, j * DV:(j + 1) * DV] for j in heads]
        scores = [jnp.dot(q[j], k[j].astype(F32).T.astype(BF16), preferred_element_type=F32) for j in heads]
        s = [s_ref[j] for j in heads]
        new_s = [lax.dot_general(k[j] * kdec_ref[j], v[j], TN_DIMS, preferred_element_type=F32) for j in heads]
        y = [jnp.dot(jnp.concatenate([(scores[j] * dmat_ref[j]).astype(BF16), q[j] * qdec_ref[j]], axis=1),
                     jnp.concatenate([v[j], s[j].astype(BF16)], axis=0), preferred_element_type=F32)
             for j in heads]
        for j in heads:
            s_ref[j] = s[j] * cdec_ref[j] + new_s[j]
            g = g_ref[rows, j * DV:(j + 1) * DV].astype(F32)
            o_ref[rows, j * DV:(j + 1) * DV] = (_rms_scale(y[j]) * g).astype(o_ref.dtype)
        return carry

    lax.fori_loop(0, nchunks, body, 0, unroll=MIXER_UNROLL)
    sfin_ref[...] = s_ref[...]


def _ret_tables():
    log_g = jnp.log1p(-jnp.exp2(-5.0 - jnp.arange(HEADS, dtype=F32)))
    idx = jnp.arange(CHUNK, dtype=F32)
    diff = idx[:, None] - idx[None, :]
    dmat = jnp.where(diff[None] >= 0, jnp.exp(jnp.maximum(diff, 0.0)[None] * log_g[:, None, None]), 0.0)
    qdec = jnp.exp((idx + 1)[None, :] * log_g[:, None])
    kdec = jnp.exp((CHUNK - 1 - idx)[None, :] * log_g[:, None])
    cdec = jnp.exp(CHUNK * log_g)
    qdec = jnp.broadcast_to(qdec[:, :, None], (HEADS, CHUNK, DK)).astype(BF16)
    kdec = jnp.broadcast_to(kdec[:, :, None], (HEADS, CHUNK, DK)).astype(BF16)
    cdec = jnp.broadcast_to(cdec[:, None, None], (HEADS, 1, DV))
    return dmat, qdec, kdec, cdec


def _mixer_grid(batch, seq):
    rb = min(seq, MIXER_ROWS)
    return rb, seq // rb


def _ret_call(qk, vi, act, tables, s0, batch, seq, pos0):
    dmat, qdec, kdec, cdec = tables
    hb = HEADS_PER_STEP
    rb, nblk = _mixer_grid(batch, seq)
    k_blk = QK_W // (hb * DK)
    kern = functools.partial(_ret_kernel, hb=hb, nchunks=rb // CHUNK, pos0=pos0)
    per_head = lambda b, h, r: (h, 0, 0)
    return pl.pallas_call(
        kern,
        grid=(batch, HEADS // hb, nblk),
        in_specs=[
            pl.BlockSpec((rb, hb * DK), lambda b, h, r: (b * nblk + r, h)),
            pl.BlockSpec((rb, hb * DK), lambda b, h, r: (b * nblk + r, k_blk + h)),
            pl.BlockSpec((rb, hb * DV), lambda b, h, r: (b * nblk + r, h)),
            pl.BlockSpec((rb, hb * DV), lambda b, h, r: (b * nblk + r, h)),
            pl.BlockSpec((hb, CHUNK, CHUNK), per_head),
            pl.BlockSpec((hb, CHUNK, DK), per_head),
            pl.BlockSpec((hb, CHUNK, DK), per_head),
            pl.BlockSpec((hb, 1, DV), per_head),
            pl.BlockSpec((hb, DK, DV), per_head),
        ],
        out_specs=[
            pl.BlockSpec((rb, hb * DV), lambda b, h, r: (b * nblk + r, h)),
            pl.BlockSpec((None, hb, DK, DV), lambda b, h, r: (b, h, 0, 0)),
        ],
        out_shape=[jax.ShapeDtypeStruct((batch * seq, V_W), BF16),
                   jax.ShapeDtypeStruct((batch, HEADS, DK, DV), F32)],
        scratch_shapes=[pltpu.VMEM((hb, DK, DV), F32)],
        compiler_params=_params("parallel", "parallel", "arbitrary", flags=MIXER_FLAGS),
        name="retention",
    )(qk, qk, vi, act, dmat, qdec, kdec, cdec, s0)


def _hg_tables():
    t = np.arange(CHUNK)[:, None]
    u = np.arange(CHUNK)[None, :]
    mats, masks = [], []
    for j in range(LEVELS):
        m = 1 << j
        upper = ((t >> j) & 1) == 1
        q_part = upper & (u >= (t & ~(m - 1))) & (u <= t)
        k_part = (~upper) & (u > t) & (u <= (t | (m - 1)))
        if m < SUBLANES:
            mats.append(q_part | k_part)
        masks.append(((t >> (j + 1)) == (u >> (j + 1))) & upper & (((u >> j) & 1) == 0))
    mats.append(u <= t)
    masks.append(t == u)
    mstack = np.concatenate(mats, axis=0).astype(np.float32)
    mstack = np.concatenate([mstack, mstack], axis=1)
    return jnp.asarray(mstack, dtype=BF16), jnp.asarray(np.stack(masks).astype(np.float32))


def _hg_kernel(q_ref, kin_ref, hi_ref, lo_ref, v_ref, g_ref, nw_ref, mstack_ref, masks_ref, s0_ref,
               o_ref, sfin_ref, st_ref, e_ref, *, hb, nchunks):
    @pl.when(pl.program_id(2) == 0)
    def _():
        st_ref[...] = s0_ref[...]

    row = lax.broadcasted_iota(jnp.int32, (CHUNK, 1), 0)

    def exponents(c, slot, pair):
        rows = pl.ds(pl.multiple_of(c * CHUNK, CHUNK), CHUNK)
        lanes = slice(pair * 2 * DK, (pair + 1) * 2 * DK)
        pieces = jnp.concatenate([hi_ref[rows, lanes], lo_ref[rows, lanes]], axis=0)
        e_ref[slot, :, lanes] = jnp.dot(mstack_ref[...], pieces, preferred_element_type=F32)

    def process(c, slot, nxt):
        for g0 in range(0, hb, HG_GROUP):
            process_group(c, slot, nxt, range(g0, g0 + HG_GROUP))

    def process_group(c, slot, nxt, heads):
        rows = pl.ds(pl.multiple_of(c * CHUNK, CHUNK), CHUNK)
        expo = {j: e_ref.at[slot, :, j * DK:(j + 1) * DK] for j in heads}
        qb = {j: q_ref[rows, j * DK:(j + 1) * DK] for j in heads}
        q = {j: qb[j].astype(F32) for j in heads}
        kin = {j: kin_ref[rows, j * DK:(j + 1) * DK].astype(F32) for j in heads}
        b = {j: expo[j][SMALL_LEVELS * CHUNK:, :] for j in heads}
        a = {j: jnp.dot(qb[j], kin[j].T.astype(BF16), preferred_element_type=F32) * masks_ref[LEVELS]
             for j in heads}
        a_rows = None
        issue_at = {1 + 3 * i: p for i, p in enumerate(range(heads[0] // 2, heads[-1] // 2 + 1))}
        for lev in range(LEVELS):
            m = 1 << lev
            if nxt is not None and lev in issue_at:
                exponents(nxt, 1 - slot, issue_at[lev])
            if m < SUBLANES:
                for j in heads:
                    e = jnp.exp2(expo[j][lev * CHUNK:(lev + 1) * CHUNK, :])
                    x = jnp.where(((row >> lev) & 1) == 1, q[j], kin[j]) * e
                    a[j] = a[j] + (jnp.dot(x.astype(BF16), x.T.astype(BF16), preferred_element_type=F32)
                                   * masks_ref[lev])
            else:
                if a_rows is None:
                    a_rows = {j: [a[j][s:s + SUBLANES] for s in range(0, CHUNK, SUBLANES)] for j in heads}
                upper = [s for s0 in range(m, CHUNK, 2 * m) for s in range(s0, s0 + m, SUBLANES)]
                for j in heads:
                    parts = []
                    for s0 in range(0, CHUNK, 2 * m):
                        b_mid = jnp.broadcast_to(b[j][s0 + m - 1:s0 + m, :], (m, DK))
                        parts += [b_mid - b[j][s0:s0 + m], b[j][s0 + m:s0 + 2 * m] - b_mid]
                    e = jnp.exp2(jnp.concatenate(parts, axis=0))
                    ke_t = (kin[j] * e).T.astype(BF16)
                    qe = jnp.concatenate([q[j][s:s + SUBLANES] * e[s:s + SUBLANES] for s in upper], axis=0)
                    p = jnp.dot(qe.astype(BF16), ke_t, preferred_element_type=F32)
                    for idx, s in enumerate(upper):
                        a_rows[j][s // SUBLANES] = (a_rows[j][s // SUBLANES]
                                                    + p[idx * SUBLANES:(idx + 1) * SUBLANES]
                                                    * masks_ref[lev, s:s + SUBLANES, :])
        total = {j: b[j][CHUNK - 1:CHUNK, :] for j in heads}
        v = {j: v_ref[rows, j * DV:(j + 1) * DV] for j in heads}
        st = {j: st_ref[j] for j in heads}
        new_st = {j: lax.dot_general((kin[j] * jnp.exp2(total[j] - b[j])).astype(BF16), v[j], TN_DIMS,
                                     preferred_element_type=F32) for j in heads}
        y = {j: jnp.dot(jnp.concatenate([jnp.concatenate(a_rows[j], axis=0).astype(BF16),
                                         (q[j] * jnp.exp2(b[j])).astype(BF16)], axis=1),
                        jnp.concatenate([v[j], st[j].astype(BF16)], axis=0), preferred_element_type=F32)
             for j in heads}
        for j in heads:
            decay = jnp.broadcast_to(jnp.exp2(total[j]), (CHUNK, DK)).T
            st_ref[j] = st[j] * jnp.concatenate([decay] * (DV // DK), axis=1) + new_st[j]
            g = g_ref[rows, j * DV:(j + 1) * DV].astype(F32)
            o_ref[rows, j * DV:(j + 1) * DV] = (_rms_scale(y[j]) * nw_ref[...] * g).astype(o_ref.dtype)

    for pr in range(hb // 2):
        exponents(0, 0, pr)
    if nchunks == 1:
        process(0, 0, None)
    else:
        assert nchunks % 2 == 0

        def pair(p, carry):
            c = 2 * p
            process(c, 0, c + 1)
            process(c + 1, 1, jnp.minimum(c + 2, nchunks - 1))
            return carry

        lax.fori_loop(0, nchunks // 2, pair, 0)
    sfin_ref[...] = st_ref[...]


def _hg_call(act, kin, lf_hi, lf_lo, vi, nw_row, tables, s0, batch, seq):
    mstack, masks = tables
    hb = HEADS_PER_STEP
    rb, nblk = _mixer_grid(batch, seq)
    q_blk = V_W // (hb * DK)
    g_blk = (V_W + QK_W) // (hb * DV)
    i_blk = V_W // (hb * DV)
    kern = functools.partial(_hg_kernel, hb=hb, nchunks=rb // CHUNK)
    row_blk = lambda off: (lambda b, h, r: (b * nblk + r, off + h))
    return pl.pallas_call(
        kern,
        grid=(batch, HEADS // hb, nblk),
        in_specs=[
            pl.BlockSpec((rb, hb * DK), row_blk(q_blk)),
            pl.BlockSpec((rb, hb * DK), row_blk(0)),
            pl.BlockSpec((rb, hb * DK), row_blk(0)),
            pl.BlockSpec((rb, hb * DK), row_blk(0)),
            pl.BlockSpec((rb, hb * DV), row_blk(i_blk)),
            pl.BlockSpec((rb, hb * DV), row_blk(g_blk)),
            pl.BlockSpec((1, DV), lambda b, h, r: (0, 0)),
            pl.BlockSpec(mstack.shape, lambda b, h, r: (0, 0)),
            pl.BlockSpec(masks.shape, lambda b, h, r: (0, 0, 0)),
            pl.BlockSpec((hb, DK, DV), lambda b, h, r: (h, 0, 0)),
        ],
        out_specs=[
            pl.BlockSpec((rb, hb * DV), row_blk(0)),
            pl.BlockSpec((None, hb, DK, DV), lambda b, h, r: (b, h, 0, 0)),
        ],
        out_shape=[jax.ShapeDtypeStruct((batch * seq, V_W), BF16),
                   jax.ShapeDtypeStruct((batch, HEADS, DK, DV), F32)],
        scratch_shapes=[pltpu.VMEM((hb, DK, DV), F32),
                        pltpu.VMEM((2, (SMALL_LEVELS + 1) * CHUNK, hb * DK), F32)],
        compiler_params=_params("parallel", "parallel", "arbitrary", flags=MIXER_FLAGS),
        name="hgrn2",
    )(act, kin, lf_hi, lf_lo, vi, act, nw_row, mstack, masks, s0)


def _merge_kernel(yr_ref, yh_ref, wr_ref, wh_ref, gr_ref, gh_ref, o_ref):
    a = jnp.dot(yr_ref[...], wr_ref[...], preferred_element_type=F32)
    b = jnp.dot(yh_ref[...], wh_ref[...], preferred_element_type=F32)
    o_ref[...] = (gr_ref[...].astype(F32) * a + gh_ref[...].astype(F32) * b).astype(o_ref.dtype)


def _merge_call(yr, yh, gates, w_ret, w_hg, layer):
    rows, k = yr.shape
    d = w_ret.shape[-1]
    tm = min(rows, ROW_TILE)
    tn = 512
    gh_blk = d // tn
    return pl.pallas_call(
        _merge_kernel,
        grid=(rows // tm, d // tn),
        in_specs=[
            pl.BlockSpec((tm, k), lambda i, j: (i, 0)),
            pl.BlockSpec((tm, k), lambda i, j: (i, 0)),
            pl.BlockSpec((None, k, tn), lambda i, j: (layer, 0, j)),
            pl.BlockSpec((None, k, tn), lambda i, j: (layer, 0, j)),
            pl.BlockSpec((tm, tn), lambda i, j: (i, j)),
            pl.BlockSpec((tm, tn), lambda i, j: (i, gh_blk + j)),
        ],
        out_specs=pl.BlockSpec((tm, tn), lambda i, j: (i, j)),
        out_shape=jax.ShapeDtypeStruct((rows, d), BF16),
        compiler_params=_params("parallel", "arbitrary"),
        name="branch_merge",
    )(yr, yh, w_ret, w_hg, gates, gates)


def _out_kernel(y_ref, w_ref, h_ref, post_ref, nxt_ref, hn_ref, xn_ref):
    m = jnp.dot(y_ref[...], w_ref[...], preferred_element_type=F32)
    hn = h_ref[...] + _rms_scale(m) * post_ref[...]
    hn_ref[...] = hn
    xn_ref[...] = (_rms_scale(hn) * nxt_ref[...]).astype(xn_ref.dtype)


def _out_call(y, w_out, h, post_row, next_row, layer):
    rows, d = h.shape
    tm = min(rows, 512)
    row_blk = pl.BlockSpec((tm, d), lambda i: (i, 0))
    vec_blk = pl.BlockSpec((1, d), lambda i: (0, 0))
    return pl.pallas_call(
        _out_kernel,
        grid=(rows // tm,),
        in_specs=[row_blk, pl.BlockSpec((None, d, d), lambda i: (layer, 0, 0)), row_blk, vec_blk, vec_blk],
        out_specs=[row_blk, row_blk],
        out_shape=[jax.ShapeDtypeStruct((rows, d), F32), jax.ShapeDtypeStruct((rows, d), BF16)],
        compiler_params=_params("parallel"),
        name="out_proj",
    )(y, w_out, h, post_row, next_row)


def _ffn_kernel(*refs, with_next, cast_weights, n_prev):
    x_ref, wg_ref, wu_ref, wd_ref, h_ref, post_ref, nxt_ref = refs[:7]
    outs = refs[7 + n_prev:-1]
    acc_ref = refs[-1]
    hn_ref = outs[0]
    xn_ref = outs[1] if with_next else None
    f = pl.program_id(1)

    @pl.when(f == 0)
    def _():
        acc_ref[...] = jnp.zeros_like(acc_ref)

    wg, wu, wd = wg_ref[...], wu_ref[...], wd_ref[...]
    if cast_weights:
        wg, wu, wd = wg.astype(BF16), wu.astype(BF16), wd.astype(BF16)
        wg_out, wu_out, wd_out = outs[-3:]
        wg_out[...], wu_out[...], wd_out[...] = wg, wu, wd
    x = x_ref[...]
    g = jnp.dot(x, wg, preferred_element_type=F32)
    u = jnp.dot(x, wu, preferred_element_type=F32)
    act = (_silu(g) * u).astype(BF16)
    acc_ref[...] += jnp.dot(act, wd, preferred_element_type=F32)

    @pl.when(f == pl.num_programs(1) - 1)
    def _():
        hn = h_ref[...] + _rms_scale(acc_ref[...]) * post_ref[...]
        hn_ref[...] = hn
        if with_next:
            xn_ref[...] = (_rms_scale(hn) * nxt_ref[...]).astype(xn_ref.dtype)


def _ffn_call(xn, weights, h, post_row, next_row, with_next, layer=None, tile0=0, ntiles=None, prev=None):
    rows, d = h.shape
    cast_weights = layer is not None
    d_ff = weights[0].shape[-1]
    tm = min(rows, FFN_ROW_TILE)
    tf = FFN_CAST_COL_TILE if cast_weights else FFN_COL_TILE
    ntiles = rows // tm - tile0 if ntiles is None else ntiles
    row_blk = pl.BlockSpec((tm, d), lambda i, f: (i + tile0, 0))
    vec_blk = pl.BlockSpec((1, d), lambda i, f: (0, 0))
    w2d_specs = [pl.BlockSpec((d, tf), lambda i, f: (0, f)), pl.BlockSpec((d, tf), lambda i, f: (0, f)),
                 pl.BlockSpec((tf, d), lambda i, f: (f, 0))]
    if cast_weights:
        w_specs = [pl.BlockSpec((None, d, tf), lambda i, f: (layer, 0, f)),
                   pl.BlockSpec((None, d, tf), lambda i, f: (layer, 0, f)),
                   pl.BlockSpec((None, tf, d), lambda i, f: (layer, f, 0))]
    else:
        w_specs = w2d_specs
    n_state = 2 if with_next else 1
    out_specs = [row_blk] * n_state
    out_shape = [jax.ShapeDtypeStruct((rows, d), F32), jax.ShapeDtypeStruct((rows, d), BF16)][:n_state]
    if cast_weights:
        out_specs += w2d_specs
        out_shape += [jax.ShapeDtypeStruct((d, d_ff), BF16), jax.ShapeDtypeStruct((d, d_ff), BF16),
                      jax.ShapeDtypeStruct((d_ff, d), BF16)]
    operands = [xn, *weights, h, post_row, next_row]
    in_specs = [row_blk, *w_specs, row_blk, vec_blk, vec_blk]
    aliases = {}
    n_prev = 0
    if prev is not None:
        n_prev = n_state
        for idx in range(n_state):
            aliases[len(operands)] = idx
            operands.append(prev[idx])
            in_specs.append(pl.BlockSpec(memory_space=pl.ANY))
    res = pl.pallas_call(
        functools.partial(_ffn_kernel, with_next=with_next, cast_weights=cast_weights, n_prev=n_prev),
        grid=(ntiles, d_ff // tf),
        in_specs=in_specs,
        out_specs=out_specs,
        out_shape=out_shape,
        input_output_aliases=aliases,
        scratch_shapes=[pltpu.VMEM((tm, d), F32)],
        compiler_params=_params("parallel", "arbitrary"),
        name="swiglu_ffn",
    )(*operands)
    state = (res[0], res[1] if with_next else None)
    return (state, tuple(res[n_state:])) if cast_weights else state


def kernel(x, meta_tokens, norm_mix_pre, norm_mix_post, norm_ffn_pre, norm_ffn_post, w_in, hg_lb_logits,
           hg_norm_w, w_br_ret, w_br_hg, w_out, w_ffn_gate, w_ffn_up, w_ffn_down):
    batch, seq, d = x.shape
    depth = w_in.shape[0]
    assert seq % CHUNK == 0 and meta_tokens.shape == (N_META, d)
    assert w_in.shape[-1] == sum(_IN_WIDTHS) and d == V_W

    lb_sm = jax.nn.softmax(hg_lb_logits.astype(F32), axis=0)
    lbs = jnp.cumsum(lb_sm, axis=0) - lb_sm[0:1]

    w_ret_b, w_hg_b, w_out_b = (w.astype(BF16) for w in (w_br_ret, w_br_hg, w_out))

    hg_tables = _hg_tables()
    ret_tables = _ret_tables()
    meta_h = jnp.concatenate([jnp.zeros((PAD, d), F32), meta_tokens.astype(F32)], axis=0)
    meta = dict(h=meta_h, batch=1, seq=CHUNK, pos0=-PAD)
    main = dict(h=x.reshape(batch * seq, d).astype(F32), batch=batch, seq=seq, pos0=N_META)
    for rs in (meta, main):
        rs["rot"] = _rot_tables(rs["seq"], rs["pos0"])
        rs["xn"] = _norm_call(rs["h"], norm_mix_pre[0][None])

    for l in range(depth):
        last = l == depth - 1
        proj = functools.partial(_proj_call, main["xn"], meta["xn"], w_in, l, seq=seq)
        main["qk"], meta["qk"] = proj(ROT_TILES, "rot", extra=main["rot"], meta_extra=meta["rot"])
        main["vi"], meta["vi"] = proj(IDENT_TILES, "ident")
        main["act"], meta["act"] = proj(SILU_TILES, "silu")
        main["forget"], meta["forget"] = proj(FORGET_TILES, "forget", extra=(lbs[l][None],))
        if last:
            main["gates"], _ = _proj_call(main["xn"], None, w_in, l, SIGMOID_TILES, "sigmoid", seq)
        else:
            main["gates"], meta["gates"] = proj(SIGMOID_TILES, "sigmoid")

        ret_state = jnp.zeros((HEADS, DK, DV), F32)
        hg_state = jnp.zeros((HEADS, DK, DV), F32)
        for rs in (meta, main):
            kin, lf_hi, lf_lo = rs["forget"]
            rs["yr"], ret_fin = _ret_call(rs["qk"], rs["vi"], rs["act"], ret_tables, ret_state,
                                          rs["batch"], rs["seq"], rs["pos0"])
            rs["yh"], hg_fin = _hg_call(rs["act"], kin, lf_hi, lf_lo, rs["vi"], hg_norm_w[l][None], hg_tables,
                                        hg_state, rs["batch"], rs["seq"])
            if rs is meta:
                ret_state, hg_state = ret_fin[0], hg_fin[0]

        next_row = norm_mix_pre[min(l + 1, depth - 1)][None]
        post_row = norm_ffn_post[l][None]
        w_ffn_b = None
        for rs in (main,) if last else (main, meta):
            y = _merge_call(rs["yr"], rs["yh"], rs["gates"], w_ret_b, w_hg_b, l)
            h_mid, xn_ffn = _out_call(y, w_out_b, rs["h"], norm_mix_post[l][None], norm_ffn_pre[l][None], l)
            if w_ffn_b is None:
                state, w_ffn_b = _ffn_call(xn_ffn, (w_ffn_gate, w_ffn_up, w_ffn_down), h_mid, post_row, next_row,
                                           not last, layer=l, ntiles=1)
                if h_mid.shape[0] > FFN_ROW_TILE:
                    state = _ffn_call(xn_ffn, w_ffn_b, h_mid, post_row, next_row, not last, tile0=1, prev=state)
            else:
                state = _ffn_call(xn_ffn, w_ffn_b, h_mid, post_row, next_row, not last)
            rs["h"], rs["xn"] = state
    return main["h"].reshape(batch, seq, d)
```

```python
import functools

import numpy as np
import jax
import jax.numpy as jnp
from jax import lax
from jax.experimental import pallas as pl
from jax.experimental.pallas import tpu as pltpu

N_META = 16
HEADS = 8
DK = 128
DV = 256
CHUNK = 128
PAD = CHUNK - N_META
RMS_EPS = 1e-6
ROPE_BASE = 10000.0
LEVELS = 7
SUBLANES = 8
SMALL_LEVELS = 3
BF16_SUBLANES = 16
QK_W = HEADS * DK
V_W = HEADS * DV

VMEM_LIMIT_BYTES = 58 * 1024 * 1024
ROW_TILE = 1024
IN_TILE = 1024
PROJ_SLAB = 512
PROJ_VMEM_BUDGET = 54 * 1024 * 1024
HEADS_PER_STEP = 4
MIXER_ROWS = 1024
MIXER_UNROLL = 2
FFN_ROW_TILE = 512
FFN_COL_TILE = 512
MERGE_COL_TILE = 512
XLU_TRANSPOSE_LEVELS = (4, 5, 6)
HG_GROUP = 4

_IN_WIDTHS = (QK_W, QK_W, V_W, V_W, QK_W, QK_W, V_W, V_W, V_W, V_W)
_IN_STARTS = tuple(sum(_IN_WIDTHS[:i]) // IN_TILE for i in range(len(_IN_WIDTHS)))


def _tiles(*parts):
    return tuple(t for p in parts for t in range(_IN_STARTS[p], _IN_STARTS[p] + _IN_WIDTHS[p] // IN_TILE))


ROT_TILES = _tiles(0, 1)
IDENT_TILES = _tiles(2, 6)
SILU_TILES = _tiles(3, 4, 7)
FORGET_TILES = _tiles(5)
SIGMOID_TILES = _tiles(8, 9)

F32 = jnp.float32
BF16 = jnp.bfloat16
NT_DIMS = (((1,), (1,)), ((), ()))
TN_DIMS = (((0,), (0,)), ((), ()))


def _params(*semantics, flags=None):
    return pltpu.CompilerParams(dimension_semantics=semantics, vmem_limit_bytes=VMEM_LIMIT_BYTES, flags=flags)


MIXER_FLAGS = None


def _sigmoid(x):
    return 0.5 + 0.5 * jnp.tanh(0.5 * x)


def _silu(x):
    h = 0.5 * x
    return h + h * jnp.tanh(h)


def _rms_scale(x):
    return x * lax.rsqrt(jnp.mean(x * x, axis=-1, keepdims=True) + RMS_EPS)


def _tile_lookup(tiles):
    def lookup(j):
        out = tiles[-1]
        for idx in range(len(tiles) - 2, -1, -1):
            out = jnp.where(j == idx, tiles[idx], out)
        return out
    return lookup


def _side_cast_specs(arrays, layer, nsteps, step_of):
    in_specs, out_specs, out_shape = [], [], []
    for a in arrays:
        _, r, c = a.shape
        assert r % (nsteps * BF16_SUBLANES) == 0, (a.shape, nsteps)
        br = r // nsteps
        in_specs.append(pl.BlockSpec((None, br, c), lambda *g: (layer, step_of(*g), 0)))
        out_specs.append(pl.BlockSpec((br, c), lambda *g: (step_of(*g), 0)))
        out_shape.append(jax.ShapeDtypeStruct((r, c), BF16))
    return in_specs, out_specs, out_shape


def _side_cast(side_in, side_out):
    for i_ref, o_ref in zip(side_in, side_out):
        o_ref[...] = i_ref[...].astype(BF16)


def _norm_kernel(h_ref, w_ref, o_ref):
    o_ref[...] = (_rms_scale(h_ref[...]) * w_ref[...]).astype(o_ref.dtype)


def _norm_call(h, w_row):
    rows, d = h.shape
    tm = min(rows, 512)
    return pl.pallas_call(
        _norm_kernel,
        grid=(rows // tm,),
        in_specs=[pl.BlockSpec((tm, d), lambda i: (i, 0)), pl.BlockSpec((1, d), lambda i: (0, 0))],
        out_specs=pl.BlockSpec((tm, d), lambda i: (i, 0)),
        out_shape=jax.ShapeDtypeStruct((rows, d), BF16),
        compiler_params=_params("parallel"),
        name="rms_norm",
    )(h, w_row)


def _proj_epilogue(acc, mode, extras, outs, r, first_pos):
    if mode == "ident":
        outs[0][r, :] = acc.astype(BF16)
    elif mode == "silu":
        outs[0][r, :] = _silu(acc).astype(BF16)
    elif mode == "sigmoid":
        outs[0][r, :] = _sigmoid(acc).astype(BF16)
    elif mode == "rot":
        cos, sin = extras[0][r, :], extras[1][r, :]
        for g in range(IN_TILE // DK):
            xg = acc[:, g * DK:(g + 1) * DK]
            outs[0][r, g * DK:(g + 1) * DK] = (xg * cos + pltpu.roll(xg, DK // 2, 1) * sin).astype(BF16)
    elif mode == "forget":
        kin_ref, hi_ref, lo_ref = outs
        one_minus_f = (1.0 - extras[0][...]) * (0.5 - 0.5 * jnp.tanh(0.5 * acc))
        log_f = jnp.log2(1.0 - one_minus_f)
        if first_pos is not None:
            valid = (lax.broadcasted_iota(jnp.int32, (acc.shape[0], 1), 0) + first_pos) >= 0
            log_f = jnp.where(valid, log_f, 0.0)
            one_minus_f = jnp.where(valid, one_minus_f, 0.0)
        kin_ref[r, :] = one_minus_f.astype(BF16)
        hi = log_f.astype(BF16)
        hi_ref[r, :] = hi
        lo_ref[r, :] = (log_f - hi.astype(F32)).astype(BF16)


def _proj_kernel(*refs, mode, with_meta, n_side):
    n_extra = {"rot": 2, "forget": 1}.get(mode, 0)
    n_out = 3 if mode == "forget" else 1
    x_ref, w_ref = refs[:2]
    extras = refs[2:2 + n_extra]
    pos = 2 + n_extra
    if with_meta:
        xm_ref = refs[pos]
        n_meta_extra = 2 if mode == "rot" else 0
        meta_extras = refs[pos + 1:pos + 1 + n_meta_extra] if n_meta_extra else extras
        pos += 1 + n_meta_extra
    side_in = refs[pos:pos + n_side]
    pos += n_side
    outs = refs[pos:pos + n_out]
    pos += n_out
    if with_meta:
        meta_outs = refs[pos:pos + n_out]
        pos += n_out
    _side_cast(side_in, refs[pos:pos + n_side])
    wb_ref = refs[-1]

    @pl.when(pl.program_id(1) == 0)
    def _():
        wb_ref[...] = w_ref[...].astype(BF16)
        if with_meta:
            acc = jnp.dot(xm_ref[...], wb_ref[...], preferred_element_type=F32)
            _proj_epilogue(acc, mode, meta_extras, meta_outs, slice(None), -PAD)

    tm = x_ref.shape[0]
    slab = min(tm, PROJ_SLAB)
    pending = None
    for s0 in range(0, tm, slab):
        r = slice(s0, s0 + slab)
        acc = jnp.dot(x_ref[r, :], wb_ref[...], preferred_element_type=F32)
        if pending is not None:
            _proj_epilogue(*pending)
        pending = (acc, mode, extras, outs, r, None)
    _proj_epilogue(*pending)


def _proj_row_tile(rows, k, n_out, side_bytes):
    for tm in (2 * ROW_TILE, ROW_TILE):
        if tm > rows or rows % tm:
            continue
        x_bytes = 2 * tm * k * 2
        w_bytes = 2 * k * IN_TILE * 4 + k * IN_TILE * 2
        out_bytes = n_out * 2 * tm * IN_TILE * 2
        acc_bytes = 2 * PROJ_SLAB * IN_TILE * 4
        meta_bytes = 2 * CHUNK * (k + n_out * IN_TILE) * 2
        side = 2 * (side_bytes + side_bytes // 2) // (rows // tm)
        if x_bytes + w_bytes + out_bytes + acc_bytes + meta_bytes + side <= PROJ_VMEM_BUDGET:
            return tm
    return min(rows, ROW_TILE)


def _proj_call(xn, xn_meta, w_in, layer, tiles, mode, seq, extra=(), meta_extra=(), side=()):
    rows, k = xn.shape
    n_out = 3 if mode == "forget" else 1
    with_meta = xn_meta is not None
    side_bytes = sum(a.shape[1] * a.shape[2] * 4 for a in side) // len(tiles)
    tm = _proj_row_tile(rows, k, n_out, side_bytes)
    n_row_tiles = rows // tm
    lookup = _tile_lookup(tiles)
    out_cols = len(tiles) * IN_TILE
    in_specs = [pl.BlockSpec((tm, k), lambda j, i: (i, 0)),
                pl.BlockSpec((None, k, IN_TILE), lambda j, i: (layer, 0, lookup(j)))]
    if mode == "rot":
        blocks_per_seq = seq // tm
        tab = pl.BlockSpec((None, tm, DK), lambda j, i: (j, i % blocks_per_seq, 0))
        in_specs += [tab, tab]
    elif mode == "forget":
        in_specs += [pl.BlockSpec((1, IN_TILE), lambda j, i: (0, 0))]
    operands = [xn, w_in, *extra]
    out_specs = [pl.BlockSpec((tm, IN_TILE), lambda j, i: (i, j))] * n_out
    out_shape = [jax.ShapeDtypeStruct((rows, out_cols), BF16)] * n_out
    if with_meta:
        in_specs += [pl.BlockSpec((CHUNK, k), lambda j, i: (0, 0))]
        operands += [xn_meta]
        if mode == "rot":
            in_specs += [pl.BlockSpec((None, CHUNK, DK), lambda j, i: (j, 0, 0))] * 2
            operands += list(meta_extra)
        out_specs += [pl.BlockSpec((CHUNK, IN_TILE), lambda j, i: (0, j))] * n_out
        out_shape += [jax.ShapeDtypeStruct((CHUNK, out_cols), BF16)] * n_out
    side_in, side_out, side_shape = _side_cast_specs(side, layer, len(tiles) * n_row_tiles,
                                                     lambda j, i: j * n_row_tiles + i)
    res = pl.pallas_call(
        functools.partial(_proj_kernel, mode=mode, with_meta=with_meta, n_side=len(side)),
        grid=(len(tiles), n_row_tiles),
        in_specs=in_specs + side_in,
        out_specs=out_specs + side_out,
        out_shape=out_shape + side_shape,
        scratch_shapes=[pltpu.VMEM((k, IN_TILE), BF16)],
        compiler_params=_params("parallel", "arbitrary"),
        name="in_proj_" + mode,
    )(*operands, *side)
    main = res[:n_out]
    meta = res[n_out:2 * n_out] if with_meta else None
    casts = tuple(res[len(res) - len(side):]) if side else ()
    if n_out == 1:
        main, meta = main[0], (meta[0] if with_meta else None)
    return main, meta, casts


def _rot_tables(seq, pos0):
    half = DK // 2
    inv = ROPE_BASE ** (-jnp.arange(half, dtype=F32) / half)
    pos = jnp.arange(seq, dtype=jnp.int32) + pos0
    ang = pos.astype(F32)[:, None] * inv[None, :]
    cos, sin = jnp.cos(ang), jnp.sin(ang)
    cos2 = jnp.concatenate([cos, cos], axis=1)
    sin2 = jnp.concatenate([-sin, sin], axis=1)
    scale = DK ** -0.5
    return jnp.stack([cos2, cos2 * scale]), jnp.stack([sin2, sin2 * scale])


def _ret_kernel(q_ref, k_ref, v_ref, g_ref, dmat_ref, qdec_ref, kdec_ref, cdec_ref, s0_ref,
                o_ref, sfin_ref, s_ref, *, hb, nchunks, pos0):
    @pl.when(pl.program_id(2) == 0)
    def _():
        s_ref[...] = s0_ref[...]

    def body(c, carry):
        r0 = pl.multiple_of(c * CHUNK, CHUNK)
        rows = pl.ds(r0, CHUNK)
        if pos0 < 0:
            valid = (lax.broadcasted_iota(jnp.int32, (CHUNK, 1), 0) + (r0 + pos0)) >= 0
        heads = range(hb)
        q = [q_ref[rows, j * DK:(j + 1) * DK] for j in heads]
        k = [k_ref[rows, j * DK:(j + 1) * DK] for j in heads]
        if pos0 < 0:
            k = [jnp.where(valid, kj, jnp.zeros_like(kj)) for kj in k]
        v = [v_ref[rows, j * DV:(j + 1) * DV] for j in heads]
        scores = [jnp.dot(q[j], k[j].astype(F32).T.astype(BF16), preferred_element_type=F32) for j in heads]
        s = [s_ref[j] for j in heads]
        new_s = [lax.dot_general(k[j] * kdec_ref[j], v[j], TN_DIMS, preferred_element_type=F32) for j in heads]
        y = [jnp.dot(jnp.concatenate([(scores[j] * dmat_ref[j]).astype(BF16), q[j] * qdec_ref[j]], axis=1),
                     jnp.concatenate([v[j], s[j].astype(BF16)], axis=0), preferred_element_type=F32)
             for j in heads]
        for j in heads:
            s_ref[j] = s[j] * cdec_ref[j] + new_s[j]
            g = g_ref[rows, j * DV:(j + 1) * DV].astype(F32)
            o_ref[rows, j * DV:(j + 1) * DV] = (_rms_scale(y[j]) * g).astype(o_ref.dtype)
        return carry

    lax.fori_loop(0, nchunks, body, 0, unroll=MIXER_UNROLL)
    sfin_ref[...] = s_ref[...]


def _ret_tables():
    log_g = jnp.log1p(-jnp.exp2(-5.0 - jnp.arange(HEADS, dtype=F32)))
    idx = jnp.arange(CHUNK, dtype=F32)
    diff = idx[:, None] - idx[None, :]
    dmat = jnp.where(diff[None] >= 0, jnp.exp(jnp.maximum(diff, 0.0)[None] * log_g[:, None, None]), 0.0)
    qdec = jnp.exp((idx + 1)[None, :] * log_g[:, None])
    kdec = jnp.exp((CHUNK - 1 - idx)[None, :] * log_g[:, None])
    cdec = jnp.exp(CHUNK * log_g)
    qdec = jnp.broadcast_to(qdec[:, :, None], (HEADS, CHUNK, DK)).astype(BF16)
    kdec = jnp.broadcast_to(kdec[:, :, None], (HEADS, CHUNK, DK)).astype(BF16)
    cdec = jnp.broadcast_to(cdec[:, None, None], (HEADS, 1, DV))
    return dmat, qdec, kdec, cdec


def _mixer_grid(batch, seq):
    rb = min(seq, MIXER_ROWS)
    return rb, seq // rb


def _ret_call(qk, vi, act, tables, s0, batch, seq, pos0):
    dmat, qdec, kdec, cdec = tables
    hb = HEADS_PER_STEP
    rb, nblk = _mixer_grid(batch, seq)
    k_blk = QK_W // (hb * DK)
    kern = functools.partial(_ret_kernel, hb=hb, nchunks=rb // CHUNK, pos0=pos0)
    per_head = lambda b, h, r: (h, 0, 0)
    return pl.pallas_call(
        kern,
        grid=(batch, HEADS // hb, nblk),
        in_specs=[
            pl.BlockSpec((rb, hb * DK), lambda b, h, r: (b * nblk + r, h)),
            pl.BlockSpec((rb, hb * DK), lambda b, h, r: (b * nblk + r, k_blk + h)),
            pl.BlockSpec((rb, hb * DV), lambda b, h, r: (b * nblk + r, h)),
            pl.BlockSpec((rb, hb * DV), lambda b, h, r: (b * nblk + r, h)),
            pl.BlockSpec((hb, CHUNK, CHUNK), per_head),
            pl.BlockSpec((hb, CHUNK, DK), per_head),
            pl.BlockSpec((hb, CHUNK, DK), per_head),
            pl.BlockSpec((hb, 1, DV), per_head),
            pl.BlockSpec((hb, DK, DV), per_head),
        ],
        out_specs=[
            pl.BlockSpec((rb, hb * DV), lambda b, h, r: (b * nblk + r, h)),
            pl.BlockSpec((None, hb, DK, DV), lambda b, h, r: (b, h, 0, 0)),
        ],
        out_shape=[jax.ShapeDtypeStruct((batch * seq, V_W), BF16),
                   jax.ShapeDtypeStruct((batch, HEADS, DK, DV), F32)],
        scratch_shapes=[pltpu.VMEM((hb, DK, DV), F32)],
        compiler_params=_params("parallel", "parallel", "arbitrary", flags=MIXER_FLAGS),
        name="retention",
    )(qk, qk, vi, act, dmat, qdec, kdec, cdec, s0)


def _hg_tables():
    t = np.arange(CHUNK)[:, None]
    u = np.arange(CHUNK)[None, :]
    mats, masks = [], []
    for j in range(LEVELS):
        m = 1 << j
        upper = ((t >> j) & 1) == 1
        q_part = upper & (u >= (t & ~(m - 1))) & (u <= t)
        k_part = (~upper) & (u > t) & (u <= (t | (m - 1)))
        if m < SUBLANES:
            mats.append(q_part | k_part)
        masks.append(((t >> (j + 1)) == (u >> (j + 1))) & upper & (((u >> j) & 1) == 0))
    mats.append(u <= t)
    masks.append(t == u)
    mstack = np.concatenate(mats, axis=0).astype(np.float32)
    mstack = np.concatenate([mstack, mstack], axis=1)
    return jnp.asarray(mstack, dtype=BF16), jnp.asarray(np.stack(masks).astype(np.float32))


def _dot_keys(queries, keys, level):
    if level in XLU_TRANSPOSE_LEVELS:
        return jnp.dot(queries, keys.T.astype(BF16), preferred_element_type=F32)
    return lax.dot_general(queries, keys.astype(BF16), NT_DIMS, preferred_element_type=F32)


def _hg_kernel(q_ref, kin_ref, hi_ref, lo_ref, v_ref, g_ref, nw_ref, mstack_ref, masks_ref, s0_ref,
               o_ref, sfin_ref, st_ref, e_ref, *, hb, nchunks):
    @pl.when(pl.program_id(2) == 0)
    def _():
        st_ref[...] = s0_ref[...]

    row = lax.broadcasted_iota(jnp.int32, (CHUNK, 1), 0)

    def exponents(c, slot, pair):
        rows = pl.ds(pl.multiple_of(c * CHUNK, CHUNK), CHUNK)
        lanes = slice(pair * 2 * DK, (pair + 1) * 2 * DK)
        pieces = jnp.concatenate([hi_ref[rows, lanes], lo_ref[rows, lanes]], axis=0)
        e_ref[slot, :, lanes] = jnp.dot(mstack_ref[...], pieces, preferred_element_type=F32)

    def process(c, slot, nxt):
        for g0 in range(0, hb, HG_GROUP):
            process_group(c, slot, nxt, range(g0, g0 + HG_GROUP))

    def process_group(c, slot, nxt, heads):
        rows = pl.ds(pl.multiple_of(c * CHUNK, CHUNK), CHUNK)
        expo = {j: e_ref.at[slot, :, j * DK:(j + 1) * DK] for j in heads}
        qb = {j: q_ref[rows, j * DK:(j + 1) * DK] for j in heads}
        q = {j: qb[j].astype(F32) for j in heads}
        kin = {j: kin_ref[rows, j * DK:(j + 1) * DK].astype(F32) for j in heads}
        b = {j: expo[j][SMALL_LEVELS * CHUNK:, :] for j in heads}
        a = {j: _dot_keys(qb[j], kin[j], LEVELS) * masks_ref[LEVELS] for j in heads}
        a_rows = None
        issue_at = {1 + 3 * i: p for i, p in enumerate(range(heads[0] // 2, heads[-1] // 2 + 1))}
        for lev in range(LEVELS):
            m = 1 << lev
            if nxt is not None and lev in issue_at:
                exponents(nxt, 1 - slot, issue_at[lev])
            if m < SUBLANES:
                for j in heads:
                    e = jnp.exp2(expo[j][lev * CHUNK:(lev + 1) * CHUNK, :])
                    x = jnp.where(((row >> lev) & 1) == 1, q[j], kin[j]) * e
                    a[j] = a[j] + _dot_keys(x.astype(BF16), x, lev) * masks_ref[lev]
            else:
                if a_rows is None:
                    a_rows = {j: [a[j][s:s + SUBLANES] for s in range(0, CHUNK, SUBLANES)] for j in heads}
                upper = [s for s0 in range(m, CHUNK, 2 * m) for s in range(s0, s0 + m, SUBLANES)]
                for j in heads:
                    parts = []
                    for s0 in range(0, CHUNK, 2 * m):
                        b_mid = jnp.broadcast_to(b[j][s0 + m - 1:s0 + m, :], (m, DK))
                        parts += [b_mid - b[j][s0:s0 + m], b[j][s0 + m:s0 + 2 * m] - b_mid]
                    e = jnp.exp2(jnp.concatenate(parts, axis=0))
                    qe = jnp.concatenate([q[j][s:s + SUBLANES] * e[s:s + SUBLANES] for s in upper], axis=0)
                    p = _dot_keys(qe.astype(BF16), kin[j] * e, lev)
                    for idx, s in enumerate(upper):
                        a_rows[j][s // SUBLANES] = (a_rows[j][s // SUBLANES]
                                                    + p[idx * SUBLANES:(idx + 1) * SUBLANES]
                                                    * masks_ref[lev, s:s + SUBLANES, :])
        total = {j: b[j][CHUNK - 1:CHUNK, :] for j in heads}
        v = {j: v_ref[rows, j * DV:(j + 1) * DV] for j in heads}
        st = {j: st_ref[j] for j in heads}
        new_st = {j: lax.dot_general((kin[j] * jnp.exp2(total[j] - b[j])).astype(BF16), v[j], TN_DIMS,
                                     preferred_element_type=F32) for j in heads}
        y = {j: jnp.dot(jnp.concatenate([jnp.concatenate(a_rows[j], axis=0).astype(BF16),
                                         (q[j] * jnp.exp2(b[j])).astype(BF16)], axis=1),
                        jnp.concatenate([v[j], st[j].astype(BF16)], axis=0), preferred_element_type=F32)
             for j in heads}
        for j in heads:
            decay = jnp.broadcast_to(jnp.exp2(total[j]), (CHUNK, DK)).T
            st_ref[j] = st[j] * jnp.concatenate([decay] * (DV // DK), axis=1) + new_st[j]
            g = g_ref[rows, j * DV:(j + 1) * DV].astype(F32)
            o_ref[rows, j * DV:(j + 1) * DV] = (_rms_scale(y[j]) * nw_ref[...] * g).astype(o_ref.dtype)

    for pr in range(hb // 2):
        exponents(0, 0, pr)
    if nchunks == 1:
        process(0, 0, None)
    else:
        assert nchunks % 2 == 0

        def pair(p, carry):
            c = 2 * p
            process(c, 0, c + 1)
            process(c + 1, 1, jnp.minimum(c + 2, nchunks - 1))
            return carry

        lax.fori_loop(0, nchunks // 2, pair, 0)
    sfin_ref[...] = st_ref[...]


def _hg_call(act, kin, lf_hi, lf_lo, vi, nw_row, tables, s0, batch, seq):
    mstack, masks = tables
    hb = HEADS_PER_STEP
    rb, nblk = _mixer_grid(batch, seq)
    q_blk = V_W // (hb * DK)
    g_blk = (V_W + QK_W) // (hb * DV)
    i_blk = V_W // (hb * DV)
    kern = functools.partial(_hg_kernel, hb=hb, nchunks=rb // CHUNK)
    row_blk = lambda off: (lambda b, h, r: (b * nblk + r, off + h))
    return pl.pallas_call(
        kern,
        grid=(batch, HEADS // hb, nblk),
        in_specs=[
            pl.BlockSpec((rb, hb * DK), row_blk(q_blk)),
            pl.BlockSpec((rb, hb * DK), row_blk(0)),
            pl.BlockSpec((rb, hb * DK), row_blk(0)),
            pl.BlockSpec((rb, hb * DK), row_blk(0)),
            pl.BlockSpec((rb, hb * DV), row_blk(i_blk)),
            pl.BlockSpec((rb, hb * DV), row_blk(g_blk)),
            pl.BlockSpec((1, DV), lambda b, h, r: (0, 0)),
            pl.BlockSpec(mstack.shape, lambda b, h, r: (0, 0)),
            pl.BlockSpec(masks.shape, lambda b, h, r: (0, 0, 0)),
            pl.BlockSpec((hb, DK, DV), lambda b, h, r: (h, 0, 0)),
        ],
        out_specs=[
            pl.BlockSpec((rb, hb * DV), row_blk(0)),
            pl.BlockSpec((None, hb, DK, DV), lambda b, h, r: (b, h, 0, 0)),
        ],
        out_shape=[jax.ShapeDtypeStruct((batch * seq, V_W), BF16),
                   jax.ShapeDtypeStruct((batch, HEADS, DK, DV), F32)],
        scratch_shapes=[pltpu.VMEM((hb, DK, DV), F32),
                        pltpu.VMEM((2, (SMALL_LEVELS + 1) * CHUNK, hb * DK), F32)],
        compiler_params=_params("parallel", "parallel", "arbitrary", flags=MIXER_FLAGS),
        name="hgrn2",
    )(act, kin, lf_hi, lf_lo, vi, act, nw_row, mstack, masks, s0)


def _merge_kernel(yr_ref, yh_ref, wr_ref, wh_ref, gr_ref, gh_ref, *rest, n_side):
    o_ref = rest[n_side]
    _side_cast(rest[:n_side], rest[n_side + 1:])
    a = jnp.dot(yr_ref[...], wr_ref[...], preferred_element_type=F32)
    b = jnp.dot(yh_ref[...], wh_ref[...], preferred_element_type=F32)
    o_ref[...] = (gr_ref[...].astype(F32) * a + gh_ref[...].astype(F32) * b).astype(o_ref.dtype)


def _merge_call(yr, yh, gates, w_ret, w_hg, layer=None, side=()):
    rows, k = yr.shape
    d = w_ret.shape[-1]
    tm = min(rows, ROW_TILE)
    tn = MERGE_COL_TILE
    gh_blk = d // tn
    n_col_tiles = d // tn
    side_in, side_out, side_shape = _side_cast_specs(side, layer, (rows // tm) * n_col_tiles,
                                                     lambda i, j: i * n_col_tiles + j)
    res = pl.pallas_call(
        functools.partial(_merge_kernel, n_side=len(side)),
        grid=(rows // tm, n_col_tiles),
        in_specs=[
            pl.BlockSpec((tm, k), lambda i, j: (i, 0)),
            pl.BlockSpec((tm, k), lambda i, j: (i, 0)),
            pl.BlockSpec((k, tn), lambda i, j: (0, j)),
            pl.BlockSpec((k, tn), lambda i, j: (0, j)),
            pl.BlockSpec((tm, tn), lambda i, j: (i, j)),
            pl.BlockSpec((tm, tn), lambda i, j: (i, gh_blk + j)),
        ] + side_in,
        out_specs=[pl.BlockSpec((tm, tn), lambda i, j: (i, j))] + side_out,
        out_shape=[jax.ShapeDtypeStruct((rows, d), BF16)] + side_shape,
        compiler_params=_params("parallel", "arbitrary"),
        name="branch_merge",
    )(yr, yh, w_ret, w_hg, gates, gates, *side)
    return res[0], tuple(res[1:])


def _out_kernel(y_ref, w_ref, h_ref, post_ref, nxt_ref, hn_ref, xn_ref):
    m = jnp.dot(y_ref[...], w_ref[...], preferred_element_type=F32)
    hn = h_ref[...] + _rms_scale(m) * post_ref[...]
    hn_ref[...] = hn
    xn_ref[...] = (_rms_scale(hn) * nxt_ref[...]).astype(xn_ref.dtype)


def _out_call(y, w_out, h, post_row, next_row):
    rows, d = h.shape
    tm = min(rows, FFN_ROW_TILE)
    row_blk = pl.BlockSpec((tm, d), lambda i: (i, 0))
    vec_blk = pl.BlockSpec((1, d), lambda i: (0, 0))
    return pl.pallas_call(
        _out_kernel,
        grid=(rows // tm,),
        in_specs=[row_blk, pl.BlockSpec((d, d), lambda i: (0, 0)), row_blk, vec_blk, vec_blk],
        out_specs=[row_blk, row_blk],
        out_shape=[jax.ShapeDtypeStruct((rows, d), F32), jax.ShapeDtypeStruct((rows, d), BF16)],
        compiler_params=_params("parallel"),
        name="out_proj",
    )(y, w_out, h, post_row, next_row)


def _ffn_kernel(x_ref, wg_ref, wu_ref, wd_ref, h_ref, post_ref, nxt_ref, hn_ref, *rest, with_next):
    if with_next:
        xn_ref, acc_ref = rest
    else:
        (acc_ref,) = rest
    f = pl.program_id(1)

    @pl.when(f == 0)
    def _():
        acc_ref[...] = jnp.zeros_like(acc_ref)

    x = x_ref[...]
    g = jnp.dot(x, wg_ref[...], preferred_element_type=F32)
    u = jnp.dot(x, wu_ref[...], preferred_element_type=F32)
    act = (_silu(g) * u).astype(BF16)
    acc_ref[...] += jnp.dot(act, wd_ref[...], preferred_element_type=F32)

    @pl.when(f == pl.num_programs(1) - 1)
    def _():
        hn = h_ref[...] + _rms_scale(acc_ref[...]) * post_ref[...]
        hn_ref[...] = hn
        if with_next:
            xn_ref[...] = (_rms_scale(hn) * nxt_ref[...]).astype(xn_ref.dtype)


def _ffn_call(xn, w_gate, w_up, w_down, h, post_row, next_row, with_next):
    rows, d = h.shape
    d_ff = w_gate.shape[-1]
    tm = min(rows, FFN_ROW_TILE)
    tf = FFN_COL_TILE
    row_blk = pl.BlockSpec((tm, d), lambda i, f: (i, 0))
    vec_blk = pl.BlockSpec((1, d), lambda i, f: (0, 0))
    n_state = 2 if with_next else 1
    res = pl.pallas_call(
        functools.partial(_ffn_kernel, with_next=with_next),
        grid=(rows // tm, d_ff // tf),
        in_specs=[
            row_blk,
            pl.BlockSpec((d, tf), lambda i, f: (0, f)),
            pl.BlockSpec((d, tf), lambda i, f: (0, f)),
            pl.BlockSpec((tf, d), lambda i, f: (f, 0)),
            row_blk, vec_blk, vec_blk,
        ],
        out_specs=[row_blk] * n_state,
        out_shape=[jax.ShapeDtypeStruct((rows, d), F32), jax.ShapeDtypeStruct((rows, d), BF16)][:n_state],
        scratch_shapes=[pltpu.VMEM((tm, d), F32)],
        compiler_params=_params("parallel", "arbitrary"),
        name="swiglu_ffn",
    )(xn, w_gate, w_up, w_down, h, post_row, next_row)
    return res[0], (res[1] if with_next else None)


def kernel(x, meta_tokens, norm_mix_pre, norm_mix_post, norm_ffn_pre, norm_ffn_post, w_in, hg_lb_logits,
           hg_norm_w, w_br_ret, w_br_hg, w_out, w_ffn_gate, w_ffn_up, w_ffn_down):
    batch, seq, d = x.shape
    depth = w_in.shape[0]
    assert seq % CHUNK == 0 and meta_tokens.shape == (N_META, d)
    assert w_in.shape[-1] == sum(_IN_WIDTHS) and d == V_W

    lb_sm = jax.nn.softmax(hg_lb_logits.astype(F32), axis=0)
    lbs = jnp.cumsum(lb_sm, axis=0) - lb_sm[0:1]

    d_ff = w_ffn_gate.shape[-1]
    w_down_view = w_ffn_down.reshape(depth, 2 * d_ff, d // 2)

    hg_tables = _hg_tables()
    ret_tables = _ret_tables()
    meta_h = jnp.concatenate([jnp.zeros((PAD, d), F32), meta_tokens.astype(F32)], axis=0)
    meta = dict(h=meta_h, batch=1, seq=CHUNK, pos0=-PAD)
    main = dict(h=x.reshape(batch * seq, d).astype(F32), batch=batch, seq=seq, pos0=N_META)
    for rs in (meta, main):
        rs["rot"] = _rot_tables(rs["seq"], rs["pos0"])
        rs["xn"] = _norm_call(rs["h"], norm_mix_pre[0][None])

    for l in range(depth):
        last = l == depth - 1
        proj = functools.partial(_proj_call, main["xn"], meta["xn"], w_in, l, seq=seq)
        main["qk"], meta["qk"], _ = proj(ROT_TILES, "rot", extra=main["rot"], meta_extra=meta["rot"])
        main["vi"], meta["vi"], (w_ret_b, w_hg_b, w_out_b) = proj(IDENT_TILES, "ident",
                                                                 side=(w_br_ret, w_br_hg, w_out))
        main["act"], meta["act"], _ = proj(SILU_TILES, "silu")
        main["forget"], meta["forget"], _ = proj(FORGET_TILES, "forget", extra=(lbs[l][None],))
        if last:
            main["gates"], _, _ = _proj_call(main["xn"], None, w_in, l, SIGMOID_TILES, "sigmoid", seq)
        else:
            main["gates"], meta["gates"], _ = proj(SIGMOID_TILES, "sigmoid")

        ret_state = jnp.zeros((HEADS, DK, DV), F32)
        hg_state = jnp.zeros((HEADS, DK, DV), F32)
        for rs in (meta, main):
            kin, lf_hi, lf_lo = rs["forget"]
            rs["yr"], ret_fin = _ret_call(rs["qk"], rs["vi"], rs["act"], ret_tables, ret_state,
                                          rs["batch"], rs["seq"], rs["pos0"])
            rs["yh"], hg_fin = _hg_call(rs["act"], kin, lf_hi, lf_lo, rs["vi"], hg_norm_w[l][None], hg_tables,
                                        hg_state, rs["batch"], rs["seq"])
            if rs is meta:
                ret_state, hg_state = ret_fin[0], hg_fin[0]

        next_row = norm_mix_pre[min(l + 1, depth - 1)][None]
        post_row = norm_ffn_post[l][None]
        w_ffn_b = None
        for rs in (main,) if last else (main, meta):
            if w_ffn_b is None:
                y, (w_gate_b, w_up_b, w_down_b) = _merge_call(rs["yr"], rs["yh"], rs["gates"], w_ret_b, w_hg_b, l,
                                                            side=(w_ffn_gate, w_ffn_up, w_down_view))
                w_ffn_b = (w_gate_b, w_up_b, w_down_b.reshape(d_ff, d))
            else:
                y, _ = _merge_call(rs["yr"], rs["yh"], rs["gates"], w_ret_b, w_hg_b)
            h_mid, xn_ffn = _out_call(y, w_out_b, rs["h"], norm_mix_post[l][None], norm_ffn_pre[l][None])
            rs["h"], rs["xn"] = _ffn_call(xn_ffn, *w_ffn_b, h_mid, post_row, next_row, not last)
    return main["h"].reshape(batch, seq, d)
```

```python
import functools

import numpy as np
import jax
import jax.numpy as jnp
from jax import lax
from jax.experimental import pallas as pl
from jax.experimental.pallas import tpu as pltpu

N_META = 16
HEADS = 8
DK = 128
DV = 256
CHUNK = 128
PAD = CHUNK - N_META
RMS_EPS = 1e-6
ROPE_BASE = 10000.0
LEVELS = 7
SUBLANES = 8
SMALL_LEVELS = 3
BF16_SUBLANES = 16
QK_W = HEADS * DK
V_W = HEADS * DV

VMEM_LIMIT_BYTES = 58 * 1024 * 1024
ROW_TILE = 1024
IN_TILE = 1024
PROJ_SLAB = 512
PROJ_VMEM_BUDGET = 54 * 1024 * 1024
HEADS_PER_STEP = 4
MIXER_ROWS = 1024
MIXER_UNROLL = 2
FFN_ROW_TILE = 512
FFN_COL_TILE = 512
MERGE_COL_TILE = 512
XLU_TRANSPOSE_LEVELS = (4, 5, 6)
HG_GROUP = 4

_IN_WIDTHS = (QK_W, QK_W, V_W, V_W, QK_W, QK_W, V_W, V_W, V_W, V_W)
_IN_STARTS = tuple(sum(_IN_WIDTHS[:i]) // IN_TILE for i in range(len(_IN_WIDTHS)))


def _tiles(*parts):
    return tuple(t for p in parts for t in range(_IN_STARTS[p], _IN_STARTS[p] + _IN_WIDTHS[p] // IN_TILE))


ROT_TILES = _tiles(0, 1)
IDENT_TILES = _tiles(2, 6)
SILU_TILES = _tiles(3, 4, 7)
FORGET_TILES = _tiles(5)
SIGMOID_TILES = _tiles(8, 9)

F32 = jnp.float32
BF16 = jnp.bfloat16
NT_DIMS = (((1,), (1,)), ((), ()))
TN_DIMS = (((0,), (0,)), ((), ()))


def _params(*semantics, flags=None):
    return pltpu.CompilerParams(dimension_semantics=semantics, vmem_limit_bytes=VMEM_LIMIT_BYTES, flags=flags)


MIXER_FLAGS = None


def _sigmoid(x):
    return 0.5 + 0.5 * jnp.tanh(0.5 * x)


def _silu(x):
    h = 0.5 * x
    return h + h * jnp.tanh(h)


def _rms_scale(x):
    return x * lax.rsqrt(jnp.mean(x * x, axis=-1, keepdims=True) + RMS_EPS)


def _tile_lookup(tiles):
    def lookup(j):
        out = tiles[-1]
        for idx in range(len(tiles) - 2, -1, -1):
            out = jnp.where(j == idx, tiles[idx], out)
        return out
    return lookup


def _side_cast_specs(arrays, layer, nsteps, step_of):
    in_specs, out_specs, out_shape = [], [], []
    for a in arrays:
        _, r, c = a.shape
        assert r % (nsteps * BF16_SUBLANES) == 0, (a.shape, nsteps)
        br = r // nsteps
        in_specs.append(pl.BlockSpec((None, br, c), lambda *g: (layer, step_of(*g), 0)))
        out_specs.append(pl.BlockSpec((br, c), lambda *g: (step_of(*g), 0)))
        out_shape.append(jax.ShapeDtypeStruct((r, c), BF16))
    return in_specs, out_specs, out_shape


def _side_cast(side_in, side_out):
    for i_ref, o_ref in zip(side_in, side_out):
        o_ref[...] = i_ref[...].astype(BF16)


def _norm_kernel(h_ref, w_ref, o_ref):
    o_ref[...] = (_rms_scale(h_ref[...]) * w_ref[...]).astype(o_ref.dtype)


def _norm_call(h, w_row):
    rows, d = h.shape
    tm = min(rows, 512)
    return pl.pallas_call(
        _norm_kernel,
        grid=(rows // tm,),
        in_specs=[pl.BlockSpec((tm, d), lambda i: (i, 0)), pl.BlockSpec((1, d), lambda i: (0, 0))],
        out_specs=pl.BlockSpec((tm, d), lambda i: (i, 0)),
        out_shape=jax.ShapeDtypeStruct((rows, d), BF16),
        compiler_params=_params("parallel"),
        name="rms_norm",
    )(h, w_row)


def _proj_epilogue(acc, mode, extras, outs, r, first_pos):
    if mode == "ident":
        outs[0][r, :] = acc.astype(BF16)
    elif mode == "silu":
        outs[0][r, :] = _silu(acc).astype(BF16)
    elif mode == "sigmoid":
        outs[0][r, :] = _sigmoid(acc).astype(BF16)
    elif mode == "rot":
        cos, sin = extras[0][r, :], extras[1][r, :]
        for g in range(IN_TILE // DK):
            xg = acc[:, g * DK:(g + 1) * DK]
            outs[0][r, g * DK:(g + 1) * DK] = (xg * cos + pltpu.roll(xg, DK // 2, 1) * sin).astype(BF16)
    elif mode == "forget":
        kin_ref, hi_ref, lo_ref = outs
        one_minus_f = (1.0 - extras[0][...]) * (0.5 - 0.5 * jnp.tanh(0.5 * acc))
        log_f = jnp.log2(1.0 - one_minus_f)
        if first_pos is not None:
            valid = (lax.broadcasted_iota(jnp.int32, (acc.shape[0], 1), 0) + first_pos) >= 0
            log_f = jnp.where(valid, log_f, 0.0)
            one_minus_f = jnp.where(valid, one_minus_f, 0.0)
        kin_ref[r, :] = one_minus_f.astype(BF16)
        hi = log_f.astype(BF16)
        hi_ref[r, :] = hi
        lo_ref[r, :] = (log_f - hi.astype(F32)).astype(BF16)


def _proj_kernel(*refs, mode, with_meta, n_side):
    n_extra = {"rot": 2, "forget": 1}.get(mode, 0)
    n_out = 3 if mode == "forget" else 1
    x_ref, w_ref = refs[:2]
    extras = refs[2:2 + n_extra]
    pos = 2 + n_extra
    if with_meta:
        xm_ref = refs[pos]
        n_meta_extra = 2 if mode == "rot" else 0
        meta_extras = refs[pos + 1:pos + 1 + n_meta_extra] if n_meta_extra else extras
        pos += 1 + n_meta_extra
    side_in = refs[pos:pos + n_side]
    pos += n_side
    outs = refs[pos:pos + n_out]
    pos += n_out
    if with_meta:
        meta_outs = refs[pos:pos + n_out]
        pos += n_out
    _side_cast(side_in, refs[pos:pos + n_side])
    wb_ref = refs[-1]

    @pl.when(pl.program_id(1) == 0)
    def _():
        wb_ref[...] = w_ref[...].astype(BF16)
        if with_meta:
            acc = jnp.dot(xm_ref[...], wb_ref[...], preferred_element_type=F32)
            _proj_epilogue(acc, mode, meta_extras, meta_outs, slice(None), -PAD)

    tm = x_ref.shape[0]
    slab = min(tm, PROJ_SLAB)
    pending = None
    for s0 in range(0, tm, slab):
        r = slice(s0, s0 + slab)
        acc = jnp.dot(x_ref[r, :], wb_ref[...], preferred_element_type=F32)
        if pending is not None:
            _proj_epilogue(*pending)
        pending = (acc, mode, extras, outs, r, None)
    _proj_epilogue(*pending)


def _proj_row_tile(rows, k, n_out, side_bytes):
    for tm in (2 * ROW_TILE, ROW_TILE):
        if tm > rows or rows % tm:
            continue
        x_bytes = 2 * tm * k * 2
        w_bytes = 2 * k * IN_TILE * 4 + k * IN_TILE * 2
        out_bytes = n_out * 2 * tm * IN_TILE * 2
        acc_bytes = 2 * PROJ_SLAB * IN_TILE * 4
        meta_bytes = 2 * CHUNK * (k + n_out * IN_TILE) * 2
        side = 2 * (side_bytes + side_bytes // 2) // (rows // tm)
        if x_bytes + w_bytes + out_bytes + acc_bytes + meta_bytes + side <= PROJ_VMEM_BUDGET:
            return tm
    return min(rows, ROW_TILE)


def _proj_call(xn, xn_meta, w_in, layer, tiles, mode, seq, extra=(), meta_extra=(), side=()):
    rows, k = xn.shape
    n_out = 3 if mode == "forget" else 1
    with_meta = xn_meta is not None
    side_bytes = sum(a.shape[1] * a.shape[2] * 4 for a in side) // len(tiles)
    tm = _proj_row_tile(rows, k, n_out, side_bytes)
    n_row_tiles = rows // tm
    lookup = _tile_lookup(tiles)
    out_cols = len(tiles) * IN_TILE
    in_specs = [pl.BlockSpec((tm, k), lambda j, i: (i, 0)),
                pl.BlockSpec((None, k, IN_TILE), lambda j, i: (layer, 0, lookup(j)))]
    if mode == "rot":
        blocks_per_seq = seq // tm
        tab = pl.BlockSpec((None, tm, DK), lambda j, i: (j, i % blocks_per_seq, 0))
        in_specs += [tab, tab]
    elif mode == "forget":
        in_specs += [pl.BlockSpec((1, IN_TILE), lambda j, i: (0, 0))]
    operands = [xn, w_in, *extra]
    out_specs = [pl.BlockSpec((tm, IN_TILE), lambda j, i: (i, j))] * n_out
    out_shape = [jax.ShapeDtypeStruct((rows, out_cols), BF16)] * n_out
    if with_meta:
        in_specs += [pl.BlockSpec((CHUNK, k), lambda j, i: (0, 0))]
        operands += [xn_meta]
        if mode == "rot":
            in_specs += [pl.BlockSpec((None, CHUNK, DK), lambda j, i: (j, 0, 0))] * 2
            operands += list(meta_extra)
        out_specs += [pl.BlockSpec((CHUNK, IN_TILE), lambda j, i: (0, j))] * n_out
        out_shape += [jax.ShapeDtypeStruct((CHUNK, out_cols), BF16)] * n_out
    side_in, side_out, side_shape = _side_cast_specs(side, layer, len(tiles) * n_row_tiles,
                                                     lambda j, i: j * n_row_tiles + i)
    res = pl.pallas_call(
        functools.partial(_proj_kernel, mode=mode, with_meta=with_meta, n_side=len(side)),
        grid=(len(tiles), n_row_tiles),
        in_specs=in_specs + side_in,
        out_specs=out_specs + side_out,
        out_shape=out_shape + side_shape,
        scratch_shapes=[pltpu.VMEM((k, IN_TILE), BF16)],
        compiler_params=_params("parallel", "arbitrary"),
        name="in_proj_" + mode,
    )(*operands, *side)
    main = res[:n_out]
    meta = res[n_out:2 * n_out] if with_meta else None
    casts = tuple(res[len(res) - len(side):]) if side else ()
    if n_out == 1:
        main, meta = main[0], (meta[0] if with_meta else None)
    return main, meta, casts


def _rot_tables(seq, pos0):
    half = DK // 2
    inv = ROPE_BASE ** (-jnp.arange(half, dtype=F32) / half)
    pos = jnp.arange(seq, dtype=jnp.int32) + pos0
    ang = pos.astype(F32)[:, None] * inv[None, :]
    cos, sin = jnp.cos(ang), jnp.sin(ang)
    cos2 = jnp.concatenate([cos, cos], axis=1)
    sin2 = jnp.concatenate([-sin, sin], axis=1)
    scale = DK ** -0.5
    return jnp.stack([cos2, cos2 * scale]), jnp.stack([sin2, sin2 * scale])


def _ret_kernel(q_ref, k_ref, v_ref, g_ref, dmat_ref, qdec_ref, kdec_ref, cdec_ref, s0_ref,
                o_ref, sfin_ref, s_ref, *, hb, nchunks, pos0):
    @pl.when(pl.program_id(2) == 0)
    def _():
        s_ref[...] = s0_ref[...]

    def body(c, carry):
        r0 = pl.multiple_of(c * CHUNK, CHUNK)
        rows = pl.ds(r0, CHUNK)
        if pos0 < 0:
            valid = (lax.broadcasted_iota(jnp.int32, (CHUNK, 1), 0) + (r0 + pos0)) >= 0
        heads = range(hb)
        q = [q_ref[rows, j * DK:(j + 1) * DK] for j in heads]
        k = [k_ref[rows, j * DK:(j + 1) * DK] for j in heads]
        if pos0 < 0:
            k = [jnp.where(valid, kj, jnp.zeros_like(kj)) for kj in k]
        v = [v_ref[rows, j * DV:(j + 1) * DV] for j in heads]
        scores = [jnp.dot(q[j], k[j].astype(F32).T.astype(BF16), preferred_element_type=F32) for j in heads]
        s = [s_ref[j] for j in heads]
        new_s = [lax.dot_general(k[j] * kdec_ref[j], v[j], TN_DIMS, preferred_element_type=F32) for j in heads]
        y = [jnp.dot(jnp.concatenate([(scores[j] * dmat_ref[j]).astype(BF16), q[j] * qdec_ref[j]], axis=1),
                     jnp.concatenate([v[j], s[j].astype(BF16)], axis=0), preferred_element_type=F32)
             for j in heads]
        for j in heads:
            s_ref[j] = s[j] * cdec_ref[j] + new_s[j]
            g = g_ref[rows, j * DV:(j + 1) * DV].astype(F32)
            o_ref[rows, j * DV:(j + 1) * DV] = (_rms_scale(y[j]) * g).astype(o_ref.dtype)
        return carry

    lax.fori_loop(0, nchunks, body, 0, unroll=MIXER_UNROLL)
    sfin_ref[...] = s_ref[...]


def _ret_tables():
    log_g = jnp.log1p(-jnp.exp2(-5.0 - jnp.arange(HEADS, dtype=F32)))
    idx = jnp.arange(CHUNK, dtype=F32)
    diff = idx[:, None] - idx[None, :]
    dmat = jnp.where(diff[None] >= 0, jnp.exp(jnp.maximum(diff, 0.0)[None] * log_g[:, None, None]), 0.0)
    qdec = jnp.exp((idx + 1)[None, :] * log_g[:, None])
    kdec = jnp.exp((CHUNK - 1 - idx)[None, :] * log_g[:, None])
    cdec = jnp.exp(CHUNK * log_g)
    qdec = jnp.broadcast_to(qdec[:, :, None], (HEADS, CHUNK, DK)).astype(BF16)
    kdec = jnp.broadcast_to(kdec[:, :, None], (HEADS, CHUNK, DK)).astype(BF16)
    cdec = jnp.broadcast_to(cdec[:, None, None], (HEADS, 1, DV))
    return dmat, qdec, kdec, cdec


def _mixer_grid(batch, seq):
    rb = min(seq, MIXER_ROWS)
    return rb, seq // rb


def _ret_call(qk, vi, act, tables, s0, batch, seq, pos0):
    dmat, qdec, kdec, cdec = tables
    hb = HEADS_PER_STEP
    rb, nblk = _mixer_grid(batch, seq)
    k_blk = QK_W // (hb * DK)
    kern = functools.partial(_ret_kernel, hb=hb, nchunks=rb // CHUNK, pos0=pos0)
    per_head = lambda b, h, r: (h, 0, 0)
    return pl.pallas_call(
        kern,
        grid=(batch, HEADS // hb, nblk),
        in_specs=[
            pl.BlockSpec((rb, hb * DK), lambda b, h, r: (b * nblk + r, h)),
            pl.BlockSpec((rb, hb * DK), lambda b, h, r: (b * nblk + r, k_blk + h)),
            pl.BlockSpec((rb, hb * DV), lambda b, h, r: (b * nblk + r, h)),
            pl.BlockSpec((rb, hb * DV), lambda b, h, r: (b * nblk + r, h)),
            pl.BlockSpec((hb, CHUNK, CHUNK), per_head),
            pl.BlockSpec((hb, CHUNK, DK), per_head),
            pl.BlockSpec((hb, CHUNK, DK), per_head),
            pl.BlockSpec((hb, 1, DV), per_head),
            pl.BlockSpec((hb, DK, DV), per_head),
        ],
        out_specs=[
            pl.BlockSpec((rb, hb * DV), lambda b, h, r: (b * nblk + r, h)),
            pl.BlockSpec((None, hb, DK, DV), lambda b, h, r: (b, h, 0, 0)),
        ],
        out_shape=[jax.ShapeDtypeStruct((batch * seq, V_W), BF16),
                   jax.ShapeDtypeStruct((batch, HEADS, DK, DV), F32)],
        scratch_shapes=[pltpu.VMEM((hb, DK, DV), F32)],
        compiler_params=_params("parallel", "parallel", "arbitrary", flags=MIXER_FLAGS),
        name="retention",
    )(qk, qk, vi, act, dmat, qdec, kdec, cdec, s0)


def _hg_tables():
    t = np.arange(CHUNK)[:, None]
    u = np.arange(CHUNK)[None, :]
    mats, masks = [], []
    for j in range(LEVELS):
        m = 1 << j
        upper = ((t >> j) & 1) == 1
        q_part = upper & (u >= (t & ~(m - 1))) & (u <= t)
        k_part = (~upper) & (u > t) & (u <= (t | (m - 1)))
        if m < SUBLANES:
            mats.append(q_part | k_part)
        masks.append(((t >> (j + 1)) == (u >> (j + 1))) & upper & (((u >> j) & 1) == 0))
    mats.append(u <= t)
    masks.append(t == u)
    mstack = np.concatenate(mats, axis=0).astype(np.float32)
    mstack = np.concatenate([mstack, mstack], axis=1)
    return jnp.asarray(mstack, dtype=BF16), jnp.asarray(np.stack(masks).astype(np.float32))


def _dot_keys(queries, keys, level):
    if level in XLU_TRANSPOSE_LEVELS:
        return jnp.dot(queries, keys.T.astype(BF16), preferred_element_type=F32)
    return lax.dot_general(queries, keys.astype(BF16), NT_DIMS, preferred_element_type=F32)


def _hg_kernel(q_ref, kin_ref, hi_ref, lo_ref, v_ref, g_ref, nw_ref, mstack_ref, masks_ref, s0_ref,
               o_ref, sfin_ref, st_ref, e_ref, *, hb, nchunks):
    @pl.when(pl.program_id(2) == 0)
    def _():
        st_ref[...] = s0_ref[...]

    row = lax.broadcasted_iota(jnp.int32, (CHUNK, 1), 0)

    def exponents(c, slot, pair):
        rows = pl.ds(pl.multiple_of(c * CHUNK, CHUNK), CHUNK)
        lanes = slice(pair * 2 * DK, (pair + 1) * 2 * DK)
        pieces = jnp.concatenate([hi_ref[rows, lanes], lo_ref[rows, lanes]], axis=0)
        e_ref[slot, :, lanes] = jnp.dot(mstack_ref[...], pieces, preferred_element_type=F32)

    def process(c, slot, nxt):
        for g0 in range(0, hb, HG_GROUP):
            process_group(c, slot, nxt, range(g0, g0 + HG_GROUP))

    def process_group(c, slot, nxt, heads):
        rows = pl.ds(pl.multiple_of(c * CHUNK, CHUNK), CHUNK)
        expo = {j: e_ref.at[slot, :, j * DK:(j + 1) * DK] for j in heads}
        qb = {j: q_ref[rows, j * DK:(j + 1) * DK] for j in heads}
        q = {j: qb[j].astype(F32) for j in heads}
        kin = {j: kin_ref[rows, j * DK:(j + 1) * DK].astype(F32) for j in heads}
        b = {j: expo[j][SMALL_LEVELS * CHUNK:, :] for j in heads}
        a = {j: _dot_keys(qb[j], kin[j], LEVELS) * masks_ref[LEVELS] for j in heads}
        a_rows = None
        issue_at = {1 + 3 * i: p for i, p in enumerate(range(heads[0] // 2, heads[-1] // 2 + 1))}
        for lev in range(LEVELS):
            m = 1 << lev
            if nxt is not None and lev in issue_at:
                exponents(nxt, 1 - slot, issue_at[lev])
            if m < SUBLANES:
                for j in heads:
                    e = jnp.exp2(expo[j][lev * CHUNK:(lev + 1) * CHUNK, :])
                    x = jnp.where(((row >> lev) & 1) == 1, q[j], kin[j]) * e
                    a[j] = a[j] + _dot_keys(x.astype(BF16), x, lev) * masks_ref[lev]
            else:
                if a_rows is None:
                    a_rows = {j: [a[j][s:s + SUBLANES] for s in range(0, CHUNK, SUBLANES)] for j in heads}
                upper = [s for s0 in range(m, CHUNK, 2 * m) for s in range(s0, s0 + m, SUBLANES)]
                for j in heads:
                    parts = []
                    for s0 in range(0, CHUNK, 2 * m):
                        b_mid = jnp.broadcast_to(b[j][s0 + m - 1:s0 + m, :], (m, DK))
                        parts += [b_mid - b[j][s0:s0 + m], b[j][s0 + m:s0 + 2 * m] - b_mid]
                    e = jnp.exp2(jnp.concatenate(parts, axis=0))
                    qe = jnp.concatenate([q[j][s:s + SUBLANES] * e[s:s + SUBLANES] for s in upper], axis=0)
                    p = _dot_keys(qe.astype(BF16), kin[j] * e, lev)
                    for idx, s in enumerate(upper):
                        a_rows[j][s // SUBLANES] = (a_rows[j][s // SUBLANES]
                                                    + p[idx * SUBLANES:(idx + 1) * SUBLANES]
                                                    * masks_ref[lev, s:s + SUBLANES, :])
        total = {j: b[j][CHUNK - 1:CHUNK, :] for j in heads}
        v = {j: v_ref[rows, j * DV:(j + 1) * DV] for j in heads}
        st = {j: st_ref[j] for j in heads}
        new_st = {j: lax.dot_general((kin[j] * jnp.exp2(total[j] - b[j])).astype(BF16), v[j], TN_DIMS,
                                     preferred_element_type=F32) for j in heads}
        y = {j: jnp.dot(jnp.concatenate([jnp.concatenate(a_rows[j], axis=0).astype(BF16),
                                         (q[j] * jnp.exp2(b[j])).astype(BF16)], axis=1),
                        jnp.concatenate([v[j], st[j].astype(BF16)], axis=0), preferred_element_type=F32)
             for j in heads}
        for j in heads:
            decay = jnp.broadcast_to(jnp.exp2(total[j]), (CHUNK, DK)).T
            st_ref[j] = st[j] * jnp.concatenate([decay] * (DV // DK), axis=1) + new_st[j]
            g = g_ref[rows, j * DV:(j + 1) * DV].astype(F32)
            o_ref[rows, j * DV:(j + 1) * DV] = (_rms_scale(y[j]) * nw_ref[...] * g).astype(o_ref.dtype)

    for pr in range(hb // 2):
        exponents(0, 0, pr)
    if nchunks == 1:
        process(0, 0, None)
    else:
        assert nchunks % 2 == 0

        def pair(p, carry):
            c = 2 * p
            process(c, 0, c + 1)
            process(c + 1, 1, jnp.minimum(c + 2, nchunks - 1))
            return carry

        lax.fori_loop(0, nchunks // 2, pair, 0)
    sfin_ref[...] = st_ref[...]


def _hg_call(act, kin, lf_hi, lf_lo, vi, nw_row, tables, s0, batch, seq):
    mstack, masks = tables
    hb = HEADS_PER_STEP
    rb, nblk = _mixer_grid(batch, seq)
    q_blk = V_W // (hb * DK)
    g_blk = (V_W + QK_W) // (hb * DV)
    i_blk = V_W // (hb * DV)
    kern = functools.partial(_hg_kernel, hb=hb, nchunks=rb // CHUNK)
    row_blk = lambda off: (lambda b, h, r: (b * nblk + r, off + h))
    return pl.pallas_call(
        kern,
        grid=(batch, HEADS // hb, nblk),
        in_specs=[
            pl.BlockSpec((rb, hb * DK), row_blk(q_blk)),
            pl.BlockSpec((rb, hb * DK), row_blk(0)),
            pl.BlockSpec((rb, hb * DK), row_blk(0)),
            pl.BlockSpec((rb, hb * DK), row_blk(0)),
            pl.BlockSpec((rb, hb * DV), row_blk(i_blk)),
            pl.BlockSpec((rb, hb * DV), row_blk(g_blk)),
            pl.BlockSpec((1, DV), lambda b, h, r: (0, 0)),
            pl.BlockSpec(mstack.shape, lambda b, h, r: (0, 0)),
            pl.BlockSpec(masks.shape, lambda b, h, r: (0, 0, 0)),
            pl.BlockSpec((hb, DK, DV), lambda b, h, r: (h, 0, 0)),
        ],
        out_specs=[
            pl.BlockSpec((rb, hb * DV), row_blk(0)),
            pl.BlockSpec((None, hb, DK, DV), lambda b, h, r: (b, h, 0, 0)),
        ],
        out_shape=[jax.ShapeDtypeStruct((batch * seq, V_W), BF16),
                   jax.ShapeDtypeStruct((batch, HEADS, DK, DV), F32)],
        scratch_shapes=[pltpu.VMEM((hb, DK, DV), F32),
                        pltpu.VMEM((2, (SMALL_LEVELS + 1) * CHUNK, hb * DK), F32)],
        compiler_params=_params("parallel", "parallel", "arbitrary", flags=MIXER_FLAGS),
        name="hgrn2",
    )(act, kin, lf_hi, lf_lo, vi, act, nw_row, mstack, masks, s0)


def _merge_kernel(yr_ref, yh_ref, wr_ref, wh_ref, gr_ref, gh_ref, o_ref):
    a = jnp.dot(yr_ref[...], wr_ref[...], preferred_element_type=F32)
    b = jnp.dot(yh_ref[...], wh_ref[...], preferred_element_type=F32)
    o_ref[...] = (gr_ref[...].astype(F32) * a + gh_ref[...].astype(F32) * b).astype(o_ref.dtype)


def _merge_call(yr, yh, gates, w_ret, w_hg):
    rows, k = yr.shape
    d = w_ret.shape[-1]
    tm = min(rows, ROW_TILE)
    tn = MERGE_COL_TILE
    gh_blk = d // tn
    return pl.pallas_call(
        _merge_kernel,
        grid=(rows // tm, d // tn),
        in_specs=[
            pl.BlockSpec((tm, k), lambda i, j: (i, 0)),
            pl.BlockSpec((tm, k), lambda i, j: (i, 0)),
            pl.BlockSpec((k, tn), lambda i, j: (0, j)),
            pl.BlockSpec((k, tn), lambda i, j: (0, j)),
            pl.BlockSpec((tm, tn), lambda i, j: (i, j)),
            pl.BlockSpec((tm, tn), lambda i, j: (i, gh_blk + j)),
        ],
        out_specs=pl.BlockSpec((tm, tn), lambda i, j: (i, j)),
        out_shape=jax.ShapeDtypeStruct((rows, d), BF16),
        compiler_params=_params("parallel", "arbitrary"),
        name="branch_merge",
    )(yr, yh, w_ret, w_hg, gates, gates)


def _out_kernel(y_ref, w_ref, h_ref, post_ref, nxt_ref, *rest, n_side):
    hn_ref, xn_ref = rest[n_side:n_side + 2]
    _side_cast(rest[:n_side], rest[n_side + 2:])
    m = jnp.dot(y_ref[...], w_ref[...], preferred_element_type=F32)
    hn = h_ref[...] + _rms_scale(m) * post_ref[...]
    hn_ref[...] = hn
    xn_ref[...] = (_rms_scale(hn) * nxt_ref[...]).astype(xn_ref.dtype)


def _out_call(y, w_out, h, post_row, next_row, layer=None, side=()):
    rows, d = h.shape
    tm = min(rows, FFN_ROW_TILE)
    row_blk = pl.BlockSpec((tm, d), lambda i: (i, 0))
    vec_blk = pl.BlockSpec((1, d), lambda i: (0, 0))
    side_in, side_out, side_shape = _side_cast_specs(side, layer, rows // tm, lambda i: i)
    res = pl.pallas_call(
        functools.partial(_out_kernel, n_side=len(side)),
        grid=(rows // tm,),
        in_specs=[row_blk, pl.BlockSpec((d, d), lambda i: (0, 0)), row_blk, vec_blk, vec_blk] + side_in,
        out_specs=[row_blk, row_blk] + side_out,
        out_shape=[jax.ShapeDtypeStruct((rows, d), F32), jax.ShapeDtypeStruct((rows, d), BF16)] + side_shape,
        compiler_params=_params("parallel"),
        name="out_proj",
    )(y, w_out, h, post_row, next_row, *side)
    return res[0], res[1], tuple(res[2:])


def _ffn_kernel(x_ref, wg_ref, wu_ref, wd_ref, h_ref, post_ref, nxt_ref, hn_ref, *rest, with_next):
    if with_next:
        xn_ref, acc_ref = rest
    else:
        (acc_ref,) = rest
    f = pl.program_id(1)

    @pl.when(f == 0)
    def _():
        acc_ref[...] = jnp.zeros_like(acc_ref)

    x = x_ref[...]
    g = jnp.dot(x, wg_ref[...], preferred_element_type=F32)
    u = jnp.dot(x, wu_ref[...], preferred_element_type=F32)
    act = (_silu(g) * u).astype(BF16)
    acc_ref[...] += jnp.dot(act, wd_ref[...], preferred_element_type=F32)

    @pl.when(f == pl.num_programs(1) - 1)
    def _():
        hn = h_ref[...] + _rms_scale(acc_ref[...]) * post_ref[...]
        hn_ref[...] = hn
        if with_next:
            xn_ref[...] = (_rms_scale(hn) * nxt_ref[...]).astype(xn_ref.dtype)


def _ffn_call(xn, w_gate, w_up, w_down, h, post_row, next_row, with_next):
    rows, d = h.shape
    d_ff = w_gate.shape[-1]
    tm = min(rows, FFN_ROW_TILE)
    tf = FFN_COL_TILE
    row_blk = pl.BlockSpec((tm, d), lambda i, f: (i, 0))
    vec_blk = pl.BlockSpec((1, d), lambda i, f: (0, 0))
    n_state = 2 if with_next else 1
    res = pl.pallas_call(
        functools.partial(_ffn_kernel, with_next=with_next),
        grid=(rows // tm, d_ff // tf),
        in_specs=[
            row_blk,
            pl.BlockSpec((d, tf), lambda i, f: (0, f)),
            pl.BlockSpec((d, tf), lambda i, f: (0, f)),
            pl.BlockSpec((tf, d), lambda i, f: (f, 0)),
            row_blk, vec_blk, vec_blk,
        ],
        out_specs=[row_blk] * n_state,
        out_shape=[jax.ShapeDtypeStruct((rows, d), F32), jax.ShapeDtypeStruct((rows, d), BF16)][:n_state],
        scratch_shapes=[pltpu.VMEM((tm, d), F32)],
        compiler_params=_params("parallel", "arbitrary"),
        name="swiglu_ffn",
    )(xn, w_gate, w_up, w_down, h, post_row, next_row)
    return res[0], (res[1] if with_next else None)


def kernel(x, meta_tokens, norm_mix_pre, norm_mix_post, norm_ffn_pre, norm_ffn_post, w_in, hg_lb_logits,
           hg_norm_w, w_br_ret, w_br_hg, w_out, w_ffn_gate, w_ffn_up, w_ffn_down):
    batch, seq, d = x.shape
    depth = w_in.shape[0]
    assert seq % CHUNK == 0 and meta_tokens.shape == (N_META, d)
    assert w_in.shape[-1] == sum(_IN_WIDTHS) and d == V_W

    lb_sm = jax.nn.softmax(hg_lb_logits.astype(F32), axis=0)
    lbs = jnp.cumsum(lb_sm, axis=0) - lb_sm[0:1]

    hg_tables = _hg_tables()
    ret_tables = _ret_tables()
    meta_h = jnp.concatenate([jnp.zeros((PAD, d), F32), meta_tokens.astype(F32)], axis=0)
    meta = dict(h=meta_h, batch=1, seq=CHUNK, pos0=-PAD)
    main = dict(h=x.reshape(batch * seq, d).astype(F32), batch=batch, seq=seq, pos0=N_META)
    for rs in (meta, main):
        rs["rot"] = _rot_tables(rs["seq"], rs["pos0"])
        rs["xn"] = _norm_call(rs["h"], norm_mix_pre[0][None])

    for l in range(depth):
        last = l == depth - 1
        proj = functools.partial(_proj_call, main["xn"], meta["xn"], w_in, l, seq=seq)
        main["qk"], meta["qk"], (w_up_b,) = proj(ROT_TILES, "rot", extra=main["rot"], meta_extra=meta["rot"],
                                                side=(w_ffn_up,))
        main["vi"], meta["vi"], (w_ret_b, w_hg_b, w_out_b) = proj(IDENT_TILES, "ident",
                                                                 side=(w_br_ret, w_br_hg, w_out))
        main["act"], meta["act"], _ = proj(SILU_TILES, "silu")
        main["forget"], meta["forget"], _ = proj(FORGET_TILES, "forget", extra=(lbs[l][None],))
        main["gates"], meta["gates"], (w_gate_b,) = _proj_call(
            main["xn"], None if last else meta["xn"], w_in, l, SIGMOID_TILES, "sigmoid", seq, side=(w_ffn_gate,))

        ret_state = jnp.zeros((HEADS, DK, DV), F32)
        hg_state = jnp.zeros((HEADS, DK, DV), F32)
        for rs in (meta, main):
            kin, lf_hi, lf_lo = rs["forget"]
            rs["yr"], ret_fin = _ret_call(rs["qk"], rs["vi"], rs["act"], ret_tables, ret_state,
                                          rs["batch"], rs["seq"], rs["pos0"])
            rs["yh"], hg_fin = _hg_call(rs["act"], kin, lf_hi, lf_lo, rs["vi"], hg_norm_w[l][None], hg_tables,
                                        hg_state, rs["batch"], rs["seq"])
            if rs is meta:
                ret_state, hg_state = ret_fin[0], hg_fin[0]

        next_row = norm_mix_pre[min(l + 1, depth - 1)][None]
        post_row = norm_ffn_post[l][None]
        w_down_b = None
        for rs in (main,) if last else (main, meta):
            y = _merge_call(rs["yr"], rs["yh"], rs["gates"], w_ret_b, w_hg_b)
            side = () if rs is meta else (w_ffn_down,)
            h_mid, xn_ffn, casts = _out_call(y, w_out_b, rs["h"], norm_mix_post[l][None], norm_ffn_pre[l][None],
                                             l, side)
            if rs is main:
                (w_down_b,) = casts
            rs["h"], rs["xn"] = _ffn_call(xn_ffn, w_gate_b, w_up_b, w_down_b, h_mid, post_row, next_row, not last)
    return main["h"].reshape(batch, seq, d)
```

```python
import functools

import numpy as np
import jax
import jax.numpy as jnp
from jax import lax
from jax.experimental import pallas as pl
from jax.experimental.pallas import tpu as pltpu

N_META = 16
HEADS = 8
DK = 128
DV = 256
CHUNK = 128
PAD = CHUNK - N_META
RMS_EPS = 1e-6
ROPE_BASE = 10000.0
LEVELS = 7
SUBLANES = 8
SMALL_LEVELS = 3
BF16_SUBLANES = 16
QK_W = HEADS * DK
V_W = HEADS * DV

VMEM_LIMIT_BYTES = 58 * 1024 * 1024
ROW_TILE = 1024
IN_TILE = 1024
PROJ_SLAB = 512
PROJ_VMEM_BUDGET = 54 * 1024 * 1024
HEADS_PER_STEP = 4
MIXER_ROWS = 2048
MIXER_UNROLL = 2
FFN_ROW_TILE = 512
FFN_COL_TILE = 512
MERGE_COL_TILE = 512
XLU_TRANSPOSE_LEVELS = (4, 5, 6)
HG_GROUP = 4

_IN_WIDTHS = (QK_W, QK_W, V_W, V_W, QK_W, QK_W, V_W, V_W, V_W, V_W)
_IN_STARTS = tuple(sum(_IN_WIDTHS[:i]) // IN_TILE for i in range(len(_IN_WIDTHS)))


def _tiles(*parts):
    return tuple(t for p in parts for t in range(_IN_STARTS[p], _IN_STARTS[p] + _IN_WIDTHS[p] // IN_TILE))


ROT_TILES = _tiles(0, 1)
IDENT_TILES = _tiles(2, 6)
SILU_TILES = _tiles(3, 4, 7)
FORGET_TILES = _tiles(5)
SIGMOID_TILES = _tiles(8, 9)

F32 = jnp.float32
BF16 = jnp.bfloat16
NT_DIMS = (((1,), (1,)), ((), ()))
TN_DIMS = (((0,), (0,)), ((), ()))


def _params(*semantics, flags=None):
    return pltpu.CompilerParams(dimension_semantics=semantics, vmem_limit_bytes=VMEM_LIMIT_BYTES, flags=flags)


MIXER_FLAGS = None


def _sigmoid(x):
    return 0.5 + 0.5 * jnp.tanh(0.5 * x)


def _silu(x):
    h = 0.5 * x
    return h + h * jnp.tanh(h)


def _rms_scale(x):
    return x * lax.rsqrt(jnp.mean(x * x, axis=-1, keepdims=True) + RMS_EPS)


def _tile_lookup(tiles):
    def lookup(j):
        out = tiles[-1]
        for idx in range(len(tiles) - 2, -1, -1):
            out = jnp.where(j == idx, tiles[idx], out)
        return out
    return lookup


def _side_cast_specs(arrays, layer, nsteps, step_of):
    in_specs, out_specs, out_shape = [], [], []
    for a in arrays:
        _, r, c = a.shape
        assert r % (nsteps * BF16_SUBLANES) == 0, (a.shape, nsteps)
        br = r // nsteps
        in_specs.append(pl.BlockSpec((None, br, c), lambda *g: (layer, step_of(*g), 0)))
        out_specs.append(pl.BlockSpec((br, c), lambda *g: (step_of(*g), 0)))
        out_shape.append(jax.ShapeDtypeStruct((r, c), BF16))
    return in_specs, out_specs, out_shape


def _side_cast(side_in, side_out):
    for i_ref, o_ref in zip(side_in, side_out):
        o_ref[...] = i_ref[...].astype(BF16)


def _norm_kernel(h_ref, w_ref, o_ref):
    o_ref[...] = (_rms_scale(h_ref[...]) * w_ref[...]).astype(o_ref.dtype)


def _norm_call(h, w_row):
    rows, d = h.shape
    tm = min(rows, 512)
    return pl.pallas_call(
        _norm_kernel,
        grid=(rows // tm,),
        in_specs=[pl.BlockSpec((tm, d), lambda i: (i, 0)), pl.BlockSpec((1, d), lambda i: (0, 0))],
        out_specs=pl.BlockSpec((tm, d), lambda i: (i, 0)),
        out_shape=jax.ShapeDtypeStruct((rows, d), BF16),
        compiler_params=_params("parallel"),
        name="rms_norm",
    )(h, w_row)


def _proj_epilogue(acc, mode, extras, outs, r, first_pos):
    if mode == "ident":
        outs[0][r, :] = acc.astype(BF16)
    elif mode == "silu":
        outs[0][r, :] = _silu(acc).astype(BF16)
    elif mode == "sigmoid":
        outs[0][r, :] = _sigmoid(acc).astype(BF16)
    elif mode == "rot":
        cos, sin = extras[0][r, :], extras[1][r, :]
        for g in range(IN_TILE // DK):
            xg = acc[:, g * DK:(g + 1) * DK]
            outs[0][r, g * DK:(g + 1) * DK] = (xg * cos + pltpu.roll(xg, DK // 2, 1) * sin).astype(BF16)
    elif mode == "forget":
        kin_ref, hi_ref, lo_ref = outs
        one_minus_f = (1.0 - extras[0][...]) * (0.5 - 0.5 * jnp.tanh(0.5 * acc))
        log_f = jnp.log2(1.0 - one_minus_f)
        if first_pos is not None:
            valid = (lax.broadcasted_iota(jnp.int32, (acc.shape[0], 1), 0) + first_pos) >= 0
            log_f = jnp.where(valid, log_f, 0.0)
            one_minus_f = jnp.where(valid, one_minus_f, 0.0)
        kin_ref[r, :] = one_minus_f.astype(BF16)
        hi = log_f.astype(BF16)
        hi_ref[r, :] = hi
        lo_ref[r, :] = (log_f - hi.astype(F32)).astype(BF16)


def _proj_kernel(*refs, mode, with_meta, n_side):
    n_extra = {"rot": 2, "forget": 1}.get(mode, 0)
    n_out = 3 if mode == "forget" else 1
    x_ref, w_ref = refs[:2]
    extras = refs[2:2 + n_extra]
    pos = 2 + n_extra
    if with_meta:
        xm_ref = refs[pos]
        n_meta_extra = 2 if mode == "rot" else 0
        meta_extras = refs[pos + 1:pos + 1 + n_meta_extra] if n_meta_extra else extras
        pos += 1 + n_meta_extra
    side_in = refs[pos:pos + n_side]
    pos += n_side
    outs = refs[pos:pos + n_out]
    pos += n_out
    if with_meta:
        meta_outs = refs[pos:pos + n_out]
        pos += n_out
    _side_cast(side_in, refs[pos:pos + n_side])
    wb_ref = refs[-1]

    @pl.when(pl.program_id(1) == 0)
    def _():
        wb_ref[...] = w_ref[...].astype(BF16)
        if with_meta:
            acc = jnp.dot(xm_ref[...], wb_ref[...], preferred_element_type=F32)
            _proj_epilogue(acc, mode, meta_extras, meta_outs, slice(None), -PAD)

    tm = x_ref.shape[0]
    slab = min(tm, PROJ_SLAB)
    pending = None
    for s0 in range(0, tm, slab):
        r = slice(s0, s0 + slab)
        acc = jnp.dot(x_ref[r, :], wb_ref[...], preferred_element_type=F32)
        if pending is not None:
            _proj_epilogue(*pending)
        pending = (acc, mode, extras, outs, r, None)
    _proj_epilogue(*pending)


def _proj_row_tile(rows, k, n_out, side_bytes):
    for tm in (2 * ROW_TILE, ROW_TILE):
        if tm > rows or rows % tm:
            continue
        x_bytes = 2 * tm * k * 2
        w_bytes = 2 * k * IN_TILE * 4 + k * IN_TILE * 2
        out_bytes = n_out * 2 * tm * IN_TILE * 2
        acc_bytes = 2 * PROJ_SLAB * IN_TILE * 4
        meta_bytes = 2 * CHUNK * (k + n_out * IN_TILE) * 2
        side = 2 * (side_bytes + side_bytes // 2) // (rows // tm)
        if x_bytes + w_bytes + out_bytes + acc_bytes + meta_bytes + side <= PROJ_VMEM_BUDGET:
            return tm
    return min(rows, ROW_TILE)


def _proj_call(xn, xn_meta, w_in, layer, tiles, mode, seq, extra=(), meta_extra=(), side=()):
    rows, k = xn.shape
    n_out = 3 if mode == "forget" else 1
    with_meta = xn_meta is not None
    side_bytes = sum(a.shape[1] * a.shape[2] * 4 for a in side) // len(tiles)
    tm = _proj_row_tile(rows, k, n_out, side_bytes)
    n_row_tiles = rows // tm
    lookup = _tile_lookup(tiles)
    out_cols = len(tiles) * IN_TILE
    in_specs = [pl.BlockSpec((tm, k), lambda j, i: (i, 0)),
                pl.BlockSpec((None, k, IN_TILE), lambda j, i: (layer, 0, lookup(j)))]
    if mode == "rot":
        blocks_per_seq = seq // tm
        tab = pl.BlockSpec((None, tm, DK), lambda j, i: (j, i % blocks_per_seq, 0))
        in_specs += [tab, tab]
    elif mode == "forget":
        in_specs += [pl.BlockSpec((1, IN_TILE), lambda j, i: (0, 0))]
    operands = [xn, w_in, *extra]
    out_specs = [pl.BlockSpec((tm, IN_TILE), lambda j, i: (i, j))] * n_out
    out_shape = [jax.ShapeDtypeStruct((rows, out_cols), BF16)] * n_out
    if with_meta:
        in_specs += [pl.BlockSpec((CHUNK, k), lambda j, i: (0, 0))]
        operands += [xn_meta]
        if mode == "rot":
            in_specs += [pl.BlockSpec((None, CHUNK, DK), lambda j, i: (j, 0, 0))] * 2
            operands += list(meta_extra)
        out_specs += [pl.BlockSpec((CHUNK, IN_TILE), lambda j, i: (0, j))] * n_out
        out_shape += [jax.ShapeDtypeStruct((CHUNK, out_cols), BF16)] * n_out
    side_in, side_out, side_shape = _side_cast_specs(side, layer, len(tiles) * n_row_tiles,
                                                     lambda j, i: j * n_row_tiles + i)
    res = pl.pallas_call(
        functools.partial(_proj_kernel, mode=mode, with_meta=with_meta, n_side=len(side)),
        grid=(len(tiles), n_row_tiles),
        in_specs=in_specs + side_in,
        out_specs=out_specs + side_out,
        out_shape=out_shape + side_shape,
        scratch_shapes=[pltpu.VMEM((k, IN_TILE), BF16)],
        compiler_params=_params("parallel", "arbitrary"),
        name="in_proj_" + mode,
    )(*operands, *side)
    main = res[:n_out]
    meta = res[n_out:2 * n_out] if with_meta else None
    casts = tuple(res[len(res) - len(side):]) if side else ()
    if n_out == 1:
        main, meta = main[0], (meta[0] if with_meta else None)
    return main, meta, casts


def _rot_tables(seq, pos0):
    half = DK // 2
    inv = ROPE_BASE ** (-jnp.arange(half, dtype=F32) / half)
    pos = jnp.arange(seq, dtype=jnp.int32) + pos0
    ang = pos.astype(F32)[:, None] * inv[None, :]
    cos, sin = jnp.cos(ang), jnp.sin(ang)
    cos2 = jnp.concatenate([cos, cos], axis=1)
    sin2 = jnp.concatenate([-sin, sin], axis=1)
    scale = DK ** -0.5
    return jnp.stack([cos2, cos2 * scale]), jnp.stack([sin2, sin2 * scale])


def _ret_kernel(q_ref, k_ref, v_ref, g_ref, dmat_ref, qdec_ref, kdec_ref, cdec_ref, s0_ref,
                o_ref, sfin_ref, s_ref, *, hb, nchunks, pos0):
    @pl.when(pl.program_id(2) == 0)
    def _():
        s_ref[...] = s0_ref[...]

    def body(c, carry):
        r0 = pl.multiple_of(c * CHUNK, CHUNK)
        rows = pl.ds(r0, CHUNK)
        if pos0 < 0:
            valid = (lax.broadcasted_iota(jnp.int32, (CHUNK, 1), 0) + (r0 + pos0)) >= 0
        heads = range(hb)
        q = [q_ref[rows, j * DK:(j + 1) * DK] for j in heads]
        k = [k_ref[rows, j * DK:(j + 1) * DK] for j in heads]
        if pos0 < 0:
            k = [jnp.where(valid, kj, jnp.zeros_like(kj)) for kj in k]
        v = [v_ref[rows, j * DV:(j + 1) * DV] for j in heads]
        scores = [jnp.dot(q[j], k[j].astype(F32).T.astype(BF16), preferred_element_type=F32) for j in heads]
        s = [s_ref[j] for j in heads]
        new_s = [lax.dot_general(k[j] * kdec_ref[j], v[j], TN_DIMS, preferred_element_type=F32) for j in heads]
        y = [jnp.dot(jnp.concatenate([(scores[j] * dmat_ref[j]).astype(BF16), q[j] * qdec_ref[j]], axis=1),
                     jnp.concatenate([v[j], s[j].astype(BF16)], axis=0), preferred_element_type=F32)
             for j in heads]
        for j in heads:
            s_ref[j] = s[j] * cdec_ref[j] + new_s[j]
            g = g_ref[rows, j * DV:(j + 1) * DV].astype(F32)
            o_ref[rows, j * DV:(j + 1) * DV] = (_rms_scale(y[j]) * g).astype(o_ref.dtype)
        return carry

    lax.fori_loop(0, nchunks, body, 0, unroll=MIXER_UNROLL)
    sfin_ref[...] = s_ref[...]


def _ret_tables():
    log_g = jnp.log1p(-jnp.exp2(-5.0 - jnp.arange(HEADS, dtype=F32)))
    idx = jnp.arange(CHUNK, dtype=F32)
    diff = idx[:, None] - idx[None, :]
    dmat = jnp.where(diff[None] >= 0, jnp.exp(jnp.maximum(diff, 0.0)[None] * log_g[:, None, None]), 0.0)
    qdec = jnp.exp((idx + 1)[None, :] * log_g[:, None])
    kdec = jnp.exp((CHUNK - 1 - idx)[None, :] * log_g[:, None])
    cdec = jnp.exp(CHUNK * log_g)
    qdec = jnp.broadcast_to(qdec[:, :, None], (HEADS, CHUNK, DK)).astype(BF16)
    kdec = jnp.broadcast_to(kdec[:, :, None], (HEADS, CHUNK, DK)).astype(BF16)
    cdec = jnp.broadcast_to(cdec[:, None, None], (HEADS, 1, DV))
    return dmat, qdec, kdec, cdec


def _mixer_grid(batch, seq):
    rb = min(seq, MIXER_ROWS)
    return rb, seq // rb


def _ret_call(qk, vi, act, tables, s0, batch, seq, pos0):
    dmat, qdec, kdec, cdec = tables
    hb = HEADS_PER_STEP
    rb, nblk = _mixer_grid(batch, seq)
    k_blk = QK_W // (hb * DK)
    kern = functools.partial(_ret_kernel, hb=hb, nchunks=rb // CHUNK, pos0=pos0)
    per_head = lambda b, h, r: (h, 0, 0)
    return pl.pallas_call(
        kern,
        grid=(batch, HEADS // hb, nblk),
        in_specs=[
            pl.BlockSpec((rb, hb * DK), lambda b, h, r: (b * nblk + r, h)),
            pl.BlockSpec((rb, hb * DK), lambda b, h, r: (b * nblk + r, k_blk + h)),
            pl.BlockSpec((rb, hb * DV), lambda b, h, r: (b * nblk + r, h)),
            pl.BlockSpec((rb, hb * DV), lambda b, h, r: (b * nblk + r, h)),
            pl.BlockSpec((hb, CHUNK, CHUNK), per_head),
            pl.BlockSpec((hb, CHUNK, DK), per_head),
            pl.BlockSpec((hb, CHUNK, DK), per_head),
            pl.BlockSpec((hb, 1, DV), per_head),
            pl.BlockSpec((hb, DK, DV), per_head),
        ],
        out_specs=[
            pl.BlockSpec((rb, hb * DV), lambda b, h, r: (b * nblk + r, h)),
            pl.BlockSpec((None, hb, DK, DV), lambda b, h, r: (b, h, 0, 0)),
        ],
        out_shape=[jax.ShapeDtypeStruct((batch * seq, V_W), BF16),
                   jax.ShapeDtypeStruct((batch, HEADS, DK, DV), F32)],
        scratch_shapes=[pltpu.VMEM((hb, DK, DV), F32)],
        compiler_params=_params("parallel", "parallel", "arbitrary", flags=MIXER_FLAGS),
        name="retention",
    )(qk, qk, vi, act, dmat, qdec, kdec, cdec, s0)


def _hg_tables():
    t = np.arange(CHUNK)[:, None]
    u = np.arange(CHUNK)[None, :]
    mats, masks = [], []
    for j in range(LEVELS):
        m = 1 << j
        upper = ((t >> j) & 1) == 1
        q_part = upper & (u >= (t & ~(m - 1))) & (u <= t)
        k_part = (~upper) & (u > t) & (u <= (t | (m - 1)))
        if m < SUBLANES:
            mats.append(q_part | k_part)
        masks.append(((t >> (j + 1)) == (u >> (j + 1))) & upper & (((u >> j) & 1) == 0))
    mats.append(u <= t)
    masks.append(t == u)
    mstack = np.concatenate(mats, axis=0).astype(np.float32)
    mstack = np.concatenate([mstack, mstack], axis=1)
    return jnp.asarray(mstack, dtype=BF16), jnp.asarray(np.stack(masks).astype(np.float32))


def _dot_keys(queries, keys, level):
    if level in XLU_TRANSPOSE_LEVELS:
        return jnp.dot(queries, keys.T.astype(BF16), preferred_element_type=F32)
    return lax.dot_general(queries, keys.astype(BF16), NT_DIMS, preferred_element_type=F32)


def _hg_kernel(q_ref, kin_ref, hi_ref, lo_ref, v_ref, g_ref, nw_ref, mstack_ref, masks_ref, s0_ref,
               o_ref, sfin_ref, st_ref, e_ref, *, hb, nchunks):
    @pl.when(pl.program_id(2) == 0)
    def _():
        st_ref[...] = s0_ref[...]

    row = lax.broadcasted_iota(jnp.int32, (CHUNK, 1), 0)

    def exponents(c, slot, pair):
        rows = pl.ds(pl.multiple_of(c * CHUNK, CHUNK), CHUNK)
        lanes = slice(pair * 2 * DK, (pair + 1) * 2 * DK)
        pieces = jnp.concatenate([hi_ref[rows, lanes], lo_ref[rows, lanes]], axis=0)
        e_ref[slot, :, lanes] = jnp.dot(mstack_ref[...], pieces, preferred_element_type=F32)

    def process(c, slot, nxt):
        for g0 in range(0, hb, HG_GROUP):
            process_group(c, slot, nxt, range(g0, g0 + HG_GROUP))

    def process_group(c, slot, nxt, heads):
        rows = pl.ds(pl.multiple_of(c * CHUNK, CHUNK), CHUNK)
        expo = {j: e_ref.at[slot, :, j * DK:(j + 1) * DK] for j in heads}
        qb = {j: q_ref[rows, j * DK:(j + 1) * DK] for j in heads}
        q = {j: qb[j].astype(F32) for j in heads}
        kin = {j: kin_ref[rows, j * DK:(j + 1) * DK].astype(F32) for j in heads}
        b = {j: expo[j][SMALL_LEVELS * CHUNK:, :] for j in heads}
        a = {j: _dot_keys(qb[j], kin[j], LEVELS) * masks_ref[LEVELS] for j in heads}
        a_rows = None
        issue_at = {1 + 3 * i: p for i, p in enumerate(range(heads[0] // 2, heads[-1] // 2 + 1))}
        for lev in range(LEVELS):
            m = 1 << lev
            if nxt is not None and lev in issue_at:
                exponents(nxt, 1 - slot, issue_at[lev])
            if m < SUBLANES:
                for j in heads:
                    e = jnp.exp2(expo[j][lev * CHUNK:(lev + 1) * CHUNK, :])
                    x = jnp.where(((row >> lev) & 1) == 1, q[j], kin[j]) * e
                    a[j] = a[j] + _dot_keys(x.astype(BF16), x, lev) * masks_ref[lev]
            else:
                if a_rows is None:
                    a_rows = {j: [a[j][s:s + SUBLANES] for s in range(0, CHUNK, SUBLANES)] for j in heads}
                upper = [s for s0 in range(m, CHUNK, 2 * m) for s in range(s0, s0 + m, SUBLANES)]
                for j in heads:
                    parts = []
                    for s0 in range(0, CHUNK, 2 * m):
                        b_mid = jnp.broadcast_to(b[j][s0 + m - 1:s0 + m, :], (m, DK))
                        parts += [b_mid - b[j][s0:s0 + m], b[j][s0 + m:s0 + 2 * m] - b_mid]
                    e = jnp.exp2(jnp.concatenate(parts, axis=0))
                    qe = jnp.concatenate([q[j][s:s + SUBLANES] * e[s:s + SUBLANES] for s in upper], axis=0)
                    p = _dot_keys(qe.astype(BF16), kin[j] * e, lev)
                    for idx, s in enumerate(upper):
                        a_rows[j][s // SUBLANES] = (a_rows[j][s // SUBLANES]
                                                    + p[idx * SUBLANES:(idx + 1) * SUBLANES]
                                                    * masks_ref[lev, s:s + SUBLANES, :])
        total = {j: b[j][CHUNK - 1:CHUNK, :] for j in heads}
        v = {j: v_ref[rows, j * DV:(j + 1) * DV] for j in heads}
        st = {j: st_ref[j] for j in heads}
        new_st = {j: lax.dot_general((kin[j] * jnp.exp2(total[j] - b[j])).astype(BF16), v[j], TN_DIMS,
                                     preferred_element_type=F32) for j in heads}
        y = {j: jnp.dot(jnp.concatenate([jnp.concatenate(a_rows[j], axis=0).astype(BF16),
                                         (q[j] * jnp.exp2(b[j])).astype(BF16)], axis=1),
                        jnp.concatenate([v[j], st[j].astype(BF16)], axis=0), preferred_element_type=F32)
             for j in heads}
        for j in heads:
            decay = jnp.broadcast_to(jnp.exp2(total[j]), (CHUNK, DK)).T
            st_ref[j] = st[j] * jnp.concatenate([decay] * (DV // DK), axis=1) + new_st[j]
            g = g_ref[rows, j * DV:(j + 1) * DV].astype(F32)
            o_ref[rows, j * DV:(j + 1) * DV] = (_rms_scale(y[j]) * nw_ref[...] * g).astype(o_ref.dtype)

    for pr in range(hb // 2):
        exponents(0, 0, pr)
    if nchunks == 1:
        process(0, 0, None)
    else:
        assert nchunks % 2 == 0

        def pair(p, carry):
            c = 2 * p
            process(c, 0, c + 1)
            process(c + 1, 1, jnp.minimum(c + 2, nchunks - 1))
            return carry

        lax.fori_loop(0, nchunks // 2, pair, 0)
    sfin_ref[...] = st_ref[...]


def _hg_call(act, kin, lf_hi, lf_lo, vi, nw_row, tables, s0, batch, seq):
    mstack, masks = tables
    hb = HEADS_PER_STEP
    rb, nblk = _mixer_grid(batch, seq)
    q_blk = V_W // (hb * DK)
    g_blk = (V_W + QK_W) // (hb * DV)
    i_blk = V_W // (hb * DV)
    kern = functools.partial(_hg_kernel, hb=hb, nchunks=rb // CHUNK)
    row_blk = lambda off: (lambda b, h, r: (b * nblk + r, off + h))
    return pl.pallas_call(
        kern,
        grid=(batch, HEADS // hb, nblk),
        in_specs=[
            pl.BlockSpec((rb, hb * DK), row_blk(q_blk)),
            pl.BlockSpec((rb, hb * DK), row_blk(0)),
            pl.BlockSpec((rb, hb * DK), row_blk(0)),
            pl.BlockSpec((rb, hb * DK), row_blk(0)),
            pl.BlockSpec((rb, hb * DV), row_blk(i_blk)),
            pl.BlockSpec((rb, hb * DV), row_blk(g_blk)),
            pl.BlockSpec((1, DV), lambda b, h, r: (0, 0)),
            pl.BlockSpec(mstack.shape, lambda b, h, r: (0, 0)),
            pl.BlockSpec(masks.shape, lambda b, h, r: (0, 0, 0)),
            pl.BlockSpec((hb, DK, DV), lambda b, h, r: (h, 0, 0)),
        ],
        out_specs=[
            pl.BlockSpec((rb, hb * DV), row_blk(0)),
            pl.BlockSpec((None, hb, DK, DV), lambda b, h, r: (b, h, 0, 0)),
        ],
        out_shape=[jax.ShapeDtypeStruct((batch * seq, V_W), BF16),
                   jax.ShapeDtypeStruct((batch, HEADS, DK, DV), F32)],
        scratch_shapes=[pltpu.VMEM((hb, DK, DV), F32),
                        pltpu.VMEM((2, (SMALL_LEVELS + 1) * CHUNK, hb * DK), F32)],
        compiler_params=_params("parallel", "parallel", "arbitrary", flags=MIXER_FLAGS),
        name="hgrn2",
    )(act, kin, lf_hi, lf_lo, vi, act, nw_row, mstack, masks, s0)


def _merge_kernel(yr_ref, yh_ref, wr_ref, wh_ref, gr_ref, gh_ref, o_ref):
    a = jnp.dot(yr_ref[...], wr_ref[...], preferred_element_type=F32)
    b = jnp.dot(yh_ref[...], wh_ref[...], preferred_element_type=F32)
    o_ref[...] = (gr_ref[...].astype(F32) * a + gh_ref[...].astype(F32) * b).astype(o_ref.dtype)


def _merge_call(yr, yh, gates, w_ret, w_hg):
    rows, k = yr.shape
    d = w_ret.shape[-1]
    tm = min(rows, ROW_TILE)
    tn = MERGE_COL_TILE
    gh_blk = d // tn
    return pl.pallas_call(
        _merge_kernel,
        grid=(rows // tm, d // tn),
        in_specs=[
            pl.BlockSpec((tm, k), lambda i, j: (i, 0)),
            pl.BlockSpec((tm, k), lambda i, j: (i, 0)),
            pl.BlockSpec((k, tn), lambda i, j: (0, j)),
            pl.BlockSpec((k, tn), lambda i, j: (0, j)),
            pl.BlockSpec((tm, tn), lambda i, j: (i, j)),
            pl.BlockSpec((tm, tn), lambda i, j: (i, gh_blk + j)),
        ],
        out_specs=pl.BlockSpec((tm, tn), lambda i, j: (i, j)),
        out_shape=jax.ShapeDtypeStruct((rows, d), BF16),
        compiler_params=_params("parallel", "arbitrary"),
        name="branch_merge",
    )(yr, yh, w_ret, w_hg, gates, gates)


def _out_kernel(y_ref, w_ref, h_ref, post_ref, nxt_ref, *rest, n_side):
    hn_ref, xn_ref = rest[n_side:n_side + 2]
    _side_cast(rest[:n_side], rest[n_side + 2:])
    m = jnp.dot(y_ref[...], w_ref[...], preferred_element_type=F32)
    hn = h_ref[...] + _rms_scale(m) * post_ref[...]
    hn_ref[...] = hn
    xn_ref[...] = (_rms_scale(hn) * nxt_ref[...]).astype(xn_ref.dtype)


def _out_call(y, w_out, h, post_row, next_row, layer=None, side=()):
    rows, d = h.shape
    tm = min(rows, FFN_ROW_TILE)
    row_blk = pl.BlockSpec((tm, d), lambda i: (i, 0))
    vec_blk = pl.BlockSpec((1, d), lambda i: (0, 0))
    side_in, side_out, side_shape = _side_cast_specs(side, layer, rows // tm, lambda i: i)
    res = pl.pallas_call(
        functools.partial(_out_kernel, n_side=len(side)),
        grid=(rows // tm,),
        in_specs=[row_blk, pl.BlockSpec((d, d), lambda i: (0, 0)), row_blk, vec_blk, vec_blk] + side_in,
        out_specs=[row_blk, row_blk] + side_out,
        out_shape=[jax.ShapeDtypeStruct((rows, d), F32), jax.ShapeDtypeStruct((rows, d), BF16)] + side_shape,
        compiler_params=_params("parallel"),
        name="out_proj",
    )(y, w_out, h, post_row, next_row, *side)
    return res[0], res[1], tuple(res[2:])


def _ffn_kernel(x_ref, wg_ref, wu_ref, wd_ref, h_ref, post_ref, nxt_ref, hn_ref, *rest, with_next):
    if with_next:
        xn_ref, acc_ref = rest
    else:
        (acc_ref,) = rest
    f = pl.program_id(1)

    @pl.when(f == 0)
    def _():
        acc_ref[...] = jnp.zeros_like(acc_ref)

    x = x_ref[...]
    g = jnp.dot(x, wg_ref[...], preferred_element_type=F32)
    u = jnp.dot(x, wu_ref[...], preferred_element_type=F32)
    act = (_silu(g) * u).astype(BF16)
    acc_ref[...] += jnp.dot(act, wd_ref[...], preferred_element_type=F32)

    @pl.when(f == pl.num_programs(1) - 1)
    def _():
        hn = h_ref[...] + _rms_scale(acc_ref[...]) * post_ref[...]
        hn_ref[...] = hn
        if with_next:
            xn_ref[...] = (_rms_scale(hn) * nxt_ref[...]).astype(xn_ref.dtype)


def _ffn_call(xn, w_gate, w_up, w_down, h, post_row, next_row, with_next):
    rows, d = h.shape
    d_ff = w_gate.shape[-1]
    tm = min(rows, FFN_ROW_TILE)
    tf = FFN_COL_TILE
    row_blk = pl.BlockSpec((tm, d), lambda i, f: (i, 0))
    vec_blk = pl.BlockSpec((1, d), lambda i, f: (0, 0))
    n_state = 2 if with_next else 1
    res = pl.pallas_call(
        functools.partial(_ffn_kernel, with_next=with_next),
        grid=(rows // tm, d_ff // tf),
        in_specs=[
            row_blk,
            pl.BlockSpec((d, tf), lambda i, f: (0, f)),
            pl.BlockSpec((d, tf), lambda i, f: (0, f)),
            pl.BlockSpec((tf, d), lambda i, f: (f, 0)),
            row_blk, vec_blk, vec_blk,
        ],
        out_specs=[row_blk] * n_state,
        out_shape=[jax.ShapeDtypeStruct((rows, d), F32), jax.ShapeDtypeStruct((rows, d), BF16)][:n_state],
        scratch_shapes=[pltpu.VMEM((tm, d), F32)],
        compiler_params=_params("parallel", "arbitrary"),
        name="swiglu_ffn",
    )(xn, w_gate, w_up, w_down, h, post_row, next_row)
    return res[0], (res[1] if with_next else None)


def kernel(x, meta_tokens, norm_mix_pre, norm_mix_post, norm_ffn_pre, norm_ffn_post, w_in, hg_lb_logits,
           hg_norm_w, w_br_ret, w_br_hg, w_out, w_ffn_gate, w_ffn_up, w_ffn_down):
    batch, seq, d = x.shape
    depth = w_in.shape[0]
    assert seq % CHUNK == 0 and meta_tokens.shape == (N_META, d)
    assert w_in.shape[-1] == sum(_IN_WIDTHS) and d == V_W

    lb_sm = jax.nn.softmax(hg_lb_logits.astype(F32), axis=0)
    lbs = jnp.cumsum(lb_sm, axis=0) - lb_sm[0:1]

    hg_tables = _hg_tables()
    ret_tables = _ret_tables()
    meta_h = jnp.concatenate([jnp.zeros((PAD, d), F32), meta_tokens.astype(F32)], axis=0)
    meta = dict(h=meta_h, batch=1, seq=CHUNK, pos0=-PAD)
    main = dict(h=x.reshape(batch * seq, d).astype(F32), batch=batch, seq=seq, pos0=N_META)
    for rs in (meta, main):
        rs["rot"] = _rot_tables(rs["seq"], rs["pos0"])
        rs["xn"] = _norm_call(rs["h"], norm_mix_pre[0][None])

    for l in range(depth):
        last = l == depth - 1
        proj = functools.partial(_proj_call, main["xn"], meta["xn"], w_in, l, seq=seq)
        main["qk"], meta["qk"], (w_up_b,) = proj(ROT_TILES, "rot", extra=main["rot"], meta_extra=meta["rot"],
                                                side=(w_ffn_up,))
        main["vi"], meta["vi"], (w_ret_b, w_hg_b, w_out_b) = proj(IDENT_TILES, "ident",
                                                                 side=(w_br_ret, w_br_hg, w_out))
        main["act"], meta["act"], _ = proj(SILU_TILES, "silu")
        main["forget"], meta["forget"], _ = proj(FORGET_TILES, "forget", extra=(lbs[l][None],))
        main["gates"], meta["gates"], (w_gate_b,) = _proj_call(
            main["xn"], None if last else meta["xn"], w_in, l, SIGMOID_TILES, "sigmoid", seq, side=(w_ffn_gate,))

        ret_state = jnp.zeros((HEADS, DK, DV), F32)
        hg_state = jnp.zeros((HEADS, DK, DV), F32)
        for rs in (meta, main):
            kin, lf_hi, lf_lo = rs["forget"]
            rs["yr"], ret_fin = _ret_call(rs["qk"], rs["vi"], rs["act"], ret_tables, ret_state,
                                          rs["batch"], rs["seq"], rs["pos0"])
            rs["yh"], hg_fin = _hg_call(rs["act"], kin, lf_hi, lf_lo, rs["vi"], hg_norm_w[l][None], hg_tables,
                                        hg_state, rs["batch"], rs["seq"])
            if rs is meta:
                ret_state, hg_state = ret_fin[0], hg_fin[0]

        next_row = norm_mix_pre[min(l + 1, depth - 1)][None]
        post_row = norm_ffn_post[l][None]
        w_down_b = None
        for rs in (main,) if last else (main, meta):
            y = _merge_call(rs["yr"], rs["yh"], rs["gates"], w_ret_b, w_hg_b)
            side = () if rs is meta else (w_ffn_down,)
            h_mid, xn_ffn, casts = _out_call(y, w_out_b, rs["h"], norm_mix_post[l][None], norm_ffn_pre[l][None],
                                             l, side)
            if rs is main:
                (w_down_b,) = casts
            rs["h"], rs["xn"] = _ffn_call(xn_ffn, w_gate_b, w_up_b, w_down_b, h_mid, post_row, next_row, not last)
    return main["h"].reshape(batch, seq, d)
```

```python
import functools

import numpy as np
import jax
import jax.numpy as jnp
from jax import lax
from jax.experimental import pallas as pl
from jax.experimental.pallas import tpu as pltpu

N_META = 16
HEADS = 8
DK = 128
DV = 256
CHUNK = 128
PAD = CHUNK - N_META
RMS_EPS = 1e-6
ROPE_BASE = 10000.0
LEVELS = 7
SUBLANES = 8
SMALL_LEVELS = 3
BF16_SUBLANES = 16
QK_W = HEADS * DK
V_W = HEADS * DV

VMEM_LIMIT_BYTES = 58 * 1024 * 1024
ROW_TILE = 1024
IN_TILE = 1024
PROJ_SLAB = 512
PROJ_VMEM_BUDGET = 54 * 1024 * 1024
HEADS_PER_STEP = 4
MIXER_ROWS = 2048
MIXER_UNROLL = 2
FFN_ROW_TILE = 512
FFN_COL_TILE = 512
MERGE_COL_TILE = 512
XLU_TRANSPOSE_LEVELS = (4, 5, 6)
HG_GROUP = 4

_IN_WIDTHS = (QK_W, QK_W, V_W, V_W, QK_W, QK_W, V_W, V_W, V_W, V_W)
_IN_STARTS = tuple(sum(_IN_WIDTHS[:i]) // IN_TILE for i in range(len(_IN_WIDTHS)))


def _tiles(*parts):
    return tuple(t for p in parts for t in range(_IN_STARTS[p], _IN_STARTS[p] + _IN_WIDTHS[p] // IN_TILE))


ROT_TILES = _tiles(0, 1)
IDENT_TILES = _tiles(2, 6)
SILU_TILES = _tiles(3, 4, 7)
FORGET_TILES = _tiles(5)
SIGMOID_TILES = _tiles(8, 9)

F32 = jnp.float32
BF16 = jnp.bfloat16
NT_DIMS = (((1,), (1,)), ((), ()))
TN_DIMS = (((0,), (0,)), ((), ()))


def _params(*semantics, flags=None):
    return pltpu.CompilerParams(dimension_semantics=semantics, vmem_limit_bytes=VMEM_LIMIT_BYTES, flags=flags)


MIXER_FLAGS = None


def _sigmoid(x):
    return 0.5 + 0.5 * jnp.tanh(0.5 * x)


def _silu(x):
    h = 0.5 * x
    return h + h * jnp.tanh(h)


def _rms_scale(x):
    return x * lax.rsqrt(jnp.mean(x * x, axis=-1, keepdims=True) + RMS_EPS)


def _tile_lookup(tiles):
    def lookup(j):
        out = tiles[-1]
        for idx in range(len(tiles) - 2, -1, -1):
            out = jnp.where(j == idx, tiles[idx], out)
        return out
    return lookup


def _side_cast_specs(arrays, layer, nsteps, step_of):
    in_specs, out_specs, out_shape = [], [], []
    for a in arrays:
        _, r, c = a.shape
        assert r % (nsteps * BF16_SUBLANES) == 0, (a.shape, nsteps)
        br = r // nsteps
        in_specs.append(pl.BlockSpec((None, br, c), lambda *g: (layer, step_of(*g), 0)))
        out_specs.append(pl.BlockSpec((br, c), lambda *g: (step_of(*g), 0)))
        out_shape.append(jax.ShapeDtypeStruct((r, c), BF16))
    return in_specs, out_specs, out_shape


def _side_cast(side_in, side_out):
    for i_ref, o_ref in zip(side_in, side_out):
        o_ref[...] = i_ref[...].astype(BF16)


def _norm_kernel(h_ref, w_ref, o_ref):
    o_ref[...] = (_rms_scale(h_ref[...]) * w_ref[...]).astype(o_ref.dtype)


def _norm_call(h, w_row):
    rows, d = h.shape
    tm = min(rows, 512)
    return pl.pallas_call(
        _norm_kernel,
        grid=(rows // tm,),
        in_specs=[pl.BlockSpec((tm, d), lambda i: (i, 0)), pl.BlockSpec((1, d), lambda i: (0, 0))],
        out_specs=pl.BlockSpec((tm, d), lambda i: (i, 0)),
        out_shape=jax.ShapeDtypeStruct((rows, d), BF16),
        compiler_params=_params("parallel"),
        name="rms_norm",
    )(h, w_row)


def _proj_epilogue(acc, mode, extras, outs, r, first_pos):
    if mode == "ident":
        outs[0][r, :] = acc.astype(BF16)
    elif mode == "silu":
        outs[0][r, :] = _silu(acc).astype(BF16)
    elif mode == "sigmoid":
        outs[0][r, :] = _sigmoid(acc).astype(BF16)
    elif mode == "rot":
        cos, sin = extras[0][r, :], extras[1][r, :]
        for g in range(IN_TILE // DK):
            xg = acc[:, g * DK:(g + 1) * DK]
            outs[0][r, g * DK:(g + 1) * DK] = (xg * cos + pltpu.roll(xg, DK // 2, 1) * sin).astype(BF16)
    elif mode == "forget":
        kin_ref, hi_ref, lo_ref = outs
        one_minus_f = (1.0 - extras[0][...]) * (0.5 - 0.5 * jnp.tanh(0.5 * acc))
        log_f = jnp.log2(1.0 - one_minus_f)
        if first_pos is not None:
            valid = (lax.broadcasted_iota(jnp.int32, (acc.shape[0], 1), 0) + first_pos) >= 0
            log_f = jnp.where(valid, log_f, 0.0)
            one_minus_f = jnp.where(valid, one_minus_f, 0.0)
        kin_ref[r, :] = one_minus_f.astype(BF16)
        hi = log_f.astype(BF16)
        hi_ref[r, :] = hi
        lo_ref[r, :] = (log_f - hi.astype(F32)).astype(BF16)


def _proj_kernel(*refs, mode, with_meta, n_side):
    n_extra = {"rot": 2, "forget": 1}.get(mode, 0)
    n_out = 3 if mode == "forget" else 1
    x_ref, w_ref = refs[:2]
    extras = refs[2:2 + n_extra]
    pos = 2 + n_extra
    if with_meta:
        xm_ref = refs[pos]
        n_meta_extra = 2 if mode == "rot" else 0
        meta_extras = refs[pos + 1:pos + 1 + n_meta_extra] if n_meta_extra else extras
        pos += 1 + n_meta_extra
    side_in = refs[pos:pos + n_side]
    pos += n_side
    outs = refs[pos:pos + n_out]
    pos += n_out
    if with_meta:
        meta_outs = refs[pos:pos + n_out]
        pos += n_out
    _side_cast(side_in, refs[pos:pos + n_side])
    wb_ref = refs[-1]

    @pl.when(pl.program_id(1) == 0)
    def _():
        wb_ref[...] = w_ref[...].astype(BF16)
        if with_meta:
            acc = jnp.dot(xm_ref[...], wb_ref[...], preferred_element_type=F32)
            _proj_epilogue(acc, mode, meta_extras, meta_outs, slice(None), -PAD)

    tm = x_ref.shape[0]
    slab = min(tm, PROJ_SLAB)
    pending = None
    for s0 in range(0, tm, slab):
        r = slice(s0, s0 + slab)
        acc = jnp.dot(x_ref[r, :], wb_ref[...], preferred_element_type=F32)
        if pending is not None:
            _proj_epilogue(*pending)
        pending = (acc, mode, extras, outs, r, None)
    _proj_epilogue(*pending)


def _proj_row_tile(rows, k, n_out, side_bytes):
    for tm in (2 * ROW_TILE, ROW_TILE):
        if tm > rows or rows % tm:
            continue
        x_bytes = 2 * tm * k * 2
        w_bytes = 2 * k * IN_TILE * 4 + k * IN_TILE * 2
        out_bytes = n_out * 2 * tm * IN_TILE * 2
        acc_bytes = 2 * PROJ_SLAB * IN_TILE * 4
        meta_bytes = 2 * CHUNK * (k + n_out * IN_TILE) * 2
        side = 2 * (side_bytes + side_bytes // 2) // (rows // tm)
        if x_bytes + w_bytes + out_bytes + acc_bytes + meta_bytes + side <= PROJ_VMEM_BUDGET:
            return tm
    return min(rows, ROW_TILE)


def _proj_call(xn, xn_meta, w_in, layer, tiles, mode, seq, extra=(), meta_extra=(), side=()):
    rows, k = xn.shape
    n_out = 3 if mode == "forget" else 1
    with_meta = xn_meta is not None
    side_bytes = sum(a.shape[1] * a.shape[2] * 4 for a in side) // len(tiles)
    tm = _proj_row_tile(rows, k, n_out, side_bytes)
    n_row_tiles = rows // tm
    lookup = _tile_lookup(tiles)
    out_cols = len(tiles) * IN_TILE
    in_specs = [pl.BlockSpec((tm, k), lambda j, i: (i, 0)),
                pl.BlockSpec((None, k, IN_TILE), lambda j, i: (layer, 0, lookup(j)))]
    if mode == "rot":
        blocks_per_seq = seq // tm
        tab = pl.BlockSpec((None, tm, DK), lambda j, i: (j, i % blocks_per_seq, 0))
        in_specs += [tab, tab]
    elif mode == "forget":
        in_specs += [pl.BlockSpec((1, IN_TILE), lambda j, i: (0, 0))]
    operands = [xn, w_in, *extra]
    out_specs = [pl.BlockSpec((tm, IN_TILE), lambda j, i: (i, j))] * n_out
    out_shape = [jax.ShapeDtypeStruct((rows, out_cols), BF16)] * n_out
    if with_meta:
        in_specs += [pl.BlockSpec((CHUNK, k), lambda j, i: (0, 0))]
        operands += [xn_meta]
        if mode == "rot":
            in_specs += [pl.BlockSpec((None, CHUNK, DK), lambda j, i: (j, 0, 0))] * 2
            operands += list(meta_extra)
        out_specs += [pl.BlockSpec((CHUNK, IN_TILE), lambda j, i: (0, j))] * n_out
        out_shape += [jax.ShapeDtypeStruct((CHUNK, out_cols), BF16)] * n_out
    side_in, side_out, side_shape = _side_cast_specs(side, layer, len(tiles) * n_row_tiles,
                                                     lambda j, i: j * n_row_tiles + i)
    res = pl.pallas_call(
        functools.partial(_proj_kernel, mode=mode, with_meta=with_meta, n_side=len(side)),
        grid=(len(tiles), n_row_tiles),
        in_specs=in_specs + side_in,
        out_specs=out_specs + side_out,
        out_shape=out_shape + side_shape,
        scratch_shapes=[pltpu.VMEM((k, IN_TILE), BF16)],
        compiler_params=_params("parallel", "arbitrary"),
        name="in_proj_" + mode,
    )(*operands, *side)
    main = res[:n_out]
    meta = res[n_out:2 * n_out] if with_meta else None
    casts = tuple(res[len(res) - len(side):]) if side else ()
    if n_out == 1:
        main, meta = main[0], (meta[0] if with_meta else None)
    return main, meta, casts


def _rot_tables(seq, pos0):
    half = DK // 2
    inv = ROPE_BASE ** (-jnp.arange(half, dtype=F32) / half)
    pos = jnp.arange(seq, dtype=jnp.int32) + pos0
    ang = pos.astype(F32)[:, None] * inv[None, :]
    cos, sin = jnp.cos(ang), jnp.sin(ang)
    cos2 = jnp.concatenate([cos, cos], axis=1)
    sin2 = jnp.concatenate([-sin, sin], axis=1)
    scale = DK ** -0.5
    return jnp.stack([cos2, cos2 * scale]), jnp.stack([sin2, sin2 * scale])


def _ret_kernel(q_ref, k_ref, v_ref, g_ref, dmat_ref, qdec_ref, kdec_ref, cdec_ref, s0_ref, *rest,
                hb, nchunks, pos0, n_side):
    o_ref, sfin_ref = rest[n_side:n_side + 2]
    s_ref = rest[-1]
    _side_cast(rest[:n_side], rest[n_side + 2:-1])

    @pl.when(pl.program_id(2) == 0)
    def _():
        s_ref[...] = s0_ref[...]

    def body(c, carry):
        r0 = pl.multiple_of(c * CHUNK, CHUNK)
        rows = pl.ds(r0, CHUNK)
        if pos0 < 0:
            valid = (lax.broadcasted_iota(jnp.int32, (CHUNK, 1), 0) + (r0 + pos0)) >= 0
        heads = range(hb)
        q = [q_ref[rows, j * DK:(j + 1) * DK] for j in heads]
        k = [k_ref[rows, j * DK:(j + 1) * DK] for j in heads]
        if pos0 < 0:
            k = [jnp.where(valid, kj, jnp.zeros_like(kj)) for kj in k]
        v = [v_ref[rows, j * DV:(j + 1) * DV] for j in heads]
        scores = [jnp.dot(q[j], k[j].astype(F32).T.astype(BF16), preferred_element_type=F32) for j in heads]
        s = [s_ref[j] for j in heads]
        new_s = [lax.dot_general(k[j] * kdec_ref[j], v[j], TN_DIMS, preferred_element_type=F32) for j in heads]
        y = [jnp.dot(jnp.concatenate([(scores[j] * dmat_ref[j]).astype(BF16), q[j] * qdec_ref[j]], axis=1),
                     jnp.concatenate([v[j], s[j].astype(BF16)], axis=0), preferred_element_type=F32)
             for j in heads]
        for j in heads:
            s_ref[j] = s[j] * cdec_ref[j] + new_s[j]
            g = g_ref[rows, j * DV:(j + 1) * DV].astype(F32)
            o_ref[rows, j * DV:(j + 1) * DV] = (_rms_scale(y[j]) * g).astype(o_ref.dtype)
        return carry

    lax.fori_loop(0, nchunks, body, 0, unroll=MIXER_UNROLL)
    sfin_ref[...] = s_ref[...]


def _ret_tables():
    log_g = jnp.log1p(-jnp.exp2(-5.0 - jnp.arange(HEADS, dtype=F32)))
    idx = jnp.arange(CHUNK, dtype=F32)
    diff = idx[:, None] - idx[None, :]
    dmat = jnp.where(diff[None] >= 0, jnp.exp(jnp.maximum(diff, 0.0)[None] * log_g[:, None, None]), 0.0)
    qdec = jnp.exp((idx + 1)[None, :] * log_g[:, None])
    kdec = jnp.exp((CHUNK - 1 - idx)[None, :] * log_g[:, None])
    cdec = jnp.exp(CHUNK * log_g)
    qdec = jnp.broadcast_to(qdec[:, :, None], (HEADS, CHUNK, DK)).astype(BF16)
    kdec = jnp.broadcast_to(kdec[:, :, None], (HEADS, CHUNK, DK)).astype(BF16)
    cdec = jnp.broadcast_to(cdec[:, None, None], (HEADS, 1, DV))
    return dmat, qdec, kdec, cdec


def _mixer_grid(batch, seq):
    rb = min(seq, MIXER_ROWS)
    return rb, seq // rb


def _ret_call(qk, vi, act, tables, s0, batch, seq, pos0, layer=None, side=()):
    dmat, qdec, kdec, cdec = tables
    hb = HEADS_PER_STEP
    rb, nblk = _mixer_grid(batch, seq)
    k_blk = QK_W // (hb * DK)
    kern = functools.partial(_ret_kernel, hb=hb, nchunks=rb // CHUNK, pos0=pos0, n_side=len(side))
    per_head = lambda b, h, r: (h, 0, 0)
    n_hg = HEADS // hb
    side_in, side_out, side_shape = _side_cast_specs(side, layer, batch * n_hg * nblk,
                                                     lambda b, h, r: (b * n_hg + h) * nblk + r)
    res = pl.pallas_call(
        kern,
        grid=(batch, n_hg, nblk),
        in_specs=[
            pl.BlockSpec((rb, hb * DK), lambda b, h, r: (b * nblk + r, h)),
            pl.BlockSpec((rb, hb * DK), lambda b, h, r: (b * nblk + r, k_blk + h)),
            pl.BlockSpec((rb, hb * DV), lambda b, h, r: (b * nblk + r, h)),
            pl.BlockSpec((rb, hb * DV), lambda b, h, r: (b * nblk + r, h)),
            pl.BlockSpec((hb, CHUNK, CHUNK), per_head),
            pl.BlockSpec((hb, CHUNK, DK), per_head),
            pl.BlockSpec((hb, CHUNK, DK), per_head),
            pl.BlockSpec((hb, 1, DV), per_head),
            pl.BlockSpec((hb, DK, DV), per_head),
        ] + side_in,
        out_specs=[
            pl.BlockSpec((rb, hb * DV), lambda b, h, r: (b * nblk + r, h)),
            pl.BlockSpec((None, hb, DK, DV), lambda b, h, r: (b, h, 0, 0)),
        ] + side_out,
        out_shape=[jax.ShapeDtypeStruct((batch * seq, V_W), BF16),
                   jax.ShapeDtypeStruct((batch, HEADS, DK, DV), F32)] + side_shape,
        scratch_shapes=[pltpu.VMEM((hb, DK, DV), F32)],
        compiler_params=_params("parallel", "parallel", "arbitrary", flags=MIXER_FLAGS),
        name="retention",
    )(qk, qk, vi, act, dmat, qdec, kdec, cdec, s0, *side)
    return res[0], res[1], tuple(res[2:])


def _hg_tables():
    t = np.arange(CHUNK)[:, None]
    u = np.arange(CHUNK)[None, :]
    mats, masks = [], []
    for j in range(LEVELS):
        m = 1 << j
        upper = ((t >> j) & 1) == 1
        q_part = upper & (u >= (t & ~(m - 1))) & (u <= t)
        k_part = (~upper) & (u > t) & (u <= (t | (m - 1)))
        if m < SUBLANES:
            mats.append(q_part | k_part)
        masks.append(((t >> (j + 1)) == (u >> (j + 1))) & upper & (((u >> j) & 1) == 0))
    mats.append(u <= t)
    masks.append(t == u)
    mstack = np.concatenate(mats, axis=0).astype(np.float32)
    mstack = np.concatenate([mstack, mstack], axis=1)
    return jnp.asarray(mstack, dtype=BF16), jnp.asarray(np.stack(masks).astype(np.float32))


def _dot_keys(queries, keys, level):
    if level in XLU_TRANSPOSE_LEVELS:
        return jnp.dot(queries, keys.T.astype(BF16), preferred_element_type=F32)
    return lax.dot_general(queries, keys.astype(BF16), NT_DIMS, preferred_element_type=F32)


def _hg_kernel(q_ref, kin_ref, hi_ref, lo_ref, v_ref, g_ref, nw_ref, mstack_ref, masks_ref, s0_ref,
               o_ref, sfin_ref, st_ref, e_ref, *, hb, nchunks):
    @pl.when(pl.program_id(2) == 0)
    def _():
        st_ref[...] = s0_ref[...]

    row = lax.broadcasted_iota(jnp.int32, (CHUNK, 1), 0)

    def exponents(c, slot, pair):
        rows = pl.ds(pl.multiple_of(c * CHUNK, CHUNK), CHUNK)
        lanes = slice(pair * 2 * DK, (pair + 1) * 2 * DK)
        pieces = jnp.concatenate([hi_ref[rows, lanes], lo_ref[rows, lanes]], axis=0)
        e_ref[slot, :, lanes] = jnp.dot(mstack_ref[...], pieces, preferred_element_type=F32)

    def process(c, slot, nxt):
        for g0 in range(0, hb, HG_GROUP):
            process_group(c, slot, nxt, range(g0, g0 + HG_GROUP))

    def process_group(c, slot, nxt, heads):
        rows = pl.ds(pl.multiple_of(c * CHUNK, CHUNK), CHUNK)
        expo = {j: e_ref.at[slot, :, j * DK:(j + 1) * DK] for j in heads}
        qb = {j: q_ref[rows, j * DK:(j + 1) * DK] for j in heads}
        q = {j: qb[j].astype(F32) for j in heads}
        kin = {j: kin_ref[rows, j * DK:(j + 1) * DK].astype(F32) for j in heads}
        b = {j: expo[j][SMALL_LEVELS * CHUNK:, :] for j in heads}
        a = {j: _dot_keys(qb[j], kin[j], LEVELS) * masks_ref[LEVELS] for j in heads}
        a_rows = None
        issue_at = {1 + 3 * i: p for i, p in enumerate(range(heads[0] // 2, heads[-1] // 2 + 1))}
        for lev in range(LEVELS):
            m = 1 << lev
            if nxt is not None and lev in issue_at:
                exponents(nxt, 1 - slot, issue_at[lev])
            if m < SUBLANES:
                for j in heads:
                    e = jnp.exp2(expo[j][lev * CHUNK:(lev + 1) * CHUNK, :])
                    x = jnp.where(((row >> lev) & 1) == 1, q[j], kin[j]) * e
                    a[j] = a[j] + _dot_keys(x.astype(BF16), x, lev) * masks_ref[lev]
            else:
                if a_rows is None:
                    a_rows = {j: [a[j][s:s + SUBLANES] for s in range(0, CHUNK, SUBLANES)] for j in heads}
                upper = [s for s0 in range(m, CHUNK, 2 * m) for s in range(s0, s0 + m, SUBLANES)]
                for j in heads:
                    parts = []
                    for s0 in range(0, CHUNK, 2 * m):
                        b_mid = jnp.broadcast_to(b[j][s0 + m - 1:s0 + m, :], (m, DK))
                        parts += [b_mid - b[j][s0:s0 + m], b[j][s0 + m:s0 + 2 * m] - b_mid]
                    e = jnp.exp2(jnp.concatenate(parts, axis=0))
                    qe = jnp.concatenate([q[j][s:s + SUBLANES] * e[s:s + SUBLANES] for s in upper], axis=0)
                    p = _dot_keys(qe.astype(BF16), kin[j] * e, lev)
                    for idx, s in enumerate(upper):
                        a_rows[j][s // SUBLANES] = (a_rows[j][s // SUBLANES]
                                                    + p[idx * SUBLANES:(idx + 1) * SUBLANES]
                                                    * masks_ref[lev, s:s + SUBLANES, :])
        total = {j: b[j][CHUNK - 1:CHUNK, :] for j in heads}
        v = {j: v_ref[rows, j * DV:(j + 1) * DV] for j in heads}
        st = {j: st_ref[j] for j in heads}
        new_st = {j: lax.dot_general((kin[j] * jnp.exp2(total[j] - b[j])).astype(BF16), v[j], TN_DIMS,
                                     preferred_element_type=F32) for j in heads}
        y = {j: jnp.dot(jnp.concatenate([jnp.concatenate(a_rows[j], axis=0).astype(BF16),
                                         (q[j] * jnp.exp2(b[j])).astype(BF16)], axis=1),
                        jnp.concatenate([v[j], st[j].astype(BF16)], axis=0), preferred_element_type=F32)
             for j in heads}
        for j in heads:
            decay = jnp.broadcast_to(jnp.exp2(total[j]), (CHUNK, DK)).T
            st_ref[j] = st[j] * jnp.concatenate([decay] * (DV // DK), axis=1) + new_st[j]
            g = g_ref[rows, j * DV:(j + 1) * DV].astype(F32)
            o_ref[rows, j * DV:(j + 1) * DV] = (_rms_scale(y[j]) * nw_ref[...] * g).astype(o_ref.dtype)

    for pr in range(hb // 2):
        exponents(0, 0, pr)
    if nchunks == 1:
        process(0, 0, None)
    else:
        assert nchunks % 2 == 0

        def pair(p, carry):
            c = 2 * p
            process(c, 0, c + 1)
            process(c + 1, 1, jnp.minimum(c + 2, nchunks - 1))
            return carry

        lax.fori_loop(0, nchunks // 2, pair, 0)
    sfin_ref[...] = st_ref[...]


def _hg_call(act, kin, lf_hi, lf_lo, vi, nw_row, tables, s0, batch, seq):
    mstack, masks = tables
    hb = HEADS_PER_STEP
    rb, nblk = _mixer_grid(batch, seq)
    q_blk = V_W // (hb * DK)
    g_blk = (V_W + QK_W) // (hb * DV)
    i_blk = V_W // (hb * DV)
    kern = functools.partial(_hg_kernel, hb=hb, nchunks=rb // CHUNK)
    row_blk = lambda off: (lambda b, h, r: (b * nblk + r, off + h))
    return pl.pallas_call(
        kern,
        grid=(batch, HEADS // hb, nblk),
        in_specs=[
            pl.BlockSpec((rb, hb * DK), row_blk(q_blk)),
            pl.BlockSpec((rb, hb * DK), row_blk(0)),
            pl.BlockSpec((rb, hb * DK), row_blk(0)),
            pl.BlockSpec((rb, hb * DK), row_blk(0)),
            pl.BlockSpec((rb, hb * DV), row_blk(i_blk)),
            pl.BlockSpec((rb, hb * DV), row_blk(g_blk)),
            pl.BlockSpec((1, DV), lambda b, h, r: (0, 0)),
            pl.BlockSpec(mstack.shape, lambda b, h, r: (0, 0)),
            pl.BlockSpec(masks.shape, lambda b, h, r: (0, 0, 0)),
            pl.BlockSpec((hb, DK, DV), lambda b, h, r: (h, 0, 0)),
        ],
        out_specs=[
            pl.BlockSpec((rb, hb * DV), row_blk(0)),
            pl.BlockSpec((None, hb, DK, DV), lambda b, h, r: (b, h, 0, 0)),
        ],
        out_shape=[jax.ShapeDtypeStruct((batch * seq, V_W), BF16),
                   jax.ShapeDtypeStruct((batch, HEADS, DK, DV), F32)],
        scratch_shapes=[pltpu.VMEM((hb, DK, DV), F32),
                        pltpu.VMEM((2, (SMALL_LEVELS + 1) * CHUNK, hb * DK), F32)],
        compiler_params=_params("parallel", "parallel", "arbitrary", flags=MIXER_FLAGS),
        name="hgrn2",
    )(act, kin, lf_hi, lf_lo, vi, act, nw_row, mstack, masks, s0)


def _merge_kernel(yr_ref, yh_ref, wr_ref, wh_ref, gr_ref, gh_ref, o_ref):
    a = jnp.dot(yr_ref[...], wr_ref[...], preferred_element_type=F32)
    b = jnp.dot(yh_ref[...], wh_ref[...], preferred_element_type=F32)
    o_ref[...] = (gr_ref[...].astype(F32) * a + gh_ref[...].astype(F32) * b).astype(o_ref.dtype)


def _merge_call(yr, yh, gates, w_ret, w_hg):
    rows, k = yr.shape
    d = w_ret.shape[-1]
    tm = min(rows, ROW_TILE)
    tn = MERGE_COL_TILE
    gh_blk = d // tn
    return pl.pallas_call(
        _merge_kernel,
        grid=(rows // tm, d // tn),
        in_specs=[
            pl.BlockSpec((tm, k), lambda i, j: (i, 0)),
            pl.BlockSpec((tm, k), lambda i, j: (i, 0)),
            pl.BlockSpec((k, tn), lambda i, j: (0, j)),
            pl.BlockSpec((k, tn), lambda i, j: (0, j)),
            pl.BlockSpec((tm, tn), lambda i, j: (i, j)),
            pl.BlockSpec((tm, tn), lambda i, j: (i, gh_blk + j)),
        ],
        out_specs=pl.BlockSpec((tm, tn), lambda i, j: (i, j)),
        out_shape=jax.ShapeDtypeStruct((rows, d), BF16),
        compiler_params=_params("parallel", "arbitrary"),
        name="branch_merge",
    )(yr, yh, w_ret, w_hg, gates, gates)


def _out_kernel(y_ref, w_ref, h_ref, post_ref, nxt_ref, *rest, n_side):
    hn_ref, xn_ref = rest[n_side:n_side + 2]
    _side_cast(rest[:n_side], rest[n_side + 2:])
    m = jnp.dot(y_ref[...], w_ref[...], preferred_element_type=F32)
    hn = h_ref[...] + _rms_scale(m) * post_ref[...]
    hn_ref[...] = hn
    xn_ref[...] = (_rms_scale(hn) * nxt_ref[...]).astype(xn_ref.dtype)


def _out_call(y, w_out, h, post_row, next_row, layer=None, side=()):
    rows, d = h.shape
    tm = min(rows, FFN_ROW_TILE)
    row_blk = pl.BlockSpec((tm, d), lambda i: (i, 0))
    vec_blk = pl.BlockSpec((1, d), lambda i: (0, 0))
    side_in, side_out, side_shape = _side_cast_specs(side, layer, rows // tm, lambda i: i)
    res = pl.pallas_call(
        functools.partial(_out_kernel, n_side=len(side)),
        grid=(rows // tm,),
        in_specs=[row_blk, pl.BlockSpec((d, d), lambda i: (0, 0)), row_blk, vec_blk, vec_blk] + side_in,
        out_specs=[row_blk, row_blk] + side_out,
        out_shape=[jax.ShapeDtypeStruct((rows, d), F32), jax.ShapeDtypeStruct((rows, d), BF16)] + side_shape,
        compiler_params=_params("parallel"),
        name="out_proj",
    )(y, w_out, h, post_row, next_row, *side)
    return res[0], res[1], tuple(res[2:])


def _ffn_kernel(x_ref, wg_ref, wu_ref, wd_ref, h_ref, post_ref, nxt_ref, hn_ref, *rest, with_next):
    if with_next:
        xn_ref, acc_ref = rest
    else:
        (acc_ref,) = rest
    f = pl.program_id(1)

    @pl.when(f == 0)
    def _():
        acc_ref[...] = jnp.zeros_like(acc_ref)

    x = x_ref[...]
    g = jnp.dot(x, wg_ref[...], preferred_element_type=F32)
    u = jnp.dot(x, wu_ref[...], preferred_element_type=F32)
    act = (_silu(g) * u).astype(BF16)
    acc_ref[...] += jnp.dot(act, wd_ref[...], preferred_element_type=F32)

    @pl.when(f == pl.num_programs(1) - 1)
    def _():
        hn = h_ref[...] + _rms_scale(acc_ref[...]) * post_ref[...]
        hn_ref[...] = hn
        if with_next:
            xn_ref[...] = (_rms_scale(hn) * nxt_ref[...]).astype(xn_ref.dtype)


def _ffn_call(xn, w_gate, w_up, w_down, h, post_row, next_row, with_next):
    rows, d = h.shape
    d_ff = w_gate.shape[-1]
    tm = min(rows, FFN_ROW_TILE)
    tf = FFN_COL_TILE
    row_blk = pl.BlockSpec((tm, d), lambda i, f: (i, 0))
    vec_blk = pl.BlockSpec((1, d), lambda i, f: (0, 0))
    n_state = 2 if with_next else 1
    res = pl.pallas_call(
        functools.partial(_ffn_kernel, with_next=with_next),
        grid=(rows // tm, d_ff // tf),
        in_specs=[
            row_blk,
            pl.BlockSpec((d, tf), lambda i, f: (0, f)),
            pl.BlockSpec((d, tf), lambda i, f: (0, f)),
            pl.BlockSpec((tf, d), lambda i, f: (f, 0)),
            row_blk, vec_blk, vec_blk,
        ],
        out_specs=[row_blk] * n_state,
        out_shape=[jax.ShapeDtypeStruct((rows, d), F32), jax.ShapeDtypeStruct((rows, d), BF16)][:n_state],
        scratch_shapes=[pltpu.VMEM((tm, d), F32)],
        compiler_params=_params("parallel", "arbitrary"),
        name="swiglu_ffn",
    )(xn, w_gate, w_up, w_down, h, post_row, next_row)
    return res[0], (res[1] if with_next else None)


def kernel(x, meta_tokens, norm_mix_pre, norm_mix_post, norm_ffn_pre, norm_ffn_post, w_in, hg_lb_logits,
           hg_norm_w, w_br_ret, w_br_hg, w_out, w_ffn_gate, w_ffn_up, w_ffn_down):
    batch, seq, d = x.shape
    depth = w_in.shape[0]
    assert seq % CHUNK == 0 and meta_tokens.shape == (N_META, d)
    assert w_in.shape[-1] == sum(_IN_WIDTHS) and d == V_W

    lb_sm = jax.nn.softmax(hg_lb_logits.astype(F32), axis=0)
    lbs = jnp.cumsum(lb_sm, axis=0) - lb_sm[0:1]

    hg_tables = _hg_tables()
    ret_tables = _ret_tables()
    meta_h = jnp.concatenate([jnp.zeros((PAD, d), F32), meta_tokens.astype(F32)], axis=0)
    meta = dict(h=meta_h, batch=1, seq=CHUNK, pos0=-PAD)
    main = dict(h=x.reshape(batch * seq, d).astype(F32), batch=batch, seq=seq, pos0=N_META)
    for rs in (meta, main):
        rs["rot"] = _rot_tables(rs["seq"], rs["pos0"])
        rs["xn"] = _norm_call(rs["h"], norm_mix_pre[0][None])

    for l in range(depth):
        last = l == depth - 1
        proj = functools.partial(_proj_call, main["xn"], meta["xn"], w_in, l, seq=seq)
        main["qk"], meta["qk"], _ = proj(ROT_TILES, "rot", extra=main["rot"], meta_extra=meta["rot"])
        main["vi"], meta["vi"], (w_ret_b, w_hg_b, w_out_b) = proj(IDENT_TILES, "ident",
                                                                 side=(w_br_ret, w_br_hg, w_out))
        main["act"], meta["act"], _ = proj(SILU_TILES, "silu")
        main["forget"], meta["forget"], _ = proj(FORGET_TILES, "forget", extra=(lbs[l][None],))
        main["gates"], meta["gates"], (w_gate_b,) = _proj_call(
            main["xn"], None if last else meta["xn"], w_in, l, SIGMOID_TILES, "sigmoid", seq, side=(w_ffn_gate,))

        ret_state = jnp.zeros((HEADS, DK, DV), F32)
        hg_state = jnp.zeros((HEADS, DK, DV), F32)
        for rs in (meta, main):
            kin, lf_hi, lf_lo = rs["forget"]
            side = () if rs is meta else (w_ffn_up,)
            rs["yr"], ret_fin, casts = _ret_call(rs["qk"], rs["vi"], rs["act"], ret_tables, ret_state,
                                                 rs["batch"], rs["seq"], rs["pos0"], l, side)
            if rs is main:
                (w_up_b,) = casts
            rs["yh"], hg_fin = _hg_call(rs["act"], kin, lf_hi, lf_lo, rs["vi"], hg_norm_w[l][None], hg_tables,
                                        hg_state, rs["batch"], rs["seq"])
            if rs is meta:
                ret_state, hg_state = ret_fin[0], hg_fin[0]

        next_row = norm_mix_pre[min(l + 1, depth - 1)][None]
        post_row = norm_ffn_post[l][None]
        w_down_b = None
        for rs in (main,) if last else (main, meta):
            y = _merge_call(rs["yr"], rs["yh"], rs["gates"], w_ret_b, w_hg_b)
            side = () if rs is meta else (w_ffn_down,)
            h_mid, xn_ffn, casts = _out_call(y, w_out_b, rs["h"], norm_mix_post[l][None], norm_ffn_pre[l][None],
                                             l, side)
            if rs is main:
                (w_down_b,) = casts
            rs["h"], rs["xn"] = _ffn_call(xn_ffn, w_gate_b, w_up_b, w_down_b, h_mid, post_row, next_row, not last)
    return main["h"].reshape(batch, seq, d)
```

```python
import functools

import numpy as np
import jax
import jax.numpy as jnp
from jax import lax
from jax.experimental import pallas as pl
from jax.experimental.pallas import tpu as pltpu

N_META = 16
HEADS = 8
DK = 128
DV = 256
CHUNK = 128
PAD = CHUNK - N_META
RMS_EPS = 1e-6
ROPE_BASE = 10000.0
LEVELS = 7
SUBLANES = 8
SMALL_LEVELS = 3
BF16_SUBLANES = 16
QK_W = HEADS * DK
V_W = HEADS * DV

VMEM_LIMIT_BYTES = 58 * 1024 * 1024
ROW_TILE = 1024
IN_TILE = 1024
PROJ_SLAB = 512
PROJ_VMEM_BUDGET = 54 * 1024 * 1024
HEADS_PER_STEP = 4
MIXER_ROWS = 2048
MIXER_UNROLL = 2
FFN_ROW_TILE = 512
FFN_COL_TILE = 512
MERGE_COL_TILE = 1024
XLU_TRANSPOSE_LEVELS = (4, 5, 6)
HG_GROUP = 4

_IN_WIDTHS = (QK_W, QK_W, V_W, V_W, QK_W, QK_W, V_W, V_W, V_W, V_W)
_IN_STARTS = tuple(sum(_IN_WIDTHS[:i]) // IN_TILE for i in range(len(_IN_WIDTHS)))


def _tiles(*parts):
    return tuple(t for p in parts for t in range(_IN_STARTS[p], _IN_STARTS[p] + _IN_WIDTHS[p] // IN_TILE))


ROT_TILES = _tiles(0, 1)
IDENT_TILES = _tiles(2, 6)
SILU_TILES = _tiles(3, 4, 7)
FORGET_TILES = _tiles(5)
SIGMOID_TILES = _tiles(8, 9)

F32 = jnp.float32
BF16 = jnp.bfloat16
NT_DIMS = (((1,), (1,)), ((), ()))
TN_DIMS = (((0,), (0,)), ((), ()))


def _params(*semantics, flags=None):
    return pltpu.CompilerParams(dimension_semantics=semantics, vmem_limit_bytes=VMEM_LIMIT_BYTES, flags=flags)


MIXER_FLAGS = None


def _sigmoid(x):
    return 0.5 + 0.5 * jnp.tanh(0.5 * x)


def _silu(x):
    h = 0.5 * x
    return h + h * jnp.tanh(h)


def _rms_scale(x):
    return x * lax.rsqrt(jnp.mean(x * x, axis=-1, keepdims=True) + RMS_EPS)


def _tile_lookup(tiles):
    def lookup(j):
        out = tiles[-1]
        for idx in range(len(tiles) - 2, -1, -1):
            out = jnp.where(j == idx, tiles[idx], out)
        return out
    return lookup


def _side_cast_specs(arrays, layer, nsteps, step_of):
    in_specs, out_specs, out_shape = [], [], []
    for a in arrays:
        _, r, c = a.shape
        assert r % (nsteps * BF16_SUBLANES) == 0, (a.shape, nsteps)
        br = r // nsteps
        in_specs.append(pl.BlockSpec((None, br, c), lambda *g: (layer, step_of(*g), 0)))
        out_specs.append(pl.BlockSpec((br, c), lambda *g: (step_of(*g), 0)))
        out_shape.append(jax.ShapeDtypeStruct((r, c), BF16))
    return in_specs, out_specs, out_shape


def _side_cast(side_in, side_out):
    for i_ref, o_ref in zip(side_in, side_out):
        o_ref[...] = i_ref[...].astype(BF16)


def _norm_kernel(h_ref, w_ref, o_ref):
    o_ref[...] = (_rms_scale(h_ref[...]) * w_ref[...]).astype(o_ref.dtype)


def _norm_call(h, w_row):
    rows, d = h.shape
    tm = min(rows, ROW_TILE)
    return pl.pallas_call(
        _norm_kernel,
        grid=(rows // tm,),
        in_specs=[pl.BlockSpec((tm, d), lambda i: (i, 0)), pl.BlockSpec((1, d), lambda i: (0, 0))],
        out_specs=pl.BlockSpec((tm, d), lambda i: (i, 0)),
        out_shape=jax.ShapeDtypeStruct((rows, d), BF16),
        compiler_params=_params("parallel"),
        name="rms_norm",
    )(h, w_row)


def _proj_epilogue(acc, mode, extras, outs, r, first_pos):
    if mode == "ident":
        outs[0][r, :] = acc.astype(BF16)
    elif mode == "silu":
        outs[0][r, :] = _silu(acc).astype(BF16)
    elif mode == "sigmoid":
        outs[0][r, :] = _sigmoid(acc).astype(BF16)
    elif mode == "rot":
        cos, sin = extras[0][r, :], extras[1][r, :]
        for g in range(IN_TILE // DK):
            xg = acc[:, g * DK:(g + 1) * DK]
            outs[0][r, g * DK:(g + 1) * DK] = (xg * cos + pltpu.roll(xg, DK // 2, 1) * sin).astype(BF16)
    elif mode == "forget":
        kin_ref, hi_ref, lo_ref = outs
        one_minus_f = (1.0 - extras[0][...]) * (0.5 - 0.5 * jnp.tanh(0.5 * acc))
        log_f = jnp.log2(1.0 - one_minus_f)
        if first_pos is not None:
            valid = (lax.broadcasted_iota(jnp.int32, (acc.shape[0], 1), 0) + first_pos) >= 0
            log_f = jnp.where(valid, log_f, 0.0)
            one_minus_f = jnp.where(valid, one_minus_f, 0.0)
        kin_ref[r, :] = one_minus_f.astype(BF16)
        hi = log_f.astype(BF16)
        hi_ref[r, :] = hi
        lo_ref[r, :] = (log_f - hi.astype(F32)).astype(BF16)


def _proj_kernel(*refs, mode, with_meta, n_side):
    n_extra = {"rot": 2, "forget": 1}.get(mode, 0)
    n_out = 3 if mode == "forget" else 1
    x_ref, w_ref = refs[:2]
    extras = refs[2:2 + n_extra]
    pos = 2 + n_extra
    if with_meta:
        xm_ref = refs[pos]
        n_meta_extra = 2 if mode == "rot" else 0
        meta_extras = refs[pos + 1:pos + 1 + n_meta_extra] if n_meta_extra else extras
        pos += 1 + n_meta_extra
    side_in = refs[pos:pos + n_side]
    pos += n_side
    outs = refs[pos:pos + n_out]
    pos += n_out
    if with_meta:
        meta_outs = refs[pos:pos + n_out]
        pos += n_out
    _side_cast(side_in, refs[pos:pos + n_side])
    wb_ref = refs[-1]

    @pl.when(pl.program_id(1) == 0)
    def _():
        wb_ref[...] = w_ref[...].astype(BF16)
        if with_meta:
            acc = jnp.dot(xm_ref[...], wb_ref[...], preferred_element_type=F32)
            _proj_epilogue(acc, mode, meta_extras, meta_outs, slice(None), -PAD)

    tm = x_ref.shape[0]
    slab = min(tm, PROJ_SLAB)
    pending = None
    for s0 in range(0, tm, slab):
        r = slice(s0, s0 + slab)
        acc = jnp.dot(x_ref[r, :], wb_ref[...], preferred_element_type=F32)
        if pending is not None:
            _proj_epilogue(*pending)
        pending = (acc, mode, extras, outs, r, None)
    _proj_epilogue(*pending)


def _proj_row_tile(rows, k, n_out, side_bytes):
    for tm in (2 * ROW_TILE, ROW_TILE):
        if tm > rows or rows % tm:
            continue
        x_bytes = 2 * tm * k * 2
        w_bytes = 2 * k * IN_TILE * 4 + k * IN_TILE * 2
        out_bytes = n_out * 2 * tm * IN_TILE * 2
        acc_bytes = 2 * PROJ_SLAB * IN_TILE * 4
        meta_bytes = 2 * CHUNK * (k + n_out * IN_TILE) * 2
        side = 2 * (side_bytes + side_bytes // 2) // (rows // tm)
        if x_bytes + w_bytes + out_bytes + acc_bytes + meta_bytes + side <= PROJ_VMEM_BUDGET:
            return tm
    return min(rows, ROW_TILE)


def _proj_call(xn, xn_meta, w_in, layer, tiles, mode, seq, extra=(), meta_extra=(), side=()):
    rows, k = xn.shape
    n_out = 3 if mode == "forget" else 1
    with_meta = xn_meta is not None
    side_bytes = sum(a.shape[1] * a.shape[2] * 4 for a in side) // len(tiles)
    tm = _proj_row_tile(rows, k, n_out, side_bytes)
    n_row_tiles = rows // tm
    lookup = _tile_lookup(tiles)
    out_cols = len(tiles) * IN_TILE
    in_specs = [pl.BlockSpec((tm, k), lambda j, i: (i, 0)),
                pl.BlockSpec((None, k, IN_TILE), lambda j, i: (layer, 0, lookup(j)))]
    if mode == "rot":
        blocks_per_seq = seq // tm
        tab = pl.BlockSpec((None, tm, DK), lambda j, i: (j, i % blocks_per_seq, 0))
        in_specs += [tab, tab]
    elif mode == "forget":
        in_specs += [pl.BlockSpec((1, IN_TILE), lambda j, i: (0, 0))]
    operands = [xn, w_in, *extra]
    out_specs = [pl.BlockSpec((tm, IN_TILE), lambda j, i: (i, j))] * n_out
    out_shape = [jax.ShapeDtypeStruct((rows, out_cols), BF16)] * n_out
    if with_meta:
        in_specs += [pl.BlockSpec((CHUNK, k), lambda j, i: (0, 0))]
        operands += [xn_meta]
        if mode == "rot":
            in_specs += [pl.BlockSpec((None, CHUNK, DK), lambda j, i: (j, 0, 0))] * 2
            operands += list(meta_extra)
        out_specs += [pl.BlockSpec((CHUNK, IN_TILE), lambda j, i: (0, j))] * n_out
        out_shape += [jax.ShapeDtypeStruct((CHUNK, out_cols), BF16)] * n_out
    side_in, side_out, side_shape = _side_cast_specs(side, layer, len(tiles) * n_row_tiles,
                                                     lambda j, i: j * n_row_tiles + i)
    res = pl.pallas_call(
        functools.partial(_proj_kernel, mode=mode, with_meta=with_meta, n_side=len(side)),
        grid=(len(tiles), n_row_tiles),
        in_specs=in_specs + side_in,
        out_specs=out_specs + side_out,
        out_shape=out_shape + side_shape,
        scratch_shapes=[pltpu.VMEM((k, IN_TILE), BF16)],
        compiler_params=_params("parallel", "arbitrary"),
        name="in_proj_" + mode,
    )(*operands, *side)
    main = res[:n_out]
    meta = res[n_out:2 * n_out] if with_meta else None
    casts = tuple(res[len(res) - len(side):]) if side else ()
    if n_out == 1:
        main, meta = main[0], (meta[0] if with_meta else None)
    return main, meta, casts


def _rot_tables(seq, pos0):
    half = DK // 2
    inv = ROPE_BASE ** (-jnp.arange(half, dtype=F32) / half)
    pos = jnp.arange(seq, dtype=jnp.int32) + pos0
    ang = pos.astype(F32)[:, None] * inv[None, :]
    cos, sin = jnp.cos(ang), jnp.sin(ang)
    cos2 = jnp.concatenate([cos, cos], axis=1)
    sin2 = jnp.concatenate([-sin, sin], axis=1)
    scale = DK ** -0.5
    return jnp.stack([cos2, cos2 * scale]), jnp.stack([sin2, sin2 * scale])


def _ret_kernel(q_ref, k_ref, v_ref, g_ref, dmat_ref, qdec_ref, kdec_ref, cdec_ref, s0_ref, *rest,
                hb, nchunks, pos0, n_side):
    o_ref, sfin_ref = rest[n_side:n_side + 2]
    s_ref = rest[-1]
    _side_cast(rest[:n_side], rest[n_side + 2:-1])

    @pl.when(pl.program_id(2) == 0)
    def _():
        s_ref[...] = s0_ref[...]

    def body(c, carry):
        r0 = pl.multiple_of(c * CHUNK, CHUNK)
        rows = pl.ds(r0, CHUNK)
        if pos0 < 0:
            valid = (lax.broadcasted_iota(jnp.int32, (CHUNK, 1), 0) + (r0 + pos0)) >= 0
        heads = range(hb)
        q = [q_ref[rows, j * DK:(j + 1) * DK] for j in heads]
        k = [k_ref[rows, j * DK:(j + 1) * DK] for j in heads]
        if pos0 < 0:
            k = [jnp.where(valid, kj, jnp.zeros_like(kj)) for kj in k]
        v = [v_ref[rows, j * DV:(j + 1) * DV] for j in heads]
        scores = [jnp.dot(q[j], k[j].astype(F32).T.astype(BF16), preferred_element_type=F32) for j in heads]
        s = [s_ref[j] for j in heads]
        new_s = [lax.dot_general(k[j] * kdec_ref[j], v[j], TN_DIMS, preferred_element_type=F32) for j in heads]
        y = [jnp.dot(jnp.concatenate([(scores[j] * dmat_ref[j]).astype(BF16), q[j] * qdec_ref[j]], axis=1),
                     jnp.concatenate([v[j], s[j].astype(BF16)], axis=0), preferred_element_type=F32)
             for j in heads]
        for j in heads:
            s_ref[j] = s[j] * cdec_ref[j] + new_s[j]
            g = g_ref[rows, j * DV:(j + 1) * DV].astype(F32)
            o_ref[rows, j * DV:(j + 1) * DV] = (_rms_scale(y[j]) * g).astype(o_ref.dtype)
        return carry

    lax.fori_loop(0, nchunks, body, 0, unroll=MIXER_UNROLL)
    sfin_ref[...] = s_ref[...]


def _ret_tables():
    log_g = jnp.log1p(-jnp.exp2(-5.0 - jnp.arange(HEADS, dtype=F32)))
    idx = jnp.arange(CHUNK, dtype=F32)
    diff = idx[:, None] - idx[None, :]
    dmat = jnp.where(diff[None] >= 0, jnp.exp(jnp.maximum(diff, 0.0)[None] * log_g[:, None, None]), 0.0)
    qdec = jnp.exp((idx + 1)[None, :] * log_g[:, None])
    kdec = jnp.exp((CHUNK - 1 - idx)[None, :] * log_g[:, None])
    cdec = jnp.exp(CHUNK * log_g)
    qdec = jnp.broadcast_to(qdec[:, :, None], (HEADS, CHUNK, DK)).astype(BF16)
    kdec = jnp.broadcast_to(kdec[:, :, None], (HEADS, CHUNK, DK)).astype(BF16)
    cdec = jnp.broadcast_to(cdec[:, None, None], (HEADS, 1, DV))
    return dmat, qdec, kdec, cdec


def _mixer_grid(batch, seq):
    rb = min(seq, MIXER_ROWS)
    return rb, seq // rb


def _ret_call(qk, vi, act, tables, s0, batch, seq, pos0, layer=None, side=()):
    dmat, qdec, kdec, cdec = tables
    hb = HEADS_PER_STEP
    rb, nblk = _mixer_grid(batch, seq)
    k_blk = QK_W // (hb * DK)
    kern = functools.partial(_ret_kernel, hb=hb, nchunks=rb // CHUNK, pos0=pos0, n_side=len(side))
    per_head = lambda b, h, r: (h, 0, 0)
    n_hg = HEADS // hb
    side_in, side_out, side_shape = _side_cast_specs(side, layer, batch * n_hg * nblk,
                                                     lambda b, h, r: (b * n_hg + h) * nblk + r)
    res = pl.pallas_call(
        kern,
        grid=(batch, n_hg, nblk),
        in_specs=[
            pl.BlockSpec((rb, hb * DK), lambda b, h, r: (b * nblk + r, h)),
            pl.BlockSpec((rb, hb * DK), lambda b, h, r: (b * nblk + r, k_blk + h)),
            pl.BlockSpec((rb, hb * DV), lambda b, h, r: (b * nblk + r, h)),
            pl.BlockSpec((rb, hb * DV), lambda b, h, r: (b * nblk + r, h)),
            pl.BlockSpec((hb, CHUNK, CHUNK), per_head),
            pl.BlockSpec((hb, CHUNK, DK), per_head),
            pl.BlockSpec((hb, CHUNK, DK), per_head),
            pl.BlockSpec((hb, 1, DV), per_head),
            pl.BlockSpec((hb, DK, DV), per_head),
        ] + side_in,
        out_specs=[
            pl.BlockSpec((rb, hb * DV), lambda b, h, r: (b * nblk + r, h)),
            pl.BlockSpec((None, hb, DK, DV), lambda b, h, r: (b, h, 0, 0)),
        ] + side_out,
        out_shape=[jax.ShapeDtypeStruct((batch * seq, V_W), BF16),
                   jax.ShapeDtypeStruct((batch, HEADS, DK, DV), F32)] + side_shape,
        scratch_shapes=[pltpu.VMEM((hb, DK, DV), F32)],
        compiler_params=_params("parallel", "parallel", "arbitrary", flags=MIXER_FLAGS),
        name="retention",
    )(qk, qk, vi, act, dmat, qdec, kdec, cdec, s0, *side)
    return res[0], res[1], tuple(res[2:])


def _hg_tables():
    t = np.arange(CHUNK)[:, None]
    u = np.arange(CHUNK)[None, :]
    mats, masks = [], []
    for j in range(LEVELS):
        m = 1 << j
        upper = ((t >> j) & 1) == 1
        q_part = upper & (u >= (t & ~(m - 1))) & (u <= t)
        k_part = (~upper) & (u > t) & (u <= (t | (m - 1)))
        if m < SUBLANES:
            mats.append(q_part | k_part)
        masks.append(((t >> (j + 1)) == (u >> (j + 1))) & upper & (((u >> j) & 1) == 0))
    mats.append(u <= t)
    masks.append(t == u)
    mstack = np.concatenate(mats, axis=0).astype(np.float32)
    mstack = np.concatenate([mstack, mstack], axis=1)
    return jnp.asarray(mstack, dtype=BF16), jnp.asarray(np.stack(masks).astype(np.float32))


def _dot_keys(queries, keys, level):
    if level in XLU_TRANSPOSE_LEVELS:
        return jnp.dot(queries, keys.T.astype(BF16), preferred_element_type=F32)
    return lax.dot_general(queries, keys.astype(BF16), NT_DIMS, preferred_element_type=F32)


def _hg_kernel(q_ref, kin_ref, hi_ref, lo_ref, v_ref, g_ref, nw_ref, mstack_ref, masks_ref, s0_ref,
               o_ref, sfin_ref, st_ref, e_ref, *, hb, nchunks):
    @pl.when(pl.program_id(2) == 0)
    def _():
        st_ref[...] = s0_ref[...]

    row = lax.broadcasted_iota(jnp.int32, (CHUNK, 1), 0)

    def exponents(c, slot, pair):
        rows = pl.ds(pl.multiple_of(c * CHUNK, CHUNK), CHUNK)
        lanes = slice(pair * 2 * DK, (pair + 1) * 2 * DK)
        pieces = jnp.concatenate([hi_ref[rows, lanes], lo_ref[rows, lanes]], axis=0)
        e_ref[slot, :, lanes] = jnp.dot(mstack_ref[...], pieces, preferred_element_type=F32)

    def process(c, slot, nxt):
        for g0 in range(0, hb, HG_GROUP):
            process_group(c, slot, nxt, range(g0, g0 + HG_GROUP))

    def process_group(c, slot, nxt, heads):
        rows = pl.ds(pl.multiple_of(c * CHUNK, CHUNK), CHUNK)
        expo = {j: e_ref.at[slot, :, j * DK:(j + 1) * DK] for j in heads}
        qb = {j: q_ref[rows, j * DK:(j + 1) * DK] for j in heads}
        q = {j: qb[j].astype(F32) for j in heads}
        kin = {j: kin_ref[rows, j * DK:(j + 1) * DK].astype(F32) for j in heads}
        b = {j: expo[j][SMALL_LEVELS * CHUNK:, :] for j in heads}
        a = {j: _dot_keys(qb[j], kin[j], LEVELS) * masks_ref[LEVELS] for j in heads}
        a_rows = None
        issue_at = {1 + 3 * i: p for i, p in enumerate(range(heads[0] // 2, heads[-1] // 2 + 1))}
        for lev in range(LEVELS):
            m = 1 << lev
            if nxt is not None and lev in issue_at:
                exponents(nxt, 1 - slot, issue_at[lev])
            if m < SUBLANES:
                for j in heads:
                    e = jnp.exp2(expo[j][lev * CHUNK:(lev + 1) * CHUNK, :])
                    x = jnp.where(((row >> lev) & 1) == 1, q[j], kin[j]) * e
                    a[j] = a[j] + _dot_keys(x.astype(BF16), x, lev) * masks_ref[lev]
            else:
                if a_rows is None:
                    a_rows = {j: [a[j][s:s + SUBLANES] for s in range(0, CHUNK, SUBLANES)] for j in heads}
                upper = [s for s0 in range(m, CHUNK, 2 * m) for s in range(s0, s0 + m, SUBLANES)]
                for j in heads:
                    parts = []
                    for s0 in range(0, CHUNK, 2 * m):
                        b_mid = jnp.broadcast_to(b[j][s0 + m - 1:s0 + m, :], (m, DK))
                        parts += [b_mid - b[j][s0:s0 + m], b[j][s0 + m:s0 + 2 * m] - b_mid]
                    e = jnp.exp2(jnp.concatenate(parts, axis=0))
                    qe = jnp.concatenate([q[j][s:s + SUBLANES] * e[s:s + SUBLANES] for s in upper], axis=0)
                    p = _dot_keys(qe.astype(BF16), kin[j] * e, lev)
                    for idx, s in enumerate(upper):
                        a_rows[j][s // SUBLANES] = (a_rows[j][s // SUBLANES]
                                                    + p[idx * SUBLANES:(idx + 1) * SUBLANES]
                                                    * masks_ref[lev, s:s + SUBLANES, :])
        total = {j: b[j][CHUNK - 1:CHUNK, :] for j in heads}
        v = {j: v_ref[rows, j * DV:(j + 1) * DV] for j in heads}
        st = {j: st_ref[j] for j in heads}
        new_st = {j: lax.dot_general((kin[j] * jnp.exp2(total[j] - b[j])).astype(BF16), v[j], TN_DIMS,
                                     preferred_element_type=F32) for j in heads}
        y = {j: jnp.dot(jnp.concatenate([jnp.concatenate(a_rows[j], axis=0).astype(BF16),
                                         (q[j] * jnp.exp2(b[j])).astype(BF16)], axis=1),
                        jnp.concatenate([v[j], st[j].astype(BF16)], axis=0), preferred_element_type=F32)
             for j in heads}
        for j in heads:
            decay = jnp.broadcast_to(jnp.exp2(total[j]), (CHUNK, DK)).T
            st_ref[j] = st[j] * jnp.concatenate([decay] * (DV // DK), axis=1) + new_st[j]
            g = g_ref[rows, j * DV:(j + 1) * DV].astype(F32)
            o_ref[rows, j * DV:(j + 1) * DV] = (_rms_scale(y[j]) * nw_ref[...] * g).astype(o_ref.dtype)

    for pr in range(hb // 2):
        exponents(0, 0, pr)
    if nchunks == 1:
        process(0, 0, None)
    else:
        assert nchunks % 2 == 0

        def pair(p, carry):
            c = 2 * p
            process(c, 0, c + 1)
            process(c + 1, 1, jnp.minimum(c + 2, nchunks - 1))
            return carry

        lax.fori_loop(0, nchunks // 2, pair, 0)
    sfin_ref[...] = st_ref[...]


def _hg_call(act, kin, lf_hi, lf_lo, vi, nw_row, tables, s0, batch, seq):
    mstack, masks = tables
    hb = HEADS_PER_STEP
    rb, nblk = _mixer_grid(batch, seq)
    q_blk = V_W // (hb * DK)
    g_blk = (V_W + QK_W) // (hb * DV)
    i_blk = V_W // (hb * DV)
    kern = functools.partial(_hg_kernel, hb=hb, nchunks=rb // CHUNK)
    row_blk = lambda off: (lambda b, h, r: (b * nblk + r, off + h))
    return pl.pallas_call(
        kern,
        grid=(batch, HEADS // hb, nblk),
        in_specs=[
            pl.BlockSpec((rb, hb * DK), row_blk(q_blk)),
            pl.BlockSpec((rb, hb * DK), row_blk(0)),
            pl.BlockSpec((rb, hb * DK), row_blk(0)),
            pl.BlockSpec((rb, hb * DK), row_blk(0)),
            pl.BlockSpec((rb, hb * DV), row_blk(i_blk)),
            pl.BlockSpec((rb, hb * DV), row_blk(g_blk)),
            pl.BlockSpec((1, DV), lambda b, h, r: (0, 0)),
            pl.BlockSpec(mstack.shape, lambda b, h, r: (0, 0)),
            pl.BlockSpec(masks.shape, lambda b, h, r: (0, 0, 0)),
            pl.BlockSpec((hb, DK, DV), lambda b, h, r: (h, 0, 0)),
        ],
        out_specs=[
            pl.BlockSpec((rb, hb * DV), row_blk(0)),
            pl.BlockSpec((None, hb, DK, DV), lambda b, h, r: (b, h, 0, 0)),
        ],
        out_shape=[jax.ShapeDtypeStruct((batch * seq, V_W), BF16),
                   jax.ShapeDtypeStruct((batch, HEADS, DK, DV), F32)],
        scratch_shapes=[pltpu.VMEM((hb, DK, DV), F32),
                        pltpu.VMEM((2, (SMALL_LEVELS + 1) * CHUNK, hb * DK), F32)],
        compiler_params=_params("parallel", "parallel", "arbitrary", flags=MIXER_FLAGS),
        name="hgrn2",
    )(act, kin, lf_hi, lf_lo, vi, act, nw_row, mstack, masks, s0)


def _merge_kernel(yr_ref, yh_ref, wr_ref, wh_ref, gr_ref, gh_ref, o_ref):
    a = jnp.dot(yr_ref[...], wr_ref[...], preferred_element_type=F32)
    b = jnp.dot(yh_ref[...], wh_ref[...], preferred_element_type=F32)
    o_ref[...] = (gr_ref[...].astype(F32) * a + gh_ref[...].astype(F32) * b).astype(o_ref.dtype)


def _merge_call(yr, yh, gates, w_ret, w_hg):
    rows, k = yr.shape
    d = w_ret.shape[-1]
    tm = min(rows, ROW_TILE)
    tn = MERGE_COL_TILE
    gh_blk = d // tn
    return pl.pallas_call(
        _merge_kernel,
        grid=(rows // tm, d // tn),
        in_specs=[
            pl.BlockSpec((tm, k), lambda i, j: (i, 0)),
            pl.BlockSpec((tm, k), lambda i, j: (i, 0)),
            pl.BlockSpec((k, tn), lambda i, j: (0, j)),
            pl.BlockSpec((k, tn), lambda i, j: (0, j)),
            pl.BlockSpec((tm, tn), lambda i, j: (i, j)),
            pl.BlockSpec((tm, tn), lambda i, j: (i, gh_blk + j)),
        ],
        out_specs=pl.BlockSpec((tm, tn), lambda i, j: (i, j)),
        out_shape=jax.ShapeDtypeStruct((rows, d), BF16),
        compiler_params=_params("parallel", "arbitrary"),
        name="branch_merge",
    )(yr, yh, w_ret, w_hg, gates, gates)


def _out_kernel(y_ref, w_ref, h_ref, post_ref, nxt_ref, *rest, n_side):
    hn_ref, xn_ref = rest[n_side:n_side + 2]
    _side_cast(rest[:n_side], rest[n_side + 2:])
    m = jnp.dot(y_ref[...], w_ref[...], preferred_element_type=F32)
    hn = h_ref[...] + _rms_scale(m) * post_ref[...]
    hn_ref[...] = hn
    xn_ref[...] = (_rms_scale(hn) * nxt_ref[...]).astype(xn_ref.dtype)


def _out_call(y, w_out, h, post_row, next_row, layer=None, side=()):
    rows, d = h.shape
    tm = min(rows, FFN_ROW_TILE)
    row_blk = pl.BlockSpec((tm, d), lambda i: (i, 0))
    vec_blk = pl.BlockSpec((1, d), lambda i: (0, 0))
    side_in, side_out, side_shape = _side_cast_specs(side, layer, rows // tm, lambda i: i)
    res = pl.pallas_call(
        functools.partial(_out_kernel, n_side=len(side)),
        grid=(rows // tm,),
        in_specs=[row_blk, pl.BlockSpec((d, d), lambda i: (0, 0)), row_blk, vec_blk, vec_blk] + side_in,
        out_specs=[row_blk, row_blk] + side_out,
        out_shape=[jax.ShapeDtypeStruct((rows, d), F32), jax.ShapeDtypeStruct((rows, d), BF16)] + side_shape,
        compiler_params=_params("parallel"),
        name="out_proj",
    )(y, w_out, h, post_row, next_row, *side)
    return res[0], res[1], tuple(res[2:])


def _ffn_kernel(x_ref, wg_ref, wu_ref, wd_ref, h_ref, post_ref, nxt_ref, hn_ref, *rest, with_next):
    if with_next:
        xn_ref, acc_ref = rest
    else:
        (acc_ref,) = rest
    f = pl.program_id(1)

    @pl.when(f == 0)
    def _():
        acc_ref[...] = jnp.zeros_like(acc_ref)

    x = x_ref[...]
    g = jnp.dot(x, wg_ref[...], preferred_element_type=F32)
    u = jnp.dot(x, wu_ref[...], preferred_element_type=F32)
    act = (_silu(g) * u).astype(BF16)
    acc_ref[...] += jnp.dot(act, wd_ref[...], preferred_element_type=F32)

    @pl.when(f == pl.num_programs(1) - 1)
    def _():
        hn = h_ref[...] + _rms_scale(acc_ref[...]) * post_ref[...]
        hn_ref[...] = hn
        if with_next:
            xn_ref[...] = (_rms_scale(hn) * nxt_ref[...]).astype(xn_ref.dtype)


def _ffn_call(xn, w_gate, w_up, w_down, h, post_row, next_row, with_next):
    rows, d = h.shape
    d_ff = w_gate.shape[-1]
    tm = min(rows, FFN_ROW_TILE)
    tf = FFN_COL_TILE
    row_blk = pl.BlockSpec((tm, d), lambda i, f: (i, 0))
    vec_blk = pl.BlockSpec((1, d), lambda i, f: (0, 0))
    n_state = 2 if with_next else 1
    res = pl.pallas_call(
        functools.partial(_ffn_kernel, with_next=with_next),
        grid=(rows // tm, d_ff // tf),
        in_specs=[
            row_blk,
            pl.BlockSpec((d, tf), lambda i, f: (0, f)),
            pl.BlockSpec((d, tf), lambda i, f: (0, f)),
            pl.BlockSpec((tf, d), lambda i, f: (f, 0)),
            row_blk, vec_blk, vec_blk,
        ],
        out_specs=[row_blk] * n_state,
        out_shape=[jax.ShapeDtypeStruct((rows, d), F32), jax.ShapeDtypeStruct((rows, d), BF16)][:n_state],
        scratch_shapes=[pltpu.VMEM((tm, d), F32)],
        compiler_params=_params("parallel", "arbitrary"),
        name="swiglu_ffn",
    )(xn, w_gate, w_up, w_down, h, post_row, next_row)
    return res[0], (res[1] if with_next else None)


def kernel(x, meta_tokens, norm_mix_pre, norm_mix_post, norm_ffn_pre, norm_ffn_post, w_in, hg_lb_logits,
           hg_norm_w, w_br_ret, w_br_hg, w_out, w_ffn_gate, w_ffn_up, w_ffn_down):
    batch, seq, d = x.shape
    depth = w_in.shape[0]
    assert seq % CHUNK == 0 and meta_tokens.shape == (N_META, d)
    assert w_in.shape[-1] == sum(_IN_WIDTHS) and d == V_W

    lb_sm = jax.nn.softmax(hg_lb_logits.astype(F32), axis=0)
    lbs = jnp.cumsum(lb_sm, axis=0) - lb_sm[0:1]

    hg_tables = _hg_tables()
    ret_tables = _ret_tables()
    meta_h = jnp.concatenate([jnp.zeros((PAD, d), F32), meta_tokens.astype(F32)], axis=0)
    meta = dict(h=meta_h, batch=1, seq=CHUNK, pos0=-PAD)
    main = dict(h=x.reshape(batch * seq, d).astype(F32), batch=batch, seq=seq, pos0=N_META)
    for rs in (meta, main):
        rs["rot"] = _rot_tables(rs["seq"], rs["pos0"])
        rs["xn"] = _norm_call(rs["h"], norm_mix_pre[0][None])

    for l in range(depth):
        last = l == depth - 1
        proj = functools.partial(_proj_call, main["xn"], meta["xn"], w_in, l, seq=seq)
        main["qk"], meta["qk"], _ = proj(ROT_TILES, "rot", extra=main["rot"], meta_extra=meta["rot"])
        main["vi"], meta["vi"], (w_ret_b, w_hg_b, w_out_b) = proj(IDENT_TILES, "ident",
                                                                 side=(w_br_ret, w_br_hg, w_out))
        main["act"], meta["act"], _ = proj(SILU_TILES, "silu")
        main["forget"], meta["forget"], _ = proj(FORGET_TILES, "forget", extra=(lbs[l][None],))
        main["gates"], meta["gates"], (w_gate_b,) = _proj_call(
            main["xn"], None if last else meta["xn"], w_in, l, SIGMOID_TILES, "sigmoid", seq, side=(w_ffn_gate,))

        ret_state = jnp.zeros((HEADS, DK, DV), F32)
        hg_state = jnp.zeros((HEADS, DK, DV), F32)
        for rs in (meta, main):
            kin, lf_hi, lf_lo = rs["forget"]
            side = () if rs is meta else (w_ffn_up,)
            rs["yr"], ret_fin, casts = _ret_call(rs["qk"], rs["vi"], rs["act"], ret_tables, ret_state,
                                                 rs["batch"], rs["seq"], rs["pos0"], l, side)
            if rs is main:
                (w_up_b,) = casts
            rs["yh"], hg_fin = _hg_call(rs["act"], kin, lf_hi, lf_lo, rs["vi"], hg_norm_w[l][None], hg_tables,
                                        hg_state, rs["batch"], rs["seq"])
            if rs is meta:
                ret_state, hg_state = ret_fin[0], hg_fin[0]

        next_row = norm_mix_pre[min(l + 1, depth - 1)][None]
        post_row = norm_ffn_post[l][None]
        w_down_b = None
        for rs in (main,) if last else (main, meta):
            y = _merge_call(rs["yr"], rs["yh"], rs["gates"], w_ret_b, w_hg_b)
            side = () if rs is meta else (w_ffn_down,)
            h_mid, xn_ffn, casts = _out_call(y, w_out_b, rs["h"], norm_mix_post[l][None], norm_ffn_pre[l][None],
                                             l, side)
            if rs is main:
                (w_down_b,) = casts
            rs["h"], rs["xn"] = _ffn_call(xn_ffn, w_gate_b, w_up_b, w_down_b, h_mid, post_row, next_row, not last)
    return main["h"].reshape(batch, seq, d)
```

```python
import functools

import numpy as np
import jax
import jax.numpy as jnp
from jax import lax
from jax.experimental import pallas as pl
from jax.experimental.pallas import tpu as pltpu

N_META = 16
HEADS = 8
DK = 128
DV = 256
CHUNK = 128
PAD = CHUNK - N_META
RMS_EPS = 1e-6
ROPE_BASE = 10000.0
LEVELS = 7
SUBLANES = 8
SMALL_LEVELS = 3
BF16_SUBLANES = 16
QK_W = HEADS * DK
V_W = HEADS * DV

VMEM_LIMIT_BYTES = 58 * 1024 * 1024
ROW_TILE = 1024
IN_TILE = 1024
PROJ_SLAB = 512
PROJ_VMEM_BUDGET = 54 * 1024 * 1024
HEADS_PER_STEP = 4
MIXER_ROWS = 2048
MIXER_UNROLL = 4
FFN_ROW_TILE = 512
FFN_COL_TILE = 512
MERGE_COL_TILE = 1024
XLU_TRANSPOSE_LEVELS = (4, 5, 6)
HG_GROUP = 4

_IN_WIDTHS = (QK_W, QK_W, V_W, V_W, QK_W, QK_W, V_W, V_W, V_W, V_W)
_IN_STARTS = tuple(sum(_IN_WIDTHS[:i]) // IN_TILE for i in range(len(_IN_WIDTHS)))


def _tiles(*parts):
    return tuple(t for p in parts for t in range(_IN_STARTS[p], _IN_STARTS[p] + _IN_WIDTHS[p] // IN_TILE))


ROT_TILES = _tiles(0, 1)
IDENT_TILES = _tiles(2, 6)
SILU_TILES = _tiles(3, 4, 7)
FORGET_TILES = _tiles(5)
SIGMOID_TILES = _tiles(8, 9)

F32 = jnp.float32
BF16 = jnp.bfloat16
NT_DIMS = (((1,), (1,)), ((), ()))
TN_DIMS = (((0,), (0,)), ((), ()))


def _params(*semantics, flags=None):
    return pltpu.CompilerParams(dimension_semantics=semantics, vmem_limit_bytes=VMEM_LIMIT_BYTES, flags=flags)


MIXER_FLAGS = None


def _sigmoid(x):
    return 0.5 + 0.5 * jnp.tanh(0.5 * x)


def _silu(x):
    h = 0.5 * x
    return h + h * jnp.tanh(h)


def _rms_scale(x):
    return x * lax.rsqrt(jnp.mean(x * x, axis=-1, keepdims=True) + RMS_EPS)


def _tile_lookup(tiles):
    def lookup(j):
        out = tiles[-1]
        for idx in range(len(tiles) - 2, -1, -1):
            out = jnp.where(j == idx, tiles[idx], out)
        return out
    return lookup


def _side_cast_specs(arrays, layer, nsteps, step_of):
    in_specs, out_specs, out_shape = [], [], []
    for a in arrays:
        _, r, c = a.shape
        assert r % (nsteps * BF16_SUBLANES) == 0, (a.shape, nsteps)
        br = r // nsteps
        in_specs.append(pl.BlockSpec((None, br, c), lambda *g: (layer, step_of(*g), 0)))
        out_specs.append(pl.BlockSpec((br, c), lambda *g: (step_of(*g), 0)))
        out_shape.append(jax.ShapeDtypeStruct((r, c), BF16))
    return in_specs, out_specs, out_shape


def _side_cast(side_in, side_out):
    for i_ref, o_ref in zip(side_in, side_out):
        o_ref[...] = i_ref[...].astype(BF16)


def _norm_kernel(h_ref, w_ref, o_ref):
    o_ref[...] = (_rms_scale(h_ref[...]) * w_ref[...]).astype(o_ref.dtype)


def _norm_call(h, w_row):
    rows, d = h.shape
    tm = min(rows, ROW_TILE)
    return pl.pallas_call(
        _norm_kernel,
        grid=(rows // tm,),
        in_specs=[pl.BlockSpec((tm, d), lambda i: (i, 0)), pl.BlockSpec((1, d), lambda i: (0, 0))],
        out_specs=pl.BlockSpec((tm, d), lambda i: (i, 0)),
        out_shape=jax.ShapeDtypeStruct((rows, d), BF16),
        compiler_params=_params("parallel"),
        name="rms_norm",
    )(h, w_row)


def _proj_epilogue(acc, mode, extras, outs, r, first_pos):
    if mode == "ident":
        outs[0][r, :] = acc.astype(BF16)
    elif mode == "silu":
        outs[0][r, :] = _silu(acc).astype(BF16)
    elif mode == "sigmoid":
        outs[0][r, :] = _sigmoid(acc).astype(BF16)
    elif mode == "rot":
        cos, sin = extras[0][r, :], extras[1][r, :]
        for g in range(IN_TILE // DK):
            xg = acc[:, g * DK:(g + 1) * DK]
            outs[0][r, g * DK:(g + 1) * DK] = (xg * cos + pltpu.roll(xg, DK // 2, 1) * sin).astype(BF16)
    elif mode == "forget":
        kin_ref, hi_ref, lo_ref = outs
        one_minus_f = (1.0 - extras[0][...]) * (0.5 - 0.5 * jnp.tanh(0.5 * acc))
        log_f = jnp.log2(1.0 - one_minus_f)
        if first_pos is not None:
            valid = (lax.broadcasted_iota(jnp.int32, (acc.shape[0], 1), 0) + first_pos) >= 0
            log_f = jnp.where(valid, log_f, 0.0)
            one_minus_f = jnp.where(valid, one_minus_f, 0.0)
        kin_ref[r, :] = one_minus_f.astype(BF16)
        hi = log_f.astype(BF16)
        hi_ref[r, :] = hi
        lo_ref[r, :] = (log_f - hi.astype(F32)).astype(BF16)


def _proj_kernel(*refs, mode, with_meta, n_side):
    n_extra = {"rot": 2, "forget": 1}.get(mode, 0)
    n_out = 3 if mode == "forget" else 1
    x_ref, w_ref = refs[:2]
    extras = refs[2:2 + n_extra]
    pos = 2 + n_extra
    if with_meta:
        xm_ref = refs[pos]
        n_meta_extra = 2 if mode == "rot" else 0
        meta_extras = refs[pos + 1:pos + 1 + n_meta_extra] if n_meta_extra else extras
        pos += 1 + n_meta_extra
    side_in = refs[pos:pos + n_side]
    pos += n_side
    outs = refs[pos:pos + n_out]
    pos += n_out
    if with_meta:
        meta_outs = refs[pos:pos + n_out]
        pos += n_out
    _side_cast(side_in, refs[pos:pos + n_side])
    wb_ref = refs[-1]

    @pl.when(pl.program_id(1) == 0)
    def _():
        wb_ref[...] = w_ref[...].astype(BF16)
        if with_meta:
            acc = jnp.dot(xm_ref[...], wb_ref[...], preferred_element_type=F32)
            _proj_epilogue(acc, mode, meta_extras, meta_outs, slice(None), -PAD)

    tm = x_ref.shape[0]
    slab = min(tm, PROJ_SLAB)
    pending = None
    for s0 in range(0, tm, slab):
        r = slice(s0, s0 + slab)
        acc = jnp.dot(x_ref[r, :], wb_ref[...], preferred_element_type=F32)
        if pending is not None:
            _proj_epilogue(*pending)
        pending = (acc, mode, extras, outs, r, None)
    _proj_epilogue(*pending)


def _proj_row_tile(rows, k, n_out, side_bytes):
    for tm in (2 * ROW_TILE, ROW_TILE):
        if tm > rows or rows % tm:
            continue
        x_bytes = 2 * tm * k * 2
        w_bytes = 2 * k * IN_TILE * 4 + k * IN_TILE * 2
        out_bytes = n_out * 2 * tm * IN_TILE * 2
        acc_bytes = 2 * PROJ_SLAB * IN_TILE * 4
        meta_bytes = 2 * CHUNK * (k + n_out * IN_TILE) * 2
        side = 2 * (side_bytes + side_bytes // 2) // (rows // tm)
        if x_bytes + w_bytes + out_bytes + acc_bytes + meta_bytes + side <= PROJ_VMEM_BUDGET:
            return tm
    return min(rows, ROW_TILE)


def _proj_call(xn, xn_meta, w_in, layer, tiles, mode, seq, extra=(), meta_extra=(), side=()):
    rows, k = xn.shape
    n_out = 3 if mode == "forget" else 1
    with_meta = xn_meta is not None
    side_bytes = sum(a.shape[1] * a.shape[2] * 4 for a in side) // len(tiles)
    tm = _proj_row_tile(rows, k, n_out, side_bytes)
    n_row_tiles = rows // tm
    lookup = _tile_lookup(tiles)
    out_cols = len(tiles) * IN_TILE
    in_specs = [pl.BlockSpec((tm, k), lambda j, i: (i, 0)),
                pl.BlockSpec((None, k, IN_TILE), lambda j, i: (layer, 0, lookup(j)))]
    if mode == "rot":
        blocks_per_seq = seq // tm
        tab = pl.BlockSpec((None, tm, DK), lambda j, i: (j, i % blocks_per_seq, 0))
        in_specs += [tab, tab]
    elif mode == "forget":
        in_specs += [pl.BlockSpec((1, IN_TILE), lambda j, i: (0, 0))]
    operands = [xn, w_in, *extra]
    out_specs = [pl.BlockSpec((tm, IN_TILE), lambda j, i: (i, j))] * n_out
    out_shape = [jax.ShapeDtypeStruct((rows, out_cols), BF16)] * n_out
    if with_meta:
        in_specs += [pl.BlockSpec((CHUNK, k), lambda j, i: (0, 0))]
        operands += [xn_meta]
        if mode == "rot":
            in_specs += [pl.BlockSpec((None, CHUNK, DK), lambda j, i: (j, 0, 0))] * 2
            operands += list(meta_extra)
        out_specs += [pl.BlockSpec((CHUNK, IN_TILE), lambda j, i: (0, j))] * n_out
        out_shape += [jax.ShapeDtypeStruct((CHUNK, out_cols), BF16)] * n_out
    side_in, side_out, side_shape = _side_cast_specs(side, layer, len(tiles) * n_row_tiles,
                                                     lambda j, i: j * n_row_tiles + i)
    res = pl.pallas_call(
        functools.partial(_proj_kernel, mode=mode, with_meta=with_meta, n_side=len(side)),
        grid=(len(tiles), n_row_tiles),
        in_specs=in_specs + side_in,
        out_specs=out_specs + side_out,
        out_shape=out_shape + side_shape,
        scratch_shapes=[pltpu.VMEM((k, IN_TILE), BF16)],
        compiler_params=_params("parallel", "arbitrary"),
        name="in_proj_" + mode,
    )(*operands, *side)
    main = res[:n_out]
    meta = res[n_out:2 * n_out] if with_meta else None
    casts = tuple(res[len(res) - len(side):]) if side else ()
    if n_out == 1:
        main, meta = main[0], (meta[0] if with_meta else None)
    return main, meta, casts


def _rot_tables(seq, pos0):
    half = DK // 2
    inv = ROPE_BASE ** (-jnp.arange(half, dtype=F32) / half)
    pos = jnp.arange(seq, dtype=jnp.int32) + pos0
    ang = pos.astype(F32)[:, None] * inv[None, :]
    cos, sin = jnp.cos(ang), jnp.sin(ang)
    cos2 = jnp.concatenate([cos, cos], axis=1)
    sin2 = jnp.concatenate([-sin, sin], axis=1)
    scale = DK ** -0.5
    return jnp.stack([cos2, cos2 * scale]), jnp.stack([sin2, sin2 * scale])


def _ret_kernel(q_ref, k_ref, v_ref, g_ref, dmat_ref, qdec_ref, kdec_ref, cdec_ref, s0_ref, *rest,
                hb, nchunks, pos0, n_side):
    o_ref, sfin_ref = rest[n_side:n_side + 2]
    s_ref = rest[-1]
    _side_cast(rest[:n_side], rest[n_side + 2:-1])

    @pl.when(pl.program_id(2) == 0)
    def _():
        s_ref[...] = s0_ref[...]

    def body(c, carry):
        r0 = pl.multiple_of(c * CHUNK, CHUNK)
        rows = pl.ds(r0, CHUNK)
        if pos0 < 0:
            valid = (lax.broadcasted_iota(jnp.int32, (CHUNK, 1), 0) + (r0 + pos0)) >= 0
        heads = range(hb)
        q = [q_ref[rows, j * DK:(j + 1) * DK] for j in heads]
        k = [k_ref[rows, j * DK:(j + 1) * DK] for j in heads]
        if pos0 < 0:
            k = [jnp.where(valid, kj, jnp.zeros_like(kj)) for kj in k]
        v = [v_ref[rows, j * DV:(j + 1) * DV] for j in heads]
        scores = [jnp.dot(q[j], k[j].astype(F32).T.astype(BF16), preferred_element_type=F32) for j in heads]
        s = [s_ref[j] for j in heads]
        new_s = [lax.dot_general(k[j] * kdec_ref[j], v[j], TN_DIMS, preferred_element_type=F32) for j in heads]
        y = [jnp.dot(jnp.concatenate([(scores[j] * dmat_ref[j]).astype(BF16), q[j] * qdec_ref[j]], axis=1),
                     jnp.concatenate([v[j], s[j].astype(BF16)], axis=0), preferred_element_type=F32)
             for j in heads]
        for j in heads:
            s_ref[j] = s[j] * cdec_ref[j] + new_s[j]
            g = g_ref[rows, j * DV:(j + 1) * DV].astype(F32)
            o_ref[rows, j * DV:(j + 1) * DV] = (_rms_scale(y[j]) * g).astype(o_ref.dtype)
        return carry

    lax.fori_loop(0, nchunks, body, 0, unroll=MIXER_UNROLL)
    sfin_ref[...] = s_ref[...]


def _ret_tables():
    log_g = jnp.log1p(-jnp.exp2(-5.0 - jnp.arange(HEADS, dtype=F32)))
    idx = jnp.arange(CHUNK, dtype=F32)
    diff = idx[:, None] - idx[None, :]
    dmat = jnp.where(diff[None] >= 0, jnp.exp(jnp.maximum(diff, 0.0)[None] * log_g[:, None, None]), 0.0)
    qdec = jnp.exp((idx + 1)[None, :] * log_g[:, None])
    kdec = jnp.exp((CHUNK - 1 - idx)[None, :] * log_g[:, None])
    cdec = jnp.exp(CHUNK * log_g)
    qdec = jnp.broadcast_to(qdec[:, :, None], (HEADS, CHUNK, DK)).astype(BF16)
    kdec = jnp.broadcast_to(kdec[:, :, None], (HEADS, CHUNK, DK)).astype(BF16)
    cdec = jnp.broadcast_to(cdec[:, None, None], (HEADS, 1, DV))
    return dmat, qdec, kdec, cdec


def _mixer_grid(batch, seq):
    rb = min(seq, MIXER_ROWS)
    return rb, seq // rb


def _ret_call(qk, vi, act, tables, s0, batch, seq, pos0, layer=None, side=()):
    dmat, qdec, kdec, cdec = tables
    hb = HEADS_PER_STEP
    rb, nblk = _mixer_grid(batch, seq)
    k_blk = QK_W // (hb * DK)
    kern = functools.partial(_ret_kernel, hb=hb, nchunks=rb // CHUNK, pos0=pos0, n_side=len(side))
    per_head = lambda b, h, r: (h, 0, 0)
    n_hg = HEADS // hb
    side_in, side_out, side_shape = _side_cast_specs(side, layer, batch * n_hg * nblk,
                                                     lambda b, h, r: (b * n_hg + h) * nblk + r)
    res = pl.pallas_call(
        kern,
        grid=(batch, n_hg, nblk),
        in_specs=[
            pl.BlockSpec((rb, hb * DK), lambda b, h, r: (b * nblk + r, h)),
            pl.BlockSpec((rb, hb * DK), lambda b, h, r: (b * nblk + r, k_blk + h)),
            pl.BlockSpec((rb, hb * DV), lambda b, h, r: (b * nblk + r, h)),
            pl.BlockSpec((rb, hb * DV), lambda b, h, r: (b * nblk + r, h)),
            pl.BlockSpec((hb, CHUNK, CHUNK), per_head),
            pl.BlockSpec((hb, CHUNK, DK), per_head),
            pl.BlockSpec((hb, CHUNK, DK), per_head),
            pl.BlockSpec((hb, 1, DV), per_head),
            pl.BlockSpec((hb, DK, DV), per_head),
        ] + side_in,
        out_specs=[
            pl.BlockSpec((rb, hb * DV), lambda b, h, r: (b * nblk + r, h)),
            pl.BlockSpec((None, hb, DK, DV), lambda b, h, r: (b, h, 0, 0)),
        ] + side_out,
        out_shape=[jax.ShapeDtypeStruct((batch * seq, V_W), BF16),
                   jax.ShapeDtypeStruct((batch, HEADS, DK, DV), F32)] + side_shape,
        scratch_shapes=[pltpu.VMEM((hb, DK, DV), F32)],
        compiler_params=_params("parallel", "parallel", "arbitrary", flags=MIXER_FLAGS),
        name="retention",
    )(qk, qk, vi, act, dmat, qdec, kdec, cdec, s0, *side)
    return res[0], res[1], tuple(res[2:])


def _hg_tables():
    t = np.arange(CHUNK)[:, None]
    u = np.arange(CHUNK)[None, :]
    mats, masks = [], []
    for j in range(LEVELS):
        m = 1 << j
        upper = ((t >> j) & 1) == 1
        q_part = upper & (u >= (t & ~(m - 1))) & (u <= t)
        k_part = (~upper) & (u > t) & (u <= (t | (m - 1)))
        if m < SUBLANES:
            mats.append(q_part | k_part)
        masks.append(((t >> (j + 1)) == (u >> (j + 1))) & upper & (((u >> j) & 1) == 0))
    mats.append(u <= t)
    masks.append(t == u)
    mstack = np.concatenate(mats, axis=0).astype(np.float32)
    mstack = np.concatenate([mstack, mstack], axis=1)
    return jnp.asarray(mstack, dtype=BF16), jnp.asarray(np.stack(masks).astype(np.float32))


def _dot_keys(queries, keys, level):
    if level in XLU_TRANSPOSE_LEVELS:
        return jnp.dot(queries, keys.T.astype(BF16), preferred_element_type=F32)
    return lax.dot_general(queries, keys.astype(BF16), NT_DIMS, preferred_element_type=F32)


def _hg_kernel(q_ref, kin_ref, hi_ref, lo_ref, v_ref, g_ref, nw_ref, mstack_ref, masks_ref, s0_ref,
               o_ref, sfin_ref, st_ref, e_ref, *, hb, nchunks):
    @pl.when(pl.program_id(2) == 0)
    def _():
        st_ref[...] = s0_ref[...]

    row = lax.broadcasted_iota(jnp.int32, (CHUNK, 1), 0)

    def exponents(c, slot, pair):
        rows = pl.ds(pl.multiple_of(c * CHUNK, CHUNK), CHUNK)
        lanes = slice(pair * 2 * DK, (pair + 1) * 2 * DK)
        pieces = jnp.concatenate([hi_ref[rows, lanes], lo_ref[rows, lanes]], axis=0)
        e_ref[slot, :, lanes] = jnp.dot(mstack_ref[...], pieces, preferred_element_type=F32)

    def process(c, slot, nxt):
        for g0 in range(0, hb, HG_GROUP):
            process_group(c, slot, nxt, range(g0, g0 + HG_GROUP))

    def process_group(c, slot, nxt, heads):
        rows = pl.ds(pl.multiple_of(c * CHUNK, CHUNK), CHUNK)
        expo = {j: e_ref.at[slot, :, j * DK:(j + 1) * DK] for j in heads}
        qb = {j: q_ref[rows, j * DK:(j + 1) * DK] for j in heads}
        q = {j: qb[j].astype(F32) for j in heads}
        kin = {j: kin_ref[rows, j * DK:(j + 1) * DK].astype(F32) for j in heads}
        b = {j: expo[j][SMALL_LEVELS * CHUNK:, :] for j in heads}
        a = {j: _dot_keys(qb[j], kin[j], LEVELS) * masks_ref[LEVELS] for j in heads}
        a_rows = None
        issue_at = {1 + 3 * i: p for i, p in enumerate(range(heads[0] // 2, heads[-1] // 2 + 1))}
        for lev in range(LEVELS):
            m = 1 << lev
            if nxt is not None and lev in issue_at:
                exponents(nxt, 1 - slot, issue_at[lev])
            if m < SUBLANES:
                for j in heads:
                    e = jnp.exp2(expo[j][lev * CHUNK:(lev + 1) * CHUNK, :])
                    x = jnp.where(((row >> lev) & 1) == 1, q[j], kin[j]) * e
                    a[j] = a[j] + _dot_keys(x.astype(BF16), x, lev) * masks_ref[lev]
            else:
                if a_rows is None:
                    a_rows = {j: [a[j][s:s + SUBLANES] for s in range(0, CHUNK, SUBLANES)] for j in heads}
                upper = [s for s0 in range(m, CHUNK, 2 * m) for s in range(s0, s0 + m, SUBLANES)]
                for j in heads:
                    parts = []
                    for s0 in range(0, CHUNK, 2 * m):
                        b_mid = jnp.broadcast_to(b[j][s0 + m - 1:s0 + m, :], (m, DK))
                        parts += [b_mid - b[j][s0:s0 + m], b[j][s0 + m:s0 + 2 * m] - b_mid]
                    e = jnp.exp2(jnp.concatenate(parts, axis=0))
                    qe = jnp.concatenate([q[j][s:s + SUBLANES] * e[s:s + SUBLANES] for s in upper], axis=0)
                    p = _dot_keys(qe.astype(BF16), kin[j] * e, lev)
                    for idx, s in enumerate(upper):
                        a_rows[j][s // SUBLANES] = (a_rows[j][s // SUBLANES]
                                                    + p[idx * SUBLANES:(idx + 1) * SUBLANES]
                                                    * masks_ref[lev, s:s + SUBLANES, :])
        total = {j: b[j][CHUNK - 1:CHUNK, :] for j in heads}
        v = {j: v_ref[rows, j * DV:(j + 1) * DV] for j in heads}
        st = {j: st_ref[j] for j in heads}
        new_st = {j: lax.dot_general((kin[j] * jnp.exp2(total[j] - b[j])).astype(BF16), v[j], TN_DIMS,
                                     preferred_element_type=F32) for j in heads}
        y = {j: jnp.dot(jnp.concatenate([jnp.concatenate(a_rows[j], axis=0).astype(BF16),
                                         (q[j] * jnp.exp2(b[j])).astype(BF16)], axis=1),
                        jnp.concatenate([v[j], st[j].astype(BF16)], axis=0), preferred_element_type=F32)
             for j in heads}
        for j in heads:
            decay = jnp.broadcast_to(jnp.exp2(total[j]), (CHUNK, DK)).T
            st_ref[j] = st[j] * jnp.concatenate([decay] * (DV // DK), axis=1) + new_st[j]
            g = g_ref[rows, j * DV:(j + 1) * DV].astype(F32)
            o_ref[rows, j * DV:(j + 1) * DV] = (_rms_scale(y[j]) * nw_ref[...] * g).astype(o_ref.dtype)

    for pr in range(hb // 2):
        exponents(0, 0, pr)
    if nchunks == 1:
        process(0, 0, None)
    else:
        assert nchunks % 2 == 0

        def pair(p, carry):
            c = 2 * p
            process(c, 0, c + 1)
            process(c + 1, 1, jnp.minimum(c + 2, nchunks - 1))
            return carry

        lax.fori_loop(0, nchunks // 2, pair, 0)
    sfin_ref[...] = st_ref[...]


def _hg_call(act, kin, lf_hi, lf_lo, vi, nw_row, tables, s0, batch, seq):
    mstack, masks = tables
    hb = HEADS_PER_STEP
    rb, nblk = _mixer_grid(batch, seq)
    q_blk = V_W // (hb * DK)
    g_blk = (V_W + QK_W) // (hb * DV)
    i_blk = V_W // (hb * DV)
    kern = functools.partial(_hg_kernel, hb=hb, nchunks=rb // CHUNK)
    row_blk = lambda off: (lambda b, h, r: (b * nblk + r, off + h))
    return pl.pallas_call(
        kern,
        grid=(batch, HEADS // hb, nblk),
        in_specs=[
            pl.BlockSpec((rb, hb * DK), row_blk(q_blk)),
            pl.BlockSpec((rb, hb * DK), row_blk(0)),
            pl.BlockSpec((rb, hb * DK), row_blk(0)),
            pl.BlockSpec((rb, hb * DK), row_blk(0)),
            pl.BlockSpec((rb, hb * DV), row_blk(i_blk)),
            pl.BlockSpec((rb, hb * DV), row_blk(g_blk)),
            pl.BlockSpec((1, DV), lambda b, h, r: (0, 0)),
            pl.BlockSpec(mstack.shape, lambda b, h, r: (0, 0)),
            pl.BlockSpec(masks.shape, lambda b, h, r: (0, 0, 0)),
            pl.BlockSpec((hb, DK, DV), lambda b, h, r: (h, 0, 0)),
        ],
        out_specs=[
            pl.BlockSpec((rb, hb * DV), row_blk(0)),
            pl.BlockSpec((None, hb, DK, DV), lambda b, h, r: (b, h, 0, 0)),
        ],
        out_shape=[jax.ShapeDtypeStruct((batch * seq, V_W), BF16),
                   jax.ShapeDtypeStruct((batch, HEADS, DK, DV), F32)],
        scratch_shapes=[pltpu.VMEM((hb, DK, DV), F32),
                        pltpu.VMEM((2, (SMALL_LEVELS + 1) * CHUNK, hb * DK), F32)],
        compiler_params=_params("parallel", "parallel", "arbitrary", flags=MIXER_FLAGS),
        name="hgrn2",
    )(act, kin, lf_hi, lf_lo, vi, act, nw_row, mstack, masks, s0)


def _merge_kernel(yr_ref, yh_ref, wr_ref, wh_ref, gr_ref, gh_ref, o_ref):
    a = jnp.dot(yr_ref[...], wr_ref[...], preferred_element_type=F32)
    b = jnp.dot(yh_ref[...], wh_ref[...], preferred_element_type=F32)
    o_ref[...] = (gr_ref[...].astype(F32) * a + gh_ref[...].astype(F32) * b).astype(o_ref.dtype)


def _merge_call(yr, yh, gates, w_ret, w_hg):
    rows, k = yr.shape
    d = w_ret.shape[-1]
    tm = min(rows, ROW_TILE)
    tn = MERGE_COL_TILE
    gh_blk = d // tn
    return pl.pallas_call(
        _merge_kernel,
        grid=(rows // tm, d // tn),
        in_specs=[
            pl.BlockSpec((tm, k), lambda i, j: (i, 0)),
            pl.BlockSpec((tm, k), lambda i, j: (i, 0)),
            pl.BlockSpec((k, tn), lambda i, j: (0, j)),
            pl.BlockSpec((k, tn), lambda i, j: (0, j)),
            pl.BlockSpec((tm, tn), lambda i, j: (i, j)),
            pl.BlockSpec((tm, tn), lambda i, j: (i, gh_blk + j)),
        ],
        out_specs=pl.BlockSpec((tm, tn), lambda i, j: (i, j)),
        out_shape=jax.ShapeDtypeStruct((rows, d), BF16),
        compiler_params=_params("parallel", "arbitrary"),
        name="branch_merge",
    )(yr, yh, w_ret, w_hg, gates, gates)


def _out_kernel(y_ref, w_ref, h_ref, post_ref, nxt_ref, *rest, n_side):
    hn_ref, xn_ref = rest[n_side:n_side + 2]
    _side_cast(rest[:n_side], rest[n_side + 2:])
    m = jnp.dot(y_ref[...], w_ref[...], preferred_element_type=F32)
    hn = h_ref[...] + _rms_scale(m) * post_ref[...]
    hn_ref[...] = hn
    xn_ref[...] = (_rms_scale(hn) * nxt_ref[...]).astype(xn_ref.dtype)


def _out_call(y, w_out, h, post_row, next_row, layer=None, side=()):
    rows, d = h.shape
    tm = min(rows, FFN_ROW_TILE)
    row_blk = pl.BlockSpec((tm, d), lambda i: (i, 0))
    vec_blk = pl.BlockSpec((1, d), lambda i: (0, 0))
    side_in, side_out, side_shape = _side_cast_specs(side, layer, rows // tm, lambda i: i)
    res = pl.pallas_call(
        functools.partial(_out_kernel, n_side=len(side)),
        grid=(rows // tm,),
        in_specs=[row_blk, pl.BlockSpec((d, d), lambda i: (0, 0)), row_blk, vec_blk, vec_blk] + side_in,
        out_specs=[row_blk, row_blk] + side_out,
        out_shape=[jax.ShapeDtypeStruct((rows, d), F32), jax.ShapeDtypeStruct((rows, d), BF16)] + side_shape,
        compiler_params=_params("parallel"),
        name="out_proj",
    )(y, w_out, h, post_row, next_row, *side)
    return res[0], res[1], tuple(res[2:])


def _ffn_kernel(x_ref, wg_ref, wu_ref, wd_ref, h_ref, post_ref, nxt_ref, hn_ref, *rest, with_next):
    if with_next:
        xn_ref, acc_ref = rest
    else:
        (acc_ref,) = rest
    f = pl.program_id(1)

    @pl.when(f == 0)
    def _():
        acc_ref[...] = jnp.zeros_like(acc_ref)

    x = x_ref[...]
    g = jnp.dot(x, wg_ref[...], preferred_element_type=F32)
    u = jnp.dot(x, wu_ref[...], preferred_element_type=F32)
    act = (_silu(g) * u).astype(BF16)
    acc_ref[...] += jnp.dot(act, wd_ref[...], preferred_element_type=F32)

    @pl.when(f == pl.num_programs(1) - 1)
    def _():
        hn = h_ref[...] + _rms_scale(acc_ref[...]) * post_ref[...]
        hn_ref[...] = hn
        if with_next:
            xn_ref[...] = (_rms_scale(hn) * nxt_ref[...]).astype(xn_ref.dtype)


def _ffn_call(xn, w_gate, w_up, w_down, h, post_row, next_row, with_next):
    rows, d = h.shape
    d_ff = w_gate.shape[-1]
    tm = min(rows, FFN_ROW_TILE)
    tf = FFN_COL_TILE
    row_blk = pl.BlockSpec((tm, d), lambda i, f: (i, 0))
    vec_blk = pl.BlockSpec((1, d), lambda i, f: (0, 0))
    n_state = 2 if with_next else 1
    res = pl.pallas_call(
        functools.partial(_ffn_kernel, with_next=with_next),
        grid=(rows // tm, d_ff // tf),
        in_specs=[
            row_blk,
            pl.BlockSpec((d, tf), lambda i, f: (0, f)),
            pl.BlockSpec((d, tf), lambda i, f: (0, f)),
            pl.BlockSpec((tf, d), lambda i, f: (f, 0)),
            row_blk, vec_blk, vec_blk,
        ],
        out_specs=[row_blk] * n_state,
        out_shape=[jax.ShapeDtypeStruct((rows, d), F32), jax.ShapeDtypeStruct((rows, d), BF16)][:n_state],
        scratch_shapes=[pltpu.VMEM((tm, d), F32)],
        compiler_params=_params("parallel", "arbitrary"),
        name="swiglu_ffn",
    )(xn, w_gate, w_up, w_down, h, post_row, next_row)
    return res[0], (res[1] if with_next else None)


def kernel(x, meta_tokens, norm_mix_pre, norm_mix_post, norm_ffn_pre, norm_ffn_post, w_in, hg_lb_logits,
           hg_norm_w, w_br_ret, w_br_hg, w_out, w_ffn_gate, w_ffn_up, w_ffn_down):
    batch, seq, d = x.shape
    depth = w_in.shape[0]
    assert seq % CHUNK == 0 and meta_tokens.shape == (N_META, d)
    assert w_in.shape[-1] == sum(_IN_WIDTHS) and d == V_W

    lb_sm = jax.nn.softmax(hg_lb_logits.astype(F32), axis=0)
    lbs = jnp.cumsum(lb_sm, axis=0) - lb_sm[0:1]

    hg_tables = _hg_tables()
    ret_tables = _ret_tables()
    meta_h = jnp.concatenate([jnp.zeros((PAD, d), F32), meta_tokens.astype(F32)], axis=0)
    meta = dict(h=meta_h, batch=1, seq=CHUNK, pos0=-PAD)
    main = dict(h=x.reshape(batch * seq, d).astype(F32), batch=batch, seq=seq, pos0=N_META)
    for rs in (meta, main):
        rs["rot"] = _rot_tables(rs["seq"], rs["pos0"])
        rs["xn"] = _norm_call(rs["h"], norm_mix_pre[0][None])

    for l in range(depth):
        last = l == depth - 1
        proj = functools.partial(_proj_call, main["xn"], meta["xn"], w_in, l, seq=seq)
        main["qk"], meta["qk"], _ = proj(ROT_TILES, "rot", extra=main["rot"], meta_extra=meta["rot"])
        main["vi"], meta["vi"], (w_ret_b, w_hg_b, w_out_b) = proj(IDENT_TILES, "ident",
                                                                 side=(w_br_ret, w_br_hg, w_out))
        main["act"], meta["act"], _ = proj(SILU_TILES, "silu")
        main["forget"], meta["forget"], _ = proj(FORGET_TILES, "forget", extra=(lbs[l][None],))
        main["gates"], meta["gates"], (w_gate_b,) = _proj_call(
            main["xn"], None if last else meta["xn"], w_in, l, SIGMOID_TILES, "sigmoid", seq, side=(w_ffn_gate,))

        ret_state = jnp.zeros((HEADS, DK, DV), F32)
        hg_state = jnp.zeros((HEADS, DK, DV), F32)
        for rs in (meta, main):
            kin, lf_hi, lf_lo = rs["forget"]
            side = () if rs is meta else (w_ffn_up,)
            rs["yr"], ret_fin, casts = _ret_call(rs["qk"], rs["vi"], rs["act"], ret_tables, ret_state,
                                                 rs["batch"], rs["seq"], rs["pos0"], l, side)
            if rs is main:
                (w_up_b,) = casts
            rs["yh"], hg_fin = _hg_call(rs["act"], kin, lf_hi, lf_lo, rs["vi"], hg_norm_w[l][None], hg_tables,
                                        hg_state, rs["batch"], rs["seq"])
            if rs is meta:
                ret_state, hg_state = ret_fin[0], hg_fin[0]

        next_row = norm_mix_pre[min(l + 1, depth - 1)][None]
        post_row = norm_ffn_post[l][None]
        w_down_b = None
        for rs in (main,) if last else (main, meta):
            y = _merge_call(rs["yr"], rs["yh"], rs["gates"], w_ret_b, w_hg_b)
            side = () if rs is meta else (w_ffn_down,)
            h_mid, xn_ffn, casts = _out_call(y, w_out_b, rs["h"], norm_mix_post[l][None], norm_ffn_pre[l][None],
                                             l, side)
            if rs is main:
                (w_down_b,) = casts
            rs["h"], rs["xn"] = _ffn_call(xn_ffn, w_gate_b, w_up_b, w_down_b, h_mid, post_row, next_row, not last)
    return main["h"].reshape(batch, seq, d)
```

```python
import functools

import numpy as np
import jax
import jax.numpy as jnp
from jax import lax
from jax.experimental import pallas as pl
from jax.experimental.pallas import tpu as pltpu

N_META = 16
HEADS = 8
DK = 128
DV = 256
CHUNK = 128
PAD = CHUNK - N_META
RMS_EPS = 1e-6
ROPE_BASE = 10000.0
LEVELS = 7
SUBLANES = 8
SMALL_LEVELS = 3
BF16_SUBLANES = 16
QK_W = HEADS * DK
V_W = HEADS * DV

VMEM_LIMIT_BYTES = 58 * 1024 * 1024
ROW_TILE = 1024
IN_TILE = 1024
PROJ_SLAB = 512
PROJ_VMEM_BUDGET = 54 * 1024 * 1024
HEADS_PER_STEP = 4
MIXER_ROWS = 2048
MIXER_UNROLL = 4
FFN_ROW_TILE = 512
FFN_COL_TILE = 512
MERGE_COL_TILE = 1024
XLU_TRANSPOSE_LEVELS = (4, 5, 6)
HG_GROUP = 4

_IN_WIDTHS = (QK_W, QK_W, V_W, V_W, QK_W, QK_W, V_W, V_W, V_W, V_W)
_IN_STARTS = tuple(sum(_IN_WIDTHS[:i]) // IN_TILE for i in range(len(_IN_WIDTHS)))


def _tiles(*parts):
    return tuple(t for p in parts for t in range(_IN_STARTS[p], _IN_STARTS[p] + _IN_WIDTHS[p] // IN_TILE))


ROT_TILES = _tiles(0, 1)
IDENT_TILES = _tiles(2, 6)
SILU_TILES = _tiles(3, 4, 7)
FORGET_TILES = _tiles(5)
SIGMOID_TILES = _tiles(8, 9)

F32 = jnp.float32
BF16 = jnp.bfloat16
NT_DIMS = (((1,), (1,)), ((), ()))
TN_DIMS = (((0,), (0,)), ((), ()))


def _params(*semantics, flags=None):
    return pltpu.CompilerParams(dimension_semantics=semantics, vmem_limit_bytes=VMEM_LIMIT_BYTES, flags=flags)


MIXER_FLAGS = None


def _sigmoid(x):
    return 0.5 + 0.5 * jnp.tanh(0.5 * x)


def _silu(x):
    h = 0.5 * x
    return h + h * jnp.tanh(h)


def _rms_scale(x):
    return x * lax.rsqrt(jnp.mean(x * x, axis=-1, keepdims=True) + RMS_EPS)


def _tile_lookup(tiles):
    def lookup(j):
        out = tiles[-1]
        for idx in range(len(tiles) - 2, -1, -1):
            out = jnp.where(j == idx, tiles[idx], out)
        return out
    return lookup


def _side_cast_specs(arrays, layer, nsteps, step_of):
    in_specs, out_specs, out_shape = [], [], []
    for a in arrays:
        _, r, c = a.shape
        assert r % (nsteps * BF16_SUBLANES) == 0, (a.shape, nsteps)
        br = r // nsteps
        in_specs.append(pl.BlockSpec((None, br, c), lambda *g: (layer, step_of(*g), 0)))
        out_specs.append(pl.BlockSpec((br, c), lambda *g: (step_of(*g), 0)))
        out_shape.append(jax.ShapeDtypeStruct((r, c), BF16))
    return in_specs, out_specs, out_shape


def _side_cast(side_in, side_out):
    for i_ref, o_ref in zip(side_in, side_out):
        o_ref[...] = i_ref[...].astype(BF16)


def _norm_kernel(h_ref, w_ref, o_ref):
    o_ref[...] = (_rms_scale(h_ref[...]) * w_ref[...]).astype(o_ref.dtype)


def _norm_call(h, w_row):
    rows, d = h.shape
    tm = min(rows, ROW_TILE)
    return pl.pallas_call(
        _norm_kernel,
        grid=(rows // tm,),
        in_specs=[pl.BlockSpec((tm, d), lambda i: (i, 0)), pl.BlockSpec((1, d), lambda i: (0, 0))],
        out_specs=pl.BlockSpec((tm, d), lambda i: (i, 0)),
        out_shape=jax.ShapeDtypeStruct((rows, d), BF16),
        compiler_params=_params("parallel"),
        name="rms_norm",
    )(h, w_row)


def _proj_epilogue(acc, mode, extras, outs, r, first_pos):
    if mode == "ident":
        outs[0][r, :] = acc.astype(BF16)
    elif mode == "silu":
        outs[0][r, :] = _silu(acc).astype(BF16)
    elif mode == "sigmoid":
        outs[0][r, :] = _sigmoid(acc).astype(BF16)
    elif mode == "rot":
        cos, sin = extras[0][r, :], extras[1][r, :]
        for g in range(IN_TILE // DK):
            xg = acc[:, g * DK:(g + 1) * DK]
            outs[0][r, g * DK:(g + 1) * DK] = (xg * cos + pltpu.roll(xg, DK // 2, 1) * sin).astype(BF16)
    elif mode == "forget":
        kin_ref, hi_ref, lo_ref = outs
        one_minus_f = (1.0 - extras[0][...]) * (0.5 - 0.5 * jnp.tanh(0.5 * acc))
        log_f = jnp.log2(1.0 - one_minus_f)
        if first_pos is not None:
            valid = (lax.broadcasted_iota(jnp.int32, (acc.shape[0], 1), 0) + first_pos) >= 0
            log_f = jnp.where(valid, log_f, 0.0)
            one_minus_f = jnp.where(valid, one_minus_f, 0.0)
        kin_ref[r, :] = one_minus_f.astype(BF16)
        hi = log_f.astype(BF16)
        hi_ref[r, :] = hi
        lo_ref[r, :] = (log_f - hi.astype(F32)).astype(BF16)


def _proj_kernel(*refs, mode, with_meta, n_side):
    n_extra = {"rot": 2, "forget": 1}.get(mode, 0)
    n_out = 3 if mode == "forget" else 1
    x_ref, w_ref = refs[:2]
    extras = refs[2:2 + n_extra]
    pos = 2 + n_extra
    if with_meta:
        xm_ref = refs[pos]
        n_meta_extra = 2 if mode == "rot" else 0
        meta_extras = refs[pos + 1:pos + 1 + n_meta_extra] if n_meta_extra else extras
        pos += 1 + n_meta_extra
    side_in = refs[pos:pos + n_side]
    pos += n_side
    outs = refs[pos:pos + n_out]
    pos += n_out
    if with_meta:
        meta_outs = refs[pos:pos + n_out]
        pos += n_out
    _side_cast(side_in, refs[pos:pos + n_side])
    wb_ref = refs[-1]

    @pl.when(pl.program_id(1) == 0)
    def _():
        wb_ref[...] = w_ref[...].astype(BF16)
        if with_meta:
            acc = jnp.dot(xm_ref[...], wb_ref[...], preferred_element_type=F32)
            _proj_epilogue(acc, mode, meta_extras, meta_outs, slice(None), -PAD)

    tm = x_ref.shape[0]
    slab = min(tm, PROJ_SLAB)
    pending = None
    for s0 in range(0, tm, slab):
        r = slice(s0, s0 + slab)
        acc = jnp.dot(x_ref[r, :], wb_ref[...], preferred_element_type=F32)
        if pending is not None:
            _proj_epilogue(*pending)
        pending = (acc, mode, extras, outs, r, None)
    _proj_epilogue(*pending)


def _proj_row_tile(rows, k, n_out, side_bytes):
    for tm in (2 * ROW_TILE, ROW_TILE):
        if tm > rows or rows % tm:
            continue
        x_bytes = 2 * tm * k * 2
        w_bytes = 2 * k * IN_TILE * 4 + k * IN_TILE * 2
        out_bytes = n_out * 2 * tm * IN_TILE * 2
        acc_bytes = 2 * PROJ_SLAB * IN_TILE * 4
        meta_bytes = 2 * CHUNK * (k + n_out * IN_TILE) * 2
        side = 2 * (side_bytes + side_bytes // 2) // (rows // tm)
        if x_bytes + w_bytes + out_bytes + acc_bytes + meta_bytes + side <= PROJ_VMEM_BUDGET:
            return tm
    return min(rows, ROW_TILE)


def _proj_call(xn, xn_meta, w_in, layer, tiles, mode, seq, extra=(), meta_extra=(), side=()):
    rows, k = xn.shape
    n_out = 3 if mode == "forget" else 1
    with_meta = xn_meta is not None
    side_bytes = sum(a.shape[1] * a.shape[2] * 4 for a in side) // len(tiles)
    tm = _proj_row_tile(rows, k, n_out, side_bytes)
    n_row_tiles = rows // tm
    lookup = _tile_lookup(tiles)
    out_cols = len(tiles) * IN_TILE
    in_specs = [pl.BlockSpec((tm, k), lambda j, i: (i, 0)),
                pl.BlockSpec((None, k, IN_TILE), lambda j, i: (layer, 0, lookup(j)))]
    if mode == "rot":
        blocks_per_seq = seq // tm
        tab = pl.BlockSpec((None, tm, DK), lambda j, i: (j, i % blocks_per_seq, 0))
        in_specs += [tab, tab]
    elif mode == "forget":
        in_specs += [pl.BlockSpec((1, IN_TILE), lambda j, i: (0, 0))]
    operands = [xn, w_in, *extra]
    out_specs = [pl.BlockSpec((tm, IN_TILE), lambda j, i: (i, j))] * n_out
    out_shape = [jax.ShapeDtypeStruct((rows, out_cols), BF16)] * n_out
    if with_meta:
        in_specs += [pl.BlockSpec((CHUNK, k), lambda j, i: (0, 0))]
        operands += [xn_meta]
        if mode == "rot":
            in_specs += [pl.BlockSpec((None, CHUNK, DK), lambda j, i: (j, 0, 0))] * 2
            operands += list(meta_extra)
        out_specs += [pl.BlockSpec((CHUNK, IN_TILE), lambda j, i: (0, j))] * n_out
        out_shape += [jax.ShapeDtypeStruct((CHUNK, out_cols), BF16)] * n_out
    side_in, side_out, side_shape = _side_cast_specs(side, layer, len(tiles) * n_row_tiles,
                                                     lambda j, i: j * n_row_tiles + i)
    res = pl.pallas_call(
        functools.partial(_proj_kernel, mode=mode, with_meta=with_meta, n_side=len(side)),
        grid=(len(tiles), n_row_tiles),
        in_specs=in_specs + side_in,
        out_specs=out_specs + side_out,
        out_shape=out_shape + side_shape,
        scratch_shapes=[pltpu.VMEM((k, IN_TILE), BF16)],
        compiler_params=_params("parallel", "arbitrary"),
        name="in_proj_" + mode,
    )(*operands, *side)
    main = res[:n_out]
    meta = res[n_out:2 * n_out] if with_meta else None
    casts = tuple(res[len(res) - len(side):]) if side else ()
    if n_out == 1:
        main, meta = main[0], (meta[0] if with_meta else None)
    return main, meta, casts


def _rot_tables(seq, pos0):
    half = DK // 2
    inv = ROPE_BASE ** (-jnp.arange(half, dtype=F32) / half)
    pos = jnp.arange(seq, dtype=jnp.int32) + pos0
    ang = pos.astype(F32)[:, None] * inv[None, :]
    cos, sin = jnp.cos(ang), jnp.sin(ang)
    cos2 = jnp.concatenate([cos, cos], axis=1)
    sin2 = jnp.concatenate([-sin, sin], axis=1)
    scale = DK ** -0.5
    return jnp.stack([cos2, cos2 * scale]), jnp.stack([sin2, sin2 * scale])


def _ret_kernel(q_ref, k_ref, v_ref, g_ref, dmat_ref, qdec_ref, kdec_ref, cdec_ref, s0_ref, *rest,
                hb, nchunks, pos0, n_side):
    o_ref, sfin_ref = rest[n_side:n_side + 2]
    s_ref = rest[-1]
    _side_cast(rest[:n_side], rest[n_side + 2:-1])

    @pl.when(pl.program_id(2) == 0)
    def _():
        s_ref[...] = s0_ref[...]

    def body(c, carry):
        r0 = pl.multiple_of(c * CHUNK, CHUNK)
        rows = pl.ds(r0, CHUNK)
        if pos0 < 0:
            valid = (lax.broadcasted_iota(jnp.int32, (CHUNK, 1), 0) + (r0 + pos0)) >= 0
        heads = range(hb)
        q = [q_ref[rows, j * DK:(j + 1) * DK] for j in heads]
        k = [k_ref[rows, j * DK:(j + 1) * DK] for j in heads]
        if pos0 < 0:
            k = [jnp.where(valid, kj, jnp.zeros_like(kj)) for kj in k]
        v = [v_ref[rows, j * DV:(j + 1) * DV] for j in heads]
        scores = [jnp.dot(q[j], k[j].astype(F32).T.astype(BF16), preferred_element_type=F32) for j in heads]
        s = [s_ref[j] for j in heads]
        new_s = [lax.dot_general(k[j] * kdec_ref[j], v[j], TN_DIMS, preferred_element_type=F32) for j in heads]
        y = [jnp.dot(jnp.concatenate([(scores[j] * dmat_ref[j]).astype(BF16), q[j] * qdec_ref[j]], axis=1),
                     jnp.concatenate([v[j], s[j].astype(BF16)], axis=0), preferred_element_type=F32)
             for j in heads]
        for j in heads:
            s_ref[j] = s[j] * cdec_ref[j] + new_s[j]
            g = g_ref[rows, j * DV:(j + 1) * DV].astype(F32)
            o_ref[rows, j * DV:(j + 1) * DV] = (_rms_scale(y[j]) * g).astype(o_ref.dtype)
        return carry

    lax.fori_loop(0, nchunks, body, 0, unroll=MIXER_UNROLL)
    sfin_ref[...] = s_ref[...]


def _ret_tables():
    log_g = jnp.log1p(-jnp.exp2(-5.0 - jnp.arange(HEADS, dtype=F32)))
    idx = jnp.arange(CHUNK, dtype=F32)
    diff = idx[:, None] - idx[None, :]
    dmat = jnp.where(diff[None] >= 0, jnp.exp(jnp.maximum(diff, 0.0)[None] * log_g[:, None, None]), 0.0)
    qdec = jnp.exp((idx + 1)[None, :] * log_g[:, None])
    kdec = jnp.exp((CHUNK - 1 - idx)[None, :] * log_g[:, None])
    cdec = jnp.exp(CHUNK * log_g)
    qdec = jnp.broadcast_to(qdec[:, :, None], (HEADS, CHUNK, DK)).astype(BF16)
    kdec = jnp.broadcast_to(kdec[:, :, None], (HEADS, CHUNK, DK)).astype(BF16)
    cdec = jnp.broadcast_to(cdec[:, None, None], (HEADS, 1, DV))
    return dmat, qdec, kdec, cdec


def _mixer_grid(batch, seq):
    rb = min(seq, MIXER_ROWS)
    return rb, seq // rb


def _ret_call(qk, vi, act, tables, s0, batch, seq, pos0, layer=None, side=()):
    dmat, qdec, kdec, cdec = tables
    hb = HEADS_PER_STEP
    rb, nblk = _mixer_grid(batch, seq)
    k_blk = QK_W // (hb * DK)
    kern = functools.partial(_ret_kernel, hb=hb, nchunks=rb // CHUNK, pos0=pos0, n_side=len(side))
    per_head = lambda b, h, r: (h, 0, 0)
    n_hg = HEADS // hb
    side_in, side_out, side_shape = _side_cast_specs(side, layer, batch * n_hg * nblk,
                                                     lambda b, h, r: (b * n_hg + h) * nblk + r)
    res = pl.pallas_call(
        kern,
        grid=(batch, n_hg, nblk),
        in_specs=[
            pl.BlockSpec((rb, hb * DK), lambda b, h, r: (b * nblk + r, h)),
            pl.BlockSpec((rb, hb * DK), lambda b, h, r: (b * nblk + r, k_blk + h)),
            pl.BlockSpec((rb, hb * DV), lambda b, h, r: (b * nblk + r, h)),
            pl.BlockSpec((rb, hb * DV), lambda b, h, r: (b * nblk + r, h)),
            pl.BlockSpec((hb, CHUNK, CHUNK), per_head),
            pl.BlockSpec((hb, CHUNK, DK), per_head),
            pl.BlockSpec((hb, CHUNK, DK), per_head),
            pl.BlockSpec((hb, 1, DV), per_head),
            pl.BlockSpec((hb, DK, DV), per_head),
        ] + side_in,
        out_specs=[
            pl.BlockSpec((rb, hb * DV), lambda b, h, r: (b * nblk + r, h)),
            pl.BlockSpec((None, hb, DK, DV), lambda b, h, r: (b, h, 0, 0)),
        ] + side_out,
        out_shape=[jax.ShapeDtypeStruct((batch * seq, V_W), BF16),
                   jax.ShapeDtypeStruct((batch, HEADS, DK, DV), F32)] + side_shape,
        scratch_shapes=[pltpu.VMEM((hb, DK, DV), F32)],
        compiler_params=_params("parallel", "parallel", "arbitrary", flags=MIXER_FLAGS),
        name="retention",
    )(qk, qk, vi, act, dmat, qdec, kdec, cdec, s0, *side)
    return res[0], res[1], tuple(res[2:])


def _hg_tables():
    t = np.arange(CHUNK)[:, None]
    u = np.arange(CHUNK)[None, :]
    mats, masks = [], []
    for j in range(LEVELS):
        m = 1 << j
        upper = ((t >> j) & 1) == 1
        q_part = upper & (u >= (t & ~(m - 1))) & (u <= t)
        k_part = (~upper) & (u > t) & (u <= (t | (m - 1)))
        if m < SUBLANES:
            mats.append(q_part | k_part)
        masks.append(((t >> (j + 1)) == (u >> (j + 1))) & upper & (((u >> j) & 1) == 0))
    mats.append(u <= t)
    masks.append(t == u)
    mstack = np.concatenate(mats, axis=0).astype(np.float32)
    mstack = np.concatenate([mstack, mstack], axis=1)
    return jnp.asarray(mstack, dtype=BF16), jnp.asarray(np.stack(masks).astype(np.float32))


def _dot_keys(queries, keys, level):
    if level in XLU_TRANSPOSE_LEVELS:
        return jnp.dot(queries, keys.T.astype(BF16), preferred_element_type=F32)
    return lax.dot_general(queries, keys.astype(BF16), NT_DIMS, preferred_element_type=F32)


def _hg_kernel(q_ref, kin_ref, hi_ref, lo_ref, v_ref, g_ref, nw_ref, mstack_ref, masks_ref, s0_ref,
               o_ref, sfin_ref, st_ref, e_ref, *, hb, nchunks):
    @pl.when(pl.program_id(2) == 0)
    def _():
        st_ref[...] = s0_ref[...]

    row = lax.broadcasted_iota(jnp.int32, (CHUNK, 1), 0)

    def exponents(c, slot, pair):
        rows = pl.ds(pl.multiple_of(c * CHUNK, CHUNK), CHUNK)
        lanes = slice(pair * 2 * DK, (pair + 1) * 2 * DK)
        pieces = jnp.concatenate([hi_ref[rows, lanes], lo_ref[rows, lanes]], axis=0)
        e_ref[slot, :, lanes] = jnp.dot(mstack_ref[...], pieces, preferred_element_type=F32)

    def process(c, slot, nxt):
        for g0 in range(0, hb, HG_GROUP):
            process_group(c, slot, nxt, range(g0, g0 + HG_GROUP))

    def process_group(c, slot, nxt, heads):
        rows = pl.ds(pl.multiple_of(c * CHUNK, CHUNK), CHUNK)
        expo = {j: e_ref.at[slot, :, j * DK:(j + 1) * DK] for j in heads}
        qb = {j: q_ref[rows, j * DK:(j + 1) * DK] for j in heads}
        q = {j: qb[j].astype(F32) for j in heads}
        kin = {j: kin_ref[rows, j * DK:(j + 1) * DK].astype(F32) for j in heads}
        b = {j: expo[j][SMALL_LEVELS * CHUNK:, :] for j in heads}
        a = {j: _dot_keys(qb[j], kin[j], LEVELS) * masks_ref[LEVELS] for j in heads}
        a_rows = None
        issue_at = {2 * i: p for i, p in enumerate(range(heads[0] // 2, heads[-1] // 2 + 1))}
        for lev in range(LEVELS):
            m = 1 << lev
            if nxt is not None and lev in issue_at:
                exponents(nxt, 1 - slot, issue_at[lev])
            if m < SUBLANES:
                for j in heads:
                    e = jnp.exp2(expo[j][lev * CHUNK:(lev + 1) * CHUNK, :])
                    x = jnp.where(((row >> lev) & 1) == 1, q[j], kin[j]) * e
                    a[j] = a[j] + _dot_keys(x.astype(BF16), x, lev) * masks_ref[lev]
            else:
                if a_rows is None:
                    a_rows = {j: [a[j][s:s + SUBLANES] for s in range(0, CHUNK, SUBLANES)] for j in heads}
                upper = [s for s0 in range(m, CHUNK, 2 * m) for s in range(s0, s0 + m, SUBLANES)]
                for j in heads:
                    parts = []
                    for s0 in range(0, CHUNK, 2 * m):
                        b_mid = jnp.broadcast_to(b[j][s0 + m - 1:s0 + m, :], (m, DK))
                        parts += [b_mid - b[j][s0:s0 + m], b[j][s0 + m:s0 + 2 * m] - b_mid]
                    e = jnp.exp2(jnp.concatenate(parts, axis=0))
                    qe = jnp.concatenate([q[j][s:s + SUBLANES] * e[s:s + SUBLANES] for s in upper], axis=0)
                    p = _dot_keys(qe.astype(BF16), kin[j] * e, lev)
                    for idx, s in enumerate(upper):
                        a_rows[j][s // SUBLANES] = (a_rows[j][s // SUBLANES]
                                                    + p[idx * SUBLANES:(idx + 1) * SUBLANES]
                                                    * masks_ref[lev, s:s + SUBLANES, :])
        total = {j: b[j][CHUNK - 1:CHUNK, :] for j in heads}
        v = {j: v_ref[rows, j * DV:(j + 1) * DV] for j in heads}
        st = {j: st_ref[j] for j in heads}
        new_st = {j: lax.dot_general((kin[j] * jnp.exp2(total[j] - b[j])).astype(BF16), v[j], TN_DIMS,
                                     preferred_element_type=F32) for j in heads}
        y = {j: jnp.dot(jnp.concatenate([jnp.concatenate(a_rows[j], axis=0).astype(BF16),
                                         (q[j] * jnp.exp2(b[j])).astype(BF16)], axis=1),
                        jnp.concatenate([v[j], st[j].astype(BF16)], axis=0), preferred_element_type=F32)
             for j in heads}
        for j in heads:
            decay = jnp.broadcast_to(jnp.exp2(total[j]), (CHUNK, DK)).T
            st_ref[j] = st[j] * jnp.concatenate([decay] * (DV // DK), axis=1) + new_st[j]
            g = g_ref[rows, j * DV:(j + 1) * DV].astype(F32)
            o_ref[rows, j * DV:(j + 1) * DV] = (_rms_scale(y[j]) * nw_ref[...] * g).astype(o_ref.dtype)

    for pr in range(hb // 2):
        exponents(0, 0, pr)
    if nchunks == 1:
        process(0, 0, None)
    else:
        assert nchunks % 2 == 0

        def pair(p, carry):
            c = 2 * p
            process(c, 0, c + 1)
            process(c + 1, 1, jnp.minimum(c + 2, nchunks - 1))
            return carry

        lax.fori_loop(0, nchunks // 2, pair, 0)
    sfin_ref[...] = st_ref[...]


def _hg_call(act, kin, lf_hi, lf_lo, vi, nw_row, tables, s0, batch, seq):
    mstack, masks = tables
    hb = HEADS_PER_STEP
    rb, nblk = _mixer_grid(batch, seq)
    q_blk = V_W // (hb * DK)
    g_blk = (V_W + QK_W) // (hb * DV)
    i_blk = V_W // (hb * DV)
    kern = functools.partial(_hg_kernel, hb=hb, nchunks=rb // CHUNK)
    row_blk = lambda off: (lambda b, h, r: (b * nblk + r, off + h))
    return pl.pallas_call(
        kern,
        grid=(batch, HEADS // hb, nblk),
        in_specs=[
            pl.BlockSpec((rb, hb * DK), row_blk(q_blk)),
            pl.BlockSpec((rb, hb * DK), row_blk(0)),
            pl.BlockSpec((rb, hb * DK), row_blk(0)),
            pl.BlockSpec((rb, hb * DK), row_blk(0)),
            pl.BlockSpec((rb, hb * DV), row_blk(i_blk)),
            pl.BlockSpec((rb, hb * DV), row_blk(g_blk)),
            pl.BlockSpec((1, DV), lambda b, h, r: (0, 0)),
            pl.BlockSpec(mstack.shape, lambda b, h, r: (0, 0)),
            pl.BlockSpec(masks.shape, lambda b, h, r: (0, 0, 0)),
            pl.BlockSpec((hb, DK, DV), lambda b, h, r: (h, 0, 0)),
        ],
        out_specs=[
            pl.BlockSpec((rb, hb * DV), row_blk(0)),
            pl.BlockSpec((None, hb, DK, DV), lambda b, h, r: (b, h, 0, 0)),
        ],
        out_shape=[jax.ShapeDtypeStruct((batch * seq, V_W), BF16),
                   jax.ShapeDtypeStruct((batch, HEADS, DK, DV), F32)],
        scratch_shapes=[pltpu.VMEM((hb, DK, DV), F32),
                        pltpu.VMEM((2, (SMALL_LEVELS + 1) * CHUNK, hb * DK), F32)],
        compiler_params=_params("parallel", "parallel", "arbitrary", flags=MIXER_FLAGS),
        name="hgrn2",
    )(act, kin, lf_hi, lf_lo, vi, act, nw_row, mstack, masks, s0)


def _merge_kernel(yr_ref, yh_ref, wr_ref, wh_ref, gr_ref, gh_ref, o_ref):
    a = jnp.dot(yr_ref[...], wr_ref[...], preferred_element_type=F32)
    b = jnp.dot(yh_ref[...], wh_ref[...], preferred_element_type=F32)
    o_ref[...] = (gr_ref[...].astype(F32) * a + gh_ref[...].astype(F32) * b).astype(o_ref.dtype)


def _merge_call(yr, yh, gates, w_ret, w_hg):
    rows, k = yr.shape
    d = w_ret.shape[-1]
    tm = min(rows, ROW_TILE)
    tn = MERGE_COL_TILE
    gh_blk = d // tn
    return pl.pallas_call(
        _merge_kernel,
        grid=(rows // tm, d // tn),
        in_specs=[
            pl.BlockSpec((tm, k), lambda i, j: (i, 0)),
            pl.BlockSpec((tm, k), lambda i, j: (i, 0)),
            pl.BlockSpec((k, tn), lambda i, j: (0, j)),
            pl.BlockSpec((k, tn), lambda i, j: (0, j)),
            pl.BlockSpec((tm, tn), lambda i, j: (i, j)),
            pl.BlockSpec((tm, tn), lambda i, j: (i, gh_blk + j)),
        ],
        out_specs=pl.BlockSpec((tm, tn), lambda i, j: (i, j)),
        out_shape=jax.ShapeDtypeStruct((rows, d), BF16),
        compiler_params=_params("parallel", "arbitrary"),
        name="branch_merge",
    )(yr, yh, w_ret, w_hg, gates, gates)


def _out_kernel(y_ref, w_ref, h_ref, post_ref, nxt_ref, *rest, n_side):
    hn_ref, xn_ref = rest[n_side:n_side + 2]
    _side_cast(rest[:n_side], rest[n_side + 2:])
    m = jnp.dot(y_ref[...], w_ref[...], preferred_element_type=F32)
    hn = h_ref[...] + _rms_scale(m) * post_ref[...]
    hn_ref[...] = hn
    xn_ref[...] = (_rms_scale(hn) * nxt_ref[...]).astype(xn_ref.dtype)


def _out_call(y, w_out, h, post_row, next_row, layer=None, side=()):
    rows, d = h.shape
    tm = min(rows, FFN_ROW_TILE)
    row_blk = pl.BlockSpec((tm, d), lambda i: (i, 0))
    vec_blk = pl.BlockSpec((1, d), lambda i: (0, 0))
    side_in, side_out, side_shape = _side_cast_specs(side, layer, rows // tm, lambda i: i)
    res = pl.pallas_call(
        functools.partial(_out_kernel, n_side=len(side)),
        grid=(rows // tm,),
        in_specs=[row_blk, pl.BlockSpec((d, d), lambda i: (0, 0)), row_blk, vec_blk, vec_blk] + side_in,
        out_specs=[row_blk, row_blk] + side_out,
        out_shape=[jax.ShapeDtypeStruct((rows, d), F32), jax.ShapeDtypeStruct((rows, d), BF16)] + side_shape,
        compiler_params=_params("parallel"),
        name="out_proj",
    )(y, w_out, h, post_row, next_row, *side)
    return res[0], res[1], tuple(res[2:])


def _ffn_kernel(x_ref, wg_ref, wu_ref, wd_ref, h_ref, post_ref, nxt_ref, hn_ref, *rest, with_next):
    if with_next:
        xn_ref, acc_ref = rest
    else:
        (acc_ref,) = rest
    f = pl.program_id(1)

    @pl.when(f == 0)
    def _():
        acc_ref[...] = jnp.zeros_like(acc_ref)

    x = x_ref[...]
    g = jnp.dot(x, wg_ref[...], preferred_element_type=F32)
    u = jnp.dot(x, wu_ref[...], preferred_element_type=F32)
    act = (_silu(g) * u).astype(BF16)
    acc_ref[...] += jnp.dot(act, wd_ref[...], preferred_element_type=F32)

    @pl.when(f == pl.num_programs(1) - 1)
    def _():
        hn = h_ref[...] + _rms_scale(acc_ref[...]) * post_ref[...]
        hn_ref[...] = hn
        if with_next:
            xn_ref[...] = (_rms_scale(hn) * nxt_ref[...]).astype(xn_ref.dtype)


def _ffn_call(xn, w_gate, w_up, w_down, h, post_row, next_row, with_next):
    rows, d = h.shape
    d_ff = w_gate.shape[-1]
    tm = min(rows, FFN_ROW_TILE)
    tf = FFN_COL_TILE
    row_blk = pl.BlockSpec((tm, d), lambda i, f: (i, 0))
    vec_blk = pl.BlockSpec((1, d), lambda i, f: (0, 0))
    n_state = 2 if with_next else 1
    res = pl.pallas_call(
        functools.partial(_ffn_kernel, with_next=with_next),
        grid=(rows // tm, d_ff // tf),
        in_specs=[
            row_blk,
            pl.BlockSpec((d, tf), lambda i, f: (0, f)),
            pl.BlockSpec((d, tf), lambda i, f: (0, f)),
            pl.BlockSpec((tf, d), lambda i, f: (f, 0)),
            row_blk, vec_blk, vec_blk,
        ],
        out_specs=[row_blk] * n_state,
        out_shape=[jax.ShapeDtypeStruct((rows, d), F32), jax.ShapeDtypeStruct((rows, d), BF16)][:n_state],
        scratch_shapes=[pltpu.VMEM((tm, d), F32)],
        compiler_params=_params("parallel", "arbitrary"),
        name="swiglu_ffn",
    )(xn, w_gate, w_up, w_down, h, post_row, next_row)
    return res[0], (res[1] if with_next else None)


def kernel(x, meta_tokens, norm_mix_pre, norm_mix_post, norm_ffn_pre, norm_ffn_post, w_in, hg_lb_logits,
           hg_norm_w, w_br_ret, w_br_hg, w_out, w_ffn_gate, w_ffn_up, w_ffn_down):
    batch, seq, d = x.shape
    depth = w_in.shape[0]
    assert seq % CHUNK == 0 and meta_tokens.shape == (N_META, d)
    assert w_in.shape[-1] == sum(_IN_WIDTHS) and d == V_W

    lb_sm = jax.nn.softmax(hg_lb_logits.astype(F32), axis=0)
    lbs = jnp.cumsum(lb_sm, axis=0) - lb_sm[0:1]

    hg_tables = _hg_tables()
    ret_tables = _ret_tables()
    meta_h = jnp.concatenate([jnp.zeros((PAD, d), F32), meta_tokens.astype(F32)], axis=0)
    meta = dict(h=meta_h, batch=1, seq=CHUNK, pos0=-PAD)
    main = dict(h=x.reshape(batch * seq, d).astype(F32), batch=batch, seq=seq, pos0=N_META)
    for rs in (meta, main):
        rs["rot"] = _rot_tables(rs["seq"], rs["pos0"])
        rs["xn"] = _norm_call(rs["h"], norm_mix_pre[0][None])

    for l in range(depth):
        last = l == depth - 1
        proj = functools.partial(_proj_call, main["xn"], meta["xn"], w_in, l, seq=seq)
        main["qk"], meta["qk"], _ = proj(ROT_TILES, "rot", extra=main["rot"], meta_extra=meta["rot"])
        main["vi"], meta["vi"], (w_ret_b, w_hg_b, w_out_b) = proj(IDENT_TILES, "ident",
                                                                 side=(w_br_ret, w_br_hg, w_out))
        main["act"], meta["act"], _ = proj(SILU_TILES, "silu")
        main["forget"], meta["forget"], _ = proj(FORGET_TILES, "forget", extra=(lbs[l][None],))
        main["gates"], meta["gates"], (w_gate_b,) = _proj_call(
            main["xn"], None if last else meta["xn"], w_in, l, SIGMOID_TILES, "sigmoid", seq, side=(w_ffn_gate,))

        ret_state = jnp.zeros((HEADS, DK, DV), F32)
        hg_state = jnp.zeros((HEADS, DK, DV), F32)
        for rs in (meta, main):
            kin, lf_hi, lf_lo = rs["forget"]
            side = () if rs is meta else (w_ffn_up,)
            rs["yr"], ret_fin, casts = _ret_call(rs["qk"], rs["vi"], rs["act"], ret_tables, ret_state,
                                                 rs["batch"], rs["seq"], rs["pos0"], l, side)
            if rs is main:
                (w_up_b,) = casts
            rs["yh"], hg_fin = _hg_call(rs["act"], kin, lf_hi, lf_lo, rs["vi"], hg_norm_w[l][None], hg_tables,
                                        hg_state, rs["batch"], rs["seq"])
            if rs is meta:
                ret_state, hg_state = ret_fin[0], hg_fin[0]

        next_row = norm_mix_pre[min(l + 1, depth - 1)][None]
        post_row = norm_ffn_post[l][None]
        w_down_b = None
        for rs in (main,) if last else (main, meta):
            y = _merge_call(rs["yr"], rs["yh"], rs["gates"], w_ret_b, w_hg_b)
            side = () if rs is meta else (w_ffn_down,)
            h_mid, xn_ffn, casts = _out_call(y, w_out_b, rs["h"], norm_mix_post[l][None], norm_ffn_pre[l][None],
                                             l, side)
            if rs is main:
                (w_down_b,) = casts
            rs["h"], rs["xn"] = _ffn_call(xn_ffn, w_gate_b, w_up_b, w_down_b, h_mid, post_row, next_row, not last)
    return main["h"].reshape(batch, seq, d)
```
